```python
import math
import jax, jax.numpy as jnp
from jax import lax
import numpy as np

D_MODEL = 1024
BATCH = 8
SEQ = 2048
DEPTH = 4
DEC_BATCH = 32
DEC_SEQ = 16
PAST_LEN = 4096

CHUNK = 64
N_A_LAYERS = DEPTH // 2
N_B_LAYERS = DEPTH - N_A_LAYERS
SSM_EXPAND = 2
SSM_INNER = SSM_EXPAND * D_MODEL
SSM_HEAD_DIM = 64
SSM_HEADS = SSM_INNER // SSM_HEAD_DIM
SSM_STATE = 128
SSM_GROUPS = 8
SSM_HEADS_PER_GROUP = SSM_HEADS // SSM_GROUPS
CONV_WIDTH = 4
CONV_DIM = SSM_INNER + 2 * SSM_GROUPS * SSM_STATE
SSM_IN_DIM = SSM_INNER + CONV_DIM + SSM_HEADS
WINDOW = 128
WIN_CHUNKS = WINDOW // CHUNK
ATTN_HEADS = 16
KV_HEADS = 4
ATTN_HEAD_DIM = 64
Q_PER_KV = ATTN_HEADS // KV_HEADS
KV_DIM = KV_HEADS * ATTN_HEAD_DIM
PEER_HEADS = 8
PEER_NKEYS = 128
PEER_EXPERTS = PEER_NKEYS * PEER_NKEYS
PEER_QDIM = 256
PEER_HALF = PEER_QDIM // 2
PEER_TOPK = 16
PEER_TOKEN_BLOCK = 128
EPS = 1e-6

kernel_name = 'yoco_ssd_swa_sink_peer_stream_step'


def rmsnorm(x, g):
    xf = x.astype(jnp.float32)
    y = xf * lax.rsqrt(jnp.mean(xf * xf, axis=-1, keepdims=True) + EPS)
    return (y * g.astype(jnp.float32)).astype(x.dtype)


def causal_conv(xbc, prev, w, b):
    L = xbc.shape[1]
    xp = jnp.concatenate([prev.astype(xbc.dtype), xbc], axis=1)
    out = b
    for k in range(CONV_WIDTH):
        out = out + xp[:, k:k + L] * w[k]
    return jax.nn.silu(out), xp[:, -(CONV_WIDTH - 1):]


def ssd_scan(x, dt, A, B, C, s0, block):
    f32 = jnp.float32
    bsz, L = x.shape[:2]
    nc, Q, G, R = L // block, block, SSM_GROUPS, SSM_HEADS_PER_GROUP
    x = x.astype(f32).reshape(bsz, nc, Q, G, R, SSM_HEAD_DIM)
    dt = dt.astype(f32).reshape(bsz, nc, Q, G, R)
    B = B.astype(f32).reshape(bsz, nc, Q, G, SSM_STATE)
    C = C.astype(f32).reshape(bsz, nc, Q, G, SSM_STATE)
    s0 = s0.astype(f32).reshape(bsz, G, R, SSM_HEAD_DIM, SSM_STATE)
    acum = jnp.cumsum(dt * A.reshape(G, R), axis=2)
    seg = acum[:, :, :, None] - acum[:, :, None, :]
    causal = jnp.tril(jnp.ones((Q, Q), bool))[:, :, None, None]
    lmat = jnp.exp(jnp.where(causal, seg, -jnp.inf))
    xdt = x * dt[..., None]
    cb = jnp.einsum('bcign,bcjgn->bcijg', C, B)
    y_diag = jnp.einsum('bcijgr,bcjgrp->bcigrp', cb[..., None] * lmat, xdt)
    decay_end = jnp.exp(acum[:, :, -1:] - acum)
    states = jnp.einsum('bcjgn,bcjgrp->bcgrpn', B, xdt * decay_end[..., None])
    chunk_decay = jnp.exp(acum[:, :, -1])

    def step(s, inp):
        dec, st = inp
        return dec[..., None, None] * s + st, s

    s_fin, s_prev = lax.scan(step, s0, (jnp.moveaxis(chunk_decay, 1, 0), jnp.moveaxis(states, 1, 0)))
    s_prev = jnp.moveaxis(s_prev, 0, 1)
    y_off = jnp.einsum('bcign,bcgrpn->bcigrp', C, s_prev) * jnp.exp(acum)[..., None]
    y = (y_diag + y_off).reshape(bsz, L, SSM_HEADS, SSM_HEAD_DIM)
    return y, s_fin.reshape(bsz, SSM_HEADS, SSM_HEAD_DIM, SSM_STATE)


def mamba_mixer(xn, ssm_prev, conv_prev, w_in, conv_w, conv_b, dt_bias, a_log, d_skip, norm_g, w_out):
    bsz, L, _ = xn.shape
    f32 = jnp.float32
    proj = xn @ w_in
    z = proj[..., :SSM_INNER]
    xbc = proj[..., SSM_INNER:SSM_INNER + CONV_DIM]
    dt_raw = proj[..., SSM_INNER + CONV_DIM:]
    xbc, conv_new = causal_conv(xbc, conv_prev, conv_w, conv_b)
    gn = SSM_GROUPS * SSM_STATE
    xs = xbc[..., :SSM_INNER].reshape(bsz, L, SSM_HEADS, SSM_HEAD_DIM)
    Bm = xbc[..., SSM_INNER:SSM_INNER + gn].reshape(bsz, L, SSM_GROUPS, SSM_STATE)
    Cm = xbc[..., SSM_INNER + gn:].reshape(bsz, L, SSM_GROUPS, SSM_STATE)
    dt = jax.nn.softplus(dt_raw.astype(f32) + dt_bias.astype(f32))
    A = -jnp.exp(a_log.astype(f32))
    y, s_new = ssd_scan(xs, dt, A, Bm, Cm, ssm_prev, min(CHUNK, L))
    y = y + xs.astype(f32) * d_skip.astype(f32)[:, None]
    y = y.reshape(bsz, L, SSM_INNER) * jax.nn.silu(z.astype(f32))
    y = rmsnorm(y, norm_g).astype(xn.dtype)
    return y @ w_out, s_new.astype(ssm_prev.dtype), conv_new


def sink_softmax(scores, sinks):
    s = scores.astype(jnp.float32)
    sink = jnp.broadcast_to(sinks.astype(jnp.float32).reshape(KV_HEADS, Q_PER_KV, 1, 1), s.shape[:-1] + (1,))
    p = jax.nn.softmax(jnp.concatenate([s, sink], axis=-1), axis=-1)
    return p[..., :-1]


def swa_prompt(q, k, v, sinks):
    bsz, L = q.shape[:2]
    nc = L // CHUNK
    nk = (WIN_CHUNKS + 1) * CHUNK
    qb = q.reshape(bsz, nc, CHUNK, KV_HEADS, Q_PER_KV, ATTN_HEAD_DIM)
    pad = ((0, 0), (WINDOW, 0), (0, 0), (0, 0))
    kp = jnp.pad(k, pad).reshape(bsz, nc + WIN_CHUNKS, CHUNK, KV_HEADS, ATTN_HEAD_DIM)
    vp = jnp.pad(v, pad).reshape(bsz, nc + WIN_CHUNKS, CHUNK, KV_HEADS, ATTN_HEAD_DIM)
    kb = jnp.concatenate([kp[:, o:o + nc] for o in range(WIN_CHUNKS + 1)], axis=2)
    vb = jnp.concatenate([vp[:, o:o + nc] for o in range(WIN_CHUNKS + 1)], axis=2)
    scale = ATTN_HEAD_DIM ** -0.5
    scores = jnp.einsum('bcqkgd,bcskd->bckgqs', qb, kb).astype(jnp.float32) * scale
    key_chunk = jnp.arange(nc)[:, None] - WIN_CHUNKS + (jnp.arange(nk) // CHUNK)[None, :]
    valid = (key_chunk >= 0)[None, :, None, None, None, :]
    probs = sink_softmax(jnp.where(valid, scores, -jnp.inf), sinks)
    out = jnp.einsum('bckgqs,bcskd->bcqkgd', probs.astype(v.dtype), vb)
    return out.reshape(bsz, L, ATTN_HEADS * ATTN_HEAD_DIM)


def swa_sample(q, k_all, v_all, sinks):
    bsz, T = q.shape[:2]
    qh = q.reshape(bsz, T, KV_HEADS, Q_PER_KV, ATTN_HEAD_DIM)
    scale = ATTN_HEAD_DIM ** -0.5
    scores = jnp.einsum('btkgd,bskd->bkgts', qh, k_all).astype(jnp.float32) * scale
    probs = sink_softmax(scores, sinks)
    out = jnp.einsum('bkgts,bskd->btkgd', probs.astype(v_all.dtype), v_all)
    return out.reshape(bsz, T, ATTN_HEADS * ATTN_HEAD_DIM)


def peer(xn, w_q, sub_k1, sub_k2, u_tab, v_tab):
    bsz, L, D = xn.shape
    T = bsz * L
    nblk = -(-T // PEER_TOKEN_BLOCK)
    xt = jnp.pad(xn.reshape(T, D), ((0, nblk * PEER_TOKEN_BLOCK - T), (0, 0)))
    xt = xt.reshape(nblk, PEER_TOKEN_BLOCK, D)

    def block_fn(xb):
        q = (xb @ w_q).reshape(PEER_TOKEN_BLOCK, PEER_HEADS, PEER_QDIM)
        s1 = jnp.einsum('thd,nd->thn', q[..., :PEER_HALF], sub_k1)
        s2 = jnp.einsum('thd,nd->thn', q[..., PEER_HALF:], sub_k2)
        v1, i1 = lax.top_k(s1, PEER_TOPK)
        v2, i2 = lax.top_k(s2, PEER_TOPK)
        cand = (v1[..., :, None] + v2[..., None, :]).reshape(PEER_TOKEN_BLOCK, PEER_HEADS, PEER_TOPK * PEER_TOPK)
        vs, ci = lax.top_k(cand, PEER_TOPK)
        e1 = jnp.take_along_axis(i1, ci // PEER_TOPK, axis=-1)
        e2 = jnp.take_along_axis(i2, ci % PEER_TOPK, axis=-1)
        eid = (e1 * PEER_NKEYS + e2).reshape(PEER_TOKEN_BLOCK, PEER_HEADS * PEER_TOPK)
        gate = jax.nn.softmax(vs.astype(jnp.float32), axis=-1).reshape(PEER_TOKEN_BLOCK, PEER_HEADS * PEER_TOPK)
        ue = jnp.take(u_tab, eid, axis=0)
        act = jax.nn.gelu(jnp.einsum('tkd,td->tk', ue, xb).astype(jnp.float32))
        ve = jnp.take(v_tab, eid, axis=0)
        return jnp.einsum('tk,tkd->td', (gate * act).astype(xb.dtype), ve)

    out = lax.map(block_fn, xt)
    return out.reshape(nblk * PEER_TOKEN_BLOCK, D)[:T].reshape(bsz, L, D)


def setup_inputs(seed: int = 0) -> dict:
    key = jax.random.key(seed)
    ks = iter(jax.random.split(key, 48))

    def nrm(shape, scale):
        return jax.random.normal(next(ks), shape, jnp.float32) * scale

    D = D_MODEL
    HQD = ATTN_HEADS * ATTN_HEAD_DIM
    u_dt = jax.random.uniform(next(ks), (N_A_LAYERS, SSM_HEADS), jnp.float32)
    dt0 = jnp.exp(u_dt * (math.log(0.1) - math.log(0.001)) + math.log(0.001))
    return {
        'x_prompt': nrm((BATCH, SEQ, D), 1.0),
        'x_sample': nrm((DEC_BATCH, DEC_SEQ, D), 1.0),
        'state_ssm': nrm((N_A_LAYERS, DEC_BATCH, SSM_HEADS, SSM_HEAD_DIM, SSM_STATE), 0.1),
        'state_conv': nrm((N_A_LAYERS, DEC_BATCH, CONV_WIDTH - 1, CONV_DIM), 1.0),
        'cache_k_win': nrm((DEC_BATCH, WINDOW, KV_HEADS, ATTN_HEAD_DIM), 1.0),
        'cache_v_win': nrm((DEC_BATCH, WINDOW, KV_HEADS, ATTN_HEAD_DIM), 1.0),
        'norm_mix': 1.0 + nrm((DEPTH, D), 0.05),
        'norm_ffn': 1.0 + nrm((DEPTH, D), 0.05),
        'norm_kv': 1.0 + nrm((D,), 0.05),
        'norm_final': 1.0 + nrm((D,), 0.05),
        'm_w_in': nrm((N_A_LAYERS, D, SSM_IN_DIM), D ** -0.5),
        'm_conv_w': nrm((N_A_LAYERS, CONV_WIDTH, CONV_DIM), CONV_WIDTH ** -0.5),
        'm_conv_b': nrm((N_A_LAYERS, CONV_DIM), 0.01),
        'm_dt_bias': dt0 + jnp.log(-jnp.expm1(-dt0)),
        'm_a_log': jnp.log(jax.random.uniform(next(ks), (N_A_LAYERS, SSM_HEADS), jnp.float32, 1.0, 16.0)),
        'm_d_skip': 1.0 + nrm((N_A_LAYERS, SSM_HEADS), 0.1),
        'm_norm': 1.0 + nrm((N_A_LAYERS, SSM_INNER), 0.05),
        'm_w_out': nrm((N_A_LAYERS, SSM_INNER, D), SSM_INNER ** -0.5),
        'a_w_kv': nrm((D, 2 * KV_DIM), D ** -0.5),
        'a_b_kv': nrm((2 * KV_DIM,), 0.01),
        'a_w_q': nrm((N_B_LAYERS, D, HQD), D ** -0.5),
        'a_b_q': nrm((N_B_LAYERS, HQD), 0.01),
        'a_sinks': nrm((N_B_LAYERS, ATTN_HEADS), 0.5),
        'a_w_o': nrm((N_B_LAYERS, HQD, D), HQD ** -0.5),
        'a_b_o': nrm((N_B_LAYERS, D), 0.01),
        'p_w_q': nrm((DEPTH, D, PEER_HEADS * PEER_QDIM), D ** -0.5),
        'p_sub_k1': nrm((DEPTH, PEER_NKEYS, PEER_HALF), PEER_HALF ** -0.5),
        'p_sub_k2': nrm((DEPTH, PEER_NKEYS, PEER_HALF), PEER_HALF ** -0.5),
        'p_u': nrm((DEPTH, PEER_EXPERTS, D), D ** -0.5),
        'p_v': nrm((DEPTH, PEER_EXPERTS, D), (PEER_HEADS * PEER_TOPK) ** -0.5),
    }


def reference(x_prompt, x_sample, state_ssm, state_conv, cache_k_win, cache_v_win,
              norm_mix, norm_ffn, norm_kv, norm_final,
              m_w_in, m_conv_w, m_conv_b, m_dt_bias, m_a_log, m_d_skip, m_norm, m_w_out,
              a_w_kv, a_b_kv, a_w_q, a_b_q, a_sinks, a_w_o, a_b_o,
              p_w_q, p_sub_k1, p_sub_k2, p_u, p_v):

    def run(h, ssm_init, conv_init, k_prev, v_prev):
        bsz, L, _ = h.shape
        ssm_out, conv_out = [], []
        k_new = v_new = k_all = v_all = None
        for l in range(DEPTH):
            xn = rmsnorm(h, norm_mix[l])
            if l < N_A_LAYERS:
                mix, s_new, c_new = mamba_mixer(xn, ssm_init[l], conv_init[l], m_w_in[l], m_conv_w[l],
                                                m_conv_b[l], m_dt_bias[l], m_a_log[l], m_d_skip[l],
                                                m_norm[l], m_w_out[l])
                ssm_out.append(s_new)
                conv_out.append(c_new)
            else:
                j = l - N_A_LAYERS
                if j == 0:
                    kv = rmsnorm(h, norm_kv) @ a_w_kv + a_b_kv
                    k_new = kv[..., :KV_DIM].reshape(bsz, L, KV_HEADS, ATTN_HEAD_DIM)
                    v_new = kv[..., KV_DIM:].reshape(bsz, L, KV_HEADS, ATTN_HEAD_DIM)
                    if k_prev is not None:
                        k_all = jnp.concatenate([k_prev.astype(k_new.dtype), k_new], axis=1)
                        v_all = jnp.concatenate([v_prev.astype(v_new.dtype), v_new], axis=1)
                q = xn @ a_w_q[j] + a_b_q[j]
                if k_prev is None:
                    o = swa_prompt(q, k_new, v_new, a_sinks[j])
                else:
                    o = swa_sample(q, k_all, v_all, a_sinks[j])
                mix = o @ a_w_o[j] + a_b_o[j]
            h = h + mix
            h = h + peer(rmsnorm(h, norm_ffn[l]), p_w_q[l], p_sub_k1[l], p_sub_k2[l], p_u[l], p_v[l])
        y = rmsnorm(h, norm_final)
        if k_prev is None:
            k_win, v_win = k_new[:, -WINDOW:], v_new[:, -WINDOW:]
        else:
            k_win, v_win = k_all[:, -WINDOW:], v_all[:, -WINDOW:]
        return y, jnp.stack(ssm_out), jnp.stack(conv_out), k_win, v_win

    zeros_ssm = jnp.zeros((N_A_LAYERS, x_prompt.shape[0], SSM_HEADS, SSM_HEAD_DIM, SSM_STATE), x_prompt.dtype)
    zeros_conv = jnp.zeros((N_A_LAYERS, x_prompt.shape[0], CONV_WIDTH - 1, CONV_DIM), x_prompt.dtype)
    y_prompt, pr_ssm, pr_conv, pr_k_win, pr_v_win = run(x_prompt, zeros_ssm, zeros_conv, None, None)
    y_sample, sm_ssm, sm_conv, sm_k_win, sm_v_win = run(x_sample, state_ssm, state_conv, cache_k_win, cache_v_win)
    return (y_prompt, y_sample, pr_ssm, pr_conv, pr_k_win, pr_v_win, sm_ssm, sm_conv, sm_k_win, sm_v_win)
```

```python
import functools

import jax
import jax.numpy as jnp
from jax import lax
from jax.experimental import pallas as pl
from jax.experimental.pallas import tpu as pltpu

F32 = jnp.float32
BF16 = jnp.bfloat16
EPS = 1e-6

WINDOW = 128
SSM_HEAD_DIM = 64
SSM_STATE = 128
SSM_GROUPS = 8
CONV_WIDTH = 4
SSD_BLOCK = 128
ATTN_CHUNK = 64
ATTN_HEAD_DIM = 64
KV_HEADS = 4
PEER_HEADS = 8
PEER_NKEYS = 128
PEER_TOPK = 16
LANES = 128
SUBLANES = 8
MIB = 1024 * 1024

_NT = (((1,), (1,)), ((), ()))


def _cparams(n_axes, vmem_mib):
    return pltpu.CompilerParams(
        dimension_semantics=("arbitrary",) * n_axes, vmem_limit_bytes=vmem_mib * MIB)


def _tile(n, prefs):
    for p in prefs:
        if n % p == 0:
            return p
    return n


def _rms(x, g):
    ms = jnp.mean(x * x, axis=-1, keepdims=True)
    return x * lax.rsqrt(ms + EPS) * g


def _norm_matmul_kernel(x_ref, g_ref, w_ref, b_ref, o_ref, xn_ref):
    @pl.when(pl.program_id(1) == 0)
    def _():
        xn_ref[...] = _rms(x_ref[...], g_ref[...]).astype(BF16)

    o_ref[...] = jnp.dot(xn_ref[...], w_ref[...], preferred_element_type=F32) + b_ref[...]


def norm_matmul(x, g, w, b, tn, head_major=False):
    t, d = x.shape
    n = w.shape[1]
    tm = _tile(t, (512, 256))
    if head_major:
        out_shape = jax.ShapeDtypeStruct((n // tn, t, tn), F32)
        out_spec = pl.BlockSpec((None, tm, tn), lambda i, j: (j, i, 0))
    else:
        out_shape = jax.ShapeDtypeStruct((t, n), F32)
        out_spec = pl.BlockSpec((tm, tn), lambda i, j: (i, j))
    return pl.pallas_call(
        _norm_matmul_kernel,
        grid=(t // tm, n // tn),
        in_specs=[pl.BlockSpec((tm, d), lambda i, j: (i, 0)),
                  pl.BlockSpec((1, d), lambda i, j: (0, 0)),
                  pl.BlockSpec((d, tn), lambda i, j: (0, j)),
                  pl.BlockSpec((1, tn), lambda i, j: (0, j))],
        out_specs=out_spec,
        out_shape=out_shape,
        scratch_shapes=[pltpu.VMEM((tm, d), BF16)],
        compiler_params=_cparams(2, 32),
        name="norm_matmul",
    )(x, g.reshape(1, d), w, b.reshape(1, n))


def _matmul_res_kernel(x_ref, w_ref, b_ref, r_ref, o_ref):
    mix = jnp.dot(x_ref[...].astype(BF16), w_ref[...], preferred_element_type=F32) + b_ref[...]
    o_ref[...] = r_ref[...] + mix


def matmul_bias_res(x, w, b, res):
    t, k = x.shape
    n = w.shape[1]
    tm = _tile(t, (512, 256))
    return pl.pallas_call(
        _matmul_res_kernel,
        grid=(t // tm,),
        in_specs=[pl.BlockSpec((tm, k), lambda i: (i, 0)),
                  pl.BlockSpec((k, n), lambda i: (0, 0)),
                  pl.BlockSpec((1, n), lambda i: (0, 0)),
                  pl.BlockSpec((tm, n), lambda i: (i, 0))],
        out_specs=pl.BlockSpec((tm, n), lambda i: (i, 0)),
        out_shape=jax.ShapeDtypeStruct((t, n), F32),
        compiler_params=_cparams(1, 32),
        name="matmul_bias_res",
    )(x, w, b.reshape(1, n), res)


def _rmsnorm_kernel(x_ref, g_ref, o_ref):
    o_ref[...] = _rms(x_ref[...], g_ref[...])


def rmsnorm_rows(x, g):
    t, d = x.shape
    tm = _tile(t, (512, 256))
    return pl.pallas_call(
        _rmsnorm_kernel,
        grid=(t // tm,),
        in_specs=[pl.BlockSpec((tm, d), lambda i: (i, 0)), pl.BlockSpec((1, d), lambda i: (0, 0))],
        out_specs=pl.BlockSpec((tm, d), lambda i: (i, 0)),
        out_shape=jax.ShapeDtypeStruct((t, d), F32),
        compiler_params=_cparams(1, 32),
        name="rmsnorm",
    )(x, g.reshape(1, d))


def _conv_kernel(x_ref, halo_ref, prev_ref, w_ref, b_ref, o_ref, ext_ref, *, tl):
    first = pl.program_id(1) == 0
    ext_ref[0:SUBLANES, :] = jnp.where(first, prev_ref[...], halo_ref[...])
    ext_ref[SUBLANES:SUBLANES + tl, :] = x_ref[...]
    acc = b_ref[...]
    for k in range(CONV_WIDTH):
        off = SUBLANES - (CONV_WIDTH - 1) + k
        acc = acc + ext_ref[off:off + tl, :] * w_ref[k:k + 1, :]
    o_ref[...] = acc * jax.nn.sigmoid(acc)


def causal_conv_silu(proj, prev8, w, b, bsz, seq, col0):
    c = w.shape[1]
    tl = _tile(seq, (256, 128, 64, 32, 16))
    tc = 1024
    nl = seq // tl
    cb0 = col0 // tc
    hpb = tl // SUBLANES
    return pl.pallas_call(
        functools.partial(_conv_kernel, tl=tl),
        grid=(bsz, nl, c // tc),
        in_specs=[pl.BlockSpec((tl, tc), lambda bi, li, ci: (bi * nl + li, cb0 + ci)),
                  pl.BlockSpec((SUBLANES, tc),
                               lambda bi, li, ci: (jnp.maximum((bi * nl + li) * hpb - 1, 0), cb0 + ci)),
                  pl.BlockSpec((None, SUBLANES, tc), lambda bi, li, ci: (bi, 0, ci)),
                  pl.BlockSpec((CONV_WIDTH, tc), lambda bi, li, ci: (0, ci)),
                  pl.BlockSpec((1, tc), lambda bi, li, ci: (0, ci))],
        out_specs=pl.BlockSpec((tl, tc), lambda bi, li, ci: (bi * nl + li, ci)),
        out_shape=jax.ShapeDtypeStruct((bsz * seq, c), F32),
        scratch_shapes=[pltpu.VMEM((SUBLANES + tl, tc), F32)],
        compiler_params=_cparams(3, 32),
        name="conv_silu",
    )(proj, proj, prev8, w, b.reshape(1, c))


def _split3(x):
    hi = x.astype(BF16)
    r1 = x - hi.astype(F32)
    mid = r1.astype(BF16)
    lo = (r1 - mid.astype(F32)).astype(BF16)
    return hi, mid, lo


def _ssd_kernel(*refs, rows, has_init):
    if has_init:
        (x_ref, b_ref, c_ref, dt_ref, bias_ref, alog_ref, dskip_ref, s0_ref,
         y_ref, sout_ref, st_ref) = refs
    else:
        (x_ref, b_ref, c_ref, dt_ref, bias_ref, alog_ref, dskip_ref,
         y_ref, sout_ref, st_ref) = refs
    q = SSD_BLOCK
    gw = x_ref.shape[1] // SSM_GROUPS
    hpg = gw // SSM_HEAD_DIM
    ci = pl.program_id(1)

    @pl.when(ci == 0)
    def _():
        for g in range(SSM_GROUPS):
            if has_init:
                st_ref[g] = s0_ref[g * gw:(g + 1) * gw, :].T
            else:
                st_ref[g] = jnp.zeros((SSM_STATE, gw), F32)

    def pad(a):
        if rows == q:
            return a
        return jnp.concatenate([a, jnp.zeros((q - rows, a.shape[1]), a.dtype)], axis=0)

    raw = dt_ref[...] + bias_ref[...]
    dt = pad(jnp.maximum(raw, 0.0) + jnp.log1p(jnp.exp(-jnp.abs(raw))))
    a_neg = -jnp.exp(alog_ref[...])
    dta = dt * a_neg
    ri = lax.broadcasted_iota(jnp.int32, (q, q), 0)
    cj = lax.broadcasted_iota(jnp.int32, (q, q), 1)
    tri = ri >= cj
    trib = jnp.where(tri, 1.0, 0.0).astype(BF16)
    acum = sum(jnp.dot(trib, part, preferred_element_type=F32) for part in _split3(dta))
    acum_t = acum.T
    a_last = acum[q - 1:q, :]
    dec_end = jnp.exp(a_last - acum)
    ea = jnp.exp(acum)
    cdec = jnp.exp(a_last)
    wgt = dt * dec_end

    xs = pad(x_ref[...])
    bm = pad(b_ref[...])
    cm = pad(c_ref[...])

    def expand(a, g, nrow):
        return jnp.concatenate(
            [jnp.broadcast_to(a[:, h:h + 1], (nrow, SSM_HEAD_DIM)) for h in range(g * hpg, (g + 1) * hpg)],
            axis=1)

    for g in range(SSM_GROUPS):
        bg = bm[:, g * SSM_STATE:(g + 1) * SSM_STATE]
        cgb = cm[:, g * SSM_STATE:(g + 1) * SSM_STATE].astype(BF16)
        cb = lax.dot_general(cgb, bg.astype(BF16), _NT, preferred_element_type=F32)
        xg = xs[:, g * gw:(g + 1) * gw]
        xdt = (xg * expand(dt, g, q)).astype(BF16)
        xw = (xg * expand(wgt, g, q)).astype(BF16)
        st_prev = st_ref[g]
        y_off = jnp.dot(cgb, st_prev.astype(BF16), preferred_element_type=F32) * expand(ea, g, q)
        st_new = jnp.dot(bg.T.astype(BF16), xw, preferred_element_type=F32)
        st_ref[g] = expand(cdec, g, 1) * st_prev + st_new
        yd = []
        for r in range(hpg):
            h = g * hpg + r
            seg = acum[:, h:h + 1] - acum_t[h:h + 1, :]
            lmat = jnp.where(tri, jnp.exp(seg), 0.0)
            m = (cb * lmat).astype(BF16)
            yd.append(jnp.dot(m, xdt[:, r * SSM_HEAD_DIM:(r + 1) * SSM_HEAD_DIM],
                              preferred_element_type=F32))
        yg = jnp.concatenate(yd, axis=1) + y_off + xg * dskip_ref[:, g * gw:(g + 1) * gw]
        y_ref[:, g * gw:(g + 1) * gw] = yg[:rows]

    @pl.when(ci == pl.num_programs(1) - 1)
    def _():
        for g in range(SSM_GROUPS):
            sout_ref[g * gw:(g + 1) * gw, :] = st_ref[g].T


def ssd_scan(xbc, proj, dt_col, dt_bias, a_log, d_skip, s0, bsz, seq):
    inner = d_skip.shape[0] * SSM_HEAD_DIM
    gn = SSM_GROUPS * SSM_STATE
    rows = min(SSD_BLOCK, seq)
    nc = seq // rows
    heads = d_skip.shape[0]
    pad128 = lambda v: jnp.pad(v, (0, LANES - heads)).reshape(1, LANES)
    row_map = lambda bi, ci: bi * nc + ci
    in_specs = [pl.BlockSpec((rows, inner), lambda bi, ci: (row_map(bi, ci), 0)),
                pl.BlockSpec((rows, gn), lambda bi, ci: (row_map(bi, ci), inner // gn)),
                pl.BlockSpec((rows, gn), lambda bi, ci: (row_map(bi, ci), inner // gn + 1)),
                pl.BlockSpec((rows, LANES), lambda bi, ci: (row_map(bi, ci), dt_col // LANES)),
                pl.BlockSpec((1, LANES), lambda bi, ci: (0, 0)),
                pl.BlockSpec((1, LANES), lambda bi, ci: (0, 0)),
                pl.BlockSpec((1, inner), lambda bi, ci: (0, 0))]
    args = [xbc, xbc, xbc, proj, pad128(dt_bias), pad128(a_log),
            jnp.repeat(d_skip, SSM_HEAD_DIM).reshape(1, inner)]
    if s0 is not None:
        in_specs.append(pl.BlockSpec((None, inner, SSM_STATE), lambda bi, ci: (bi, 0, 0)))
        args.append(s0)
    return pl.pallas_call(
        functools.partial(_ssd_kernel, rows=rows, has_init=s0 is not None),
        grid=(bsz, nc),
        in_specs=in_specs,
        out_specs=[pl.BlockSpec((rows, inner), lambda bi, ci: (row_map(bi, ci), 0)),
                   pl.BlockSpec((None, inner, SSM_STATE), lambda bi, ci: (bi, 0, 0))],
        out_shape=[jax.ShapeDtypeStruct((bsz * seq, inner), F32),
                   jax.ShapeDtypeStruct((bsz, inner, SSM_STATE), F32)],
        scratch_shapes=[pltpu.VMEM((SSM_GROUPS, SSM_STATE, inner // SSM_GROUPS), F32)],
        compiler_params=_cparams(2, 40),
        name="ssd_scan",
    )(*args)


def _gated_out_kernel(y_ref, z_ref, g_ref, w_ref, r_ref, o_ref):
    z = z_ref[...]
    y = y_ref[...] * (z * jax.nn.sigmoid(z))
    yn = _rms(y, g_ref[...]).astype(BF16)
    o_ref[...] = r_ref[...] + jnp.dot(yn, w_ref[...], preferred_element_type=F32)


def gated_norm_out(y, proj, g, w, res):
    t, inner = y.shape
    n = w.shape[1]
    tm = _tile(t, (256, 128))
    return pl.pallas_call(
        _gated_out_kernel,
        grid=(t // tm,),
        in_specs=[pl.BlockSpec((tm, inner), lambda i: (i, 0)),
                  pl.BlockSpec((tm, inner), lambda i: (i, 0)),
                  pl.BlockSpec((1, inner), lambda i: (0, 0)),
                  pl.BlockSpec((inner, n), lambda i: (0, 0)),
                  pl.BlockSpec((tm, n), lambda i: (i, 0))],
        out_specs=pl.BlockSpec((tm, n), lambda i: (i, 0)),
        out_shape=jax.ShapeDtypeStruct((t, n), F32),
        compiler_params=_cparams(1, 40),
        name="gated_norm_out",
    )(y, proj, g.reshape(1, inner), w, res)


def _attn_kernel(*refs, n_kv, masked, tq):
    q_ref, sink_ref = refs[0], refs[1]
    k_refs = refs[2:2 + n_kv]
    v_refs = refs[2 + n_kv:2 + 2 * n_kv]
    o_ref = refs[-1]
    hd = ATTN_HEAD_DIM
    qpk = q_ref.shape[1] // hd // KV_HEADS
    scale = hd ** -0.5
    ci = pl.program_id(1)
    outs = []
    for kh in range(KV_HEADS):
        kmat = jnp.concatenate([r[:, kh * hd:(kh + 1) * hd] for r in k_refs], axis=0).astype(BF16)
        vmat = jnp.concatenate([r[:, kh * hd:(kh + 1) * hd] for r in v_refs], axis=0).astype(BF16)
        heads = range(kh * qpk, (kh + 1) * qpk)
        q4 = jnp.concatenate([q_ref[:, h * hd:(h + 1) * hd] for h in heads], axis=0).astype(BF16)
        s = lax.dot_general(q4, kmat, _NT, preferred_element_type=F32) * scale
        if masked:
            kchunk = lax.broadcasted_iota(jnp.int32, (1, s.shape[1]), 1) // ATTN_CHUNK
            s = jnp.where(ci - (n_kv - 1) + kchunk >= 0, s, -jnp.inf)
        sink = jnp.concatenate([jnp.broadcast_to(sink_ref[:, h:h + 1], (tq, 1)) for h in heads], axis=0)
        m = jnp.maximum(jnp.max(s, axis=-1, keepdims=True), sink)
        p = jnp.exp(s - m)
        den = jnp.sum(p, axis=-1, keepdims=True) + jnp.exp(sink - m)
        o4 = jnp.dot((p / den).astype(BF16), vmat, preferred_element_type=F32)
        outs += [o4[g * tq:(g + 1) * tq] for g in range(qpk)]
    o_ref[...] = jnp.concatenate(outs, axis=1)


def swa_attention(q, sinks, k_src, v_src, k_col, v_col, bsz, seq, kv_rows, n_kv, masked):
    t, hqd = q.shape
    kvd = KV_HEADS * ATTN_HEAD_DIM
    tq = min(ATTN_CHUNK, seq)
    nq = seq // tq
    n_heads = sinks.shape[0]

    def kv_spec(col, off):
        if masked:
            return pl.BlockSpec((kv_rows, kvd),
                                lambda bi, ci: (bi * nq + jnp.maximum(ci - off, 0), col))
        return pl.BlockSpec((kv_rows, kvd), lambda bi, ci: (bi, col))

    offs = list(range(n_kv - 1, -1, -1))
    in_specs = ([pl.BlockSpec((tq, hqd), lambda bi, ci: (bi * nq + ci, 0)),
                 pl.BlockSpec((1, LANES), lambda bi, ci: (0, 0))]
                + [kv_spec(k_col, o) for o in offs] + [kv_spec(v_col, o) for o in offs])
    return pl.pallas_call(
        functools.partial(_attn_kernel, n_kv=n_kv, masked=masked, tq=tq),
        grid=(bsz, nq),
        in_specs=in_specs,
        out_specs=pl.BlockSpec((tq, hqd), lambda bi, ci: (bi * nq + ci, 0)),
        out_shape=jax.ShapeDtypeStruct((t, hqd), F32),
        compiler_params=_cparams(2, 32),
        name="swa_attention",
    )(q, jnp.pad(sinks, (0, LANES - n_heads)).reshape(1, LANES),
      *([k_src] * n_kv), *([v_src] * n_kv))


_PAIRS = [(a, b) for a in range(PEER_TOPK) for b in range(PEER_TOPK) if (a + 1) * (b + 1) <= PEER_TOPK]
_CAND_ROWS = -(-len(_PAIRS) // SUBLANES) * SUBLANES


def _extract_topk(s, n_rounds):
    n = s.shape[0]
    iota = lax.broadcasted_iota(jnp.int32, s.shape, 0)
    rank = jnp.full(s.shape, n_rounds, jnp.int32)
    vals = []
    for r in range(n_rounds):
        m = jnp.max(s, axis=0, keepdims=True)
        idx = jnp.min(jnp.where(s == m, iota, n), axis=0, keepdims=True)
        sel = iota == idx
        rank = jnp.where(sel, r, rank)
        s = jnp.where(sel, -jnp.inf, s)
        vals.append(m)
    return rank, vals


def _peer_topk_kernel(q_ref, k1_ref, k2_ref, e1_ref, c1_ref, e2_ref, r2_ref, cand_ref, sel_ref):
    half = k1_ref.shape[1]
    tm = q_ref.shape[1]

    def per_head(h, carry):
        qh = q_ref[h].astype(BF16)
        s1 = lax.dot_general(k1_ref[...], qh[:, :half], _NT, preferred_element_type=F32)
        s2 = lax.dot_general(k2_ref[...], qh[:, half:], _NT, preferred_element_type=F32)
        rank1, v1 = _extract_topk(s1, PEER_TOPK)
        rank2, v2 = _extract_topk(s2, PEER_TOPK)
        cand_ref[...] = jnp.full((_CAND_ROWS, tm), -jnp.inf, F32)
        for i, (a, b) in enumerate(_PAIRS):
            cand_ref[i:i + 1, :] = v1[a] + v2[b]
        cand = cand_ref[...]
        crank, _ = _extract_topk(cand, PEER_TOPK)
        chosen = crank < PEER_TOPK
        top = v1[0] + v2[0]
        z = jnp.sum(jnp.where(chosen, jnp.exp(cand - top), 0.0), axis=0, keepdims=True)
        sel_ref[...] = jnp.where(chosen, 1.0, 0.0)
        c1 = jnp.zeros(s1.shape, F32)
        row = 0
        for a in range(PEER_TOPK):
            nb = sum(1 for (pa, _) in _PAIRS if pa == a)
            cnt = jnp.sum(sel_ref[row:row + nb, :], axis=0, keepdims=True)
            c1 = jnp.where(rank1 == a, cnt, c1)
            row += nb
        e1_ref[h] = jnp.exp(s1 - v1[0])
        c1_ref[h] = c1
        e2_ref[h] = jnp.exp(s2 - v2[0]) / z
        r2_ref[h] = rank2.astype(F32)
        return carry

    lax.fori_loop(0, q_ref.shape[0], per_head, 0)


def peer_topk(qh, k1, k2):
    nh, t, qd = qh.shape
    nk = k1.shape[0]
    tm = LANES
    out = jax.ShapeDtypeStruct((nh, nk, t), F32)
    ospec = pl.BlockSpec((nh, nk, tm), lambda i: (0, 0, i))
    return pl.pallas_call(
        _peer_topk_kernel,
        grid=(t // tm,),
        in_specs=[pl.BlockSpec((nh, tm, qd), lambda i: (0, i, 0)),
                  pl.BlockSpec((nk, qd // 2), lambda i: (0, 0)),
                  pl.BlockSpec((nk, qd // 2), lambda i: (0, 0))],
        out_specs=[ospec] * 4,
        out_shape=[out] * 4,
        scratch_shapes=[pltpu.VMEM((_CAND_ROWS, tm), F32), pltpu.VMEM((_CAND_ROWS, tm), F32)],
        compiler_params=_cparams(1, 32),
        name="peer_topk",
    )(qh, k1, k2)


def _peer_dense_kernel(h_ref, g_ref, u_ref, vt_ref, e1_ref, c1_ref, e2_ref, r2_ref, o_ref,
                       xn_ref, acc_ref):
    ki = pl.program_id(1)
    nkeys = e2_ref.shape[1]
    te = u_ref.shape[0]

    @pl.when(ki == 0)
    def _():
        xn_ref[...] = _rms(h_ref[...], g_ref[...]).astype(BF16)
        acc_ref[...] = jnp.zeros(acc_ref.shape, F32)

    act = jax.nn.gelu(lax.dot_general(u_ref[...], xn_ref[...], _NT, preferred_element_type=F32),
                      approximate=True)
    pieces = []
    for r in range(te // nkeys):
        key1 = ki * (te // nkeys) + r
        w = jnp.zeros((nkeys, act.shape[1]), F32)
        for h in range(e2_ref.shape[0]):
            chosen = r2_ref[h] < c1_ref[h, pl.ds(key1, 1), :]
            w = w + jnp.where(chosen, e2_ref[h], 0.0) * e1_ref[h, pl.ds(key1, 1), :]
        pieces.append((w * act[r * nkeys:(r + 1) * nkeys]).astype(BF16))
    wact = jnp.concatenate(pieces, axis=0)
    acc_ref[...] += jnp.dot(vt_ref[...], wact, preferred_element_type=F32)

    @pl.when(ki == pl.num_programs(1) - 1)
    def _():
        o_ref[...] = h_ref[...] + acc_ref[...].T


def peer_dense(h, g, u, vt, e1, c1, e2, r2):
    t, d = h.shape
    ne = u.shape[0]
    nh, nk, _ = e1.shape
    tm = _tile(t, (512, 256))
    te = 1024
    sspec = pl.BlockSpec((nh, nk, tm), lambda i, k: (0, 0, i))
    return pl.pallas_call(
        _peer_dense_kernel,
        grid=(t // tm, ne // te),
        in_specs=[pl.BlockSpec((tm, d), lambda i, k: (i, 0)),
                  pl.BlockSpec((1, d), lambda i, k: (0, 0)),
                  pl.BlockSpec((te, d), lambda i, k: (k, 0)),
                  pl.BlockSpec((d, te), lambda i, k: (0, k)),
                  sspec, sspec, sspec, sspec],
        out_specs=pl.BlockSpec((tm, d), lambda i, k: (i, 0)),
        out_shape=jax.ShapeDtypeStruct((t, d), F32),
        scratch_shapes=[pltpu.VMEM((tm, d), BF16), pltpu.VMEM((d, tm), F32)],
        compiler_params=_cparams(2, 56),
        name="peer_dense",
    )(h, g.reshape(1, d), u, vt, e1, c1, e2, r2)


def kernel(x_prompt, x_sample, state_ssm, state_conv, cache_k_win, cache_v_win, norm_mix, norm_ffn, norm_kv, norm_final, m_w_in, m_conv_w, m_conv_b, m_dt_bias, m_a_log, m_d_skip, m_norm, m_w_out, a_w_kv, a_b_kv, a_w_q, a_b_q, a_sinks, a_w_o, a_b_o, p_w_q, p_sub_k1, p_sub_k2, p_u, p_v):
    depth = norm_mix.shape[0]
    n_a = m_w_in.shape[0]
    d = x_prompt.shape[-1]
    inner = m_w_out.shape[1]
    conv_dim = m_conv_w.shape[2]
    heads = m_d_skip.shape[1]
    kvd = a_w_kv.shape[1] // 2
    qdim = p_w_q.shape[2] // PEER_HEADS
    dt_col = inner + conv_dim
    in_pad = dt_col + LANES

    w_in = jnp.pad(m_w_in, ((0, 0), (0, 0), (0, in_pad - m_w_in.shape[2]))).astype(BF16)
    w_out = m_w_out.astype(BF16)
    w_kv = a_w_kv.astype(BF16)
    w_q = a_w_q.astype(BF16)
    w_o = a_w_o.astype(BF16)
    pw_q = p_w_q.astype(BF16)
    k1 = p_sub_k1.astype(BF16)
    k2 = p_sub_k2.astype(BF16)
    u_tab = p_u.astype(BF16)
    vt_tab = jnp.swapaxes(p_v.astype(BF16), 1, 2)
    zeros_in = jnp.zeros((in_pad,), F32)
    zeros_pq = jnp.zeros((p_w_q.shape[2],), F32)
    in_tn = _tile(in_pad, (896, 128))

    def run(x, ssm_init, conv_init, k_prev, v_prev):
        bsz, seq, _ = x.shape
        h = x.reshape(bsz * seq, d)
        ssm_out, conv_out = [], []
        k_src = v_src = None
        for l in range(depth):
            if l < n_a:
                proj = norm_matmul(h, norm_mix[l], w_in[l], zeros_in, in_tn)
                if conv_init is None:
                    prev8 = jnp.zeros((bsz, SUBLANES, conv_dim), F32)
                else:
                    prev8 = jnp.pad(conv_init[l], ((0, 0), (SUBLANES - (CONV_WIDTH - 1), 0), (0, 0)))
                xbc = causal_conv_silu(proj, prev8, m_conv_w[l], m_conv_b[l], bsz, seq, inner)
                s0 = None if ssm_init is None else ssm_init[l].reshape(bsz, inner, SSM_STATE)
                y, s_new = ssd_scan(xbc, proj, dt_col, m_dt_bias[l], m_a_log[l], m_d_skip[l], s0, bsz, seq)
                ssm_out.append(s_new.reshape(bsz, heads, SSM_HEAD_DIM, SSM_STATE))
                raw = proj[:, inner:dt_col].reshape(bsz, seq, conv_dim)
                conv_out.append(raw[:, seq - (CONV_WIDTH - 1):])
                h = gated_norm_out(y, proj, m_norm[l], w_out[l], h)
            else:
                j = l - n_a
                if j == 0:
                    kv = norm_matmul(h, norm_kv, w_kv, a_b_kv, 2 * kvd)
                    k_new = kv[:, :kvd].reshape(bsz, seq, kvd)
                    v_new = kv[:, kvd:].reshape(bsz, seq, kvd)
                    if k_prev is not None:
                        k_all = jnp.concatenate([k_prev.reshape(bsz, WINDOW, kvd), k_new], axis=1)
                        v_all = jnp.concatenate([v_prev.reshape(bsz, WINDOW, kvd), v_new], axis=1)
                        k_src = k_all.reshape(bsz * (WINDOW + seq), kvd)
                        v_src = v_all.reshape(bsz * (WINDOW + seq), kvd)
                        k_win, v_win = k_all[:, -WINDOW:], v_all[:, -WINDOW:]
                    else:
                        k_win, v_win = k_new[:, -WINDOW:], v_new[:, -WINDOW:]
                qp = norm_matmul(h, norm_mix[l], w_q[j], a_b_q[j], 512)
                if k_prev is None:
                    o = swa_attention(qp, a_sinks[j], kv, kv, 0, 1, bsz, seq, ATTN_CHUNK,
                                      WINDOW // ATTN_CHUNK + 1, True)
                else:
                    o = swa_attention(qp, a_sinks[j], k_src, v_src, 0, 0, bsz, seq, WINDOW + seq, 1, False)
                h = matmul_bias_res(o, w_o[j], a_b_o[j], h)
            qh = norm_matmul(h, norm_ffn[l], pw_q[l], zeros_pq, qdim, head_major=True)
            e1, c1, e2, r2 = peer_topk(qh, k1[l], k2[l])
            h = peer_dense(h, norm_ffn[l], u_tab[l], vt_tab[l], e1, c1, e2, r2)
        y = rmsnorm_rows(h, norm_final).reshape(bsz, seq, d)
        kshape = (bsz, WINDOW, KV_HEADS, ATTN_HEAD_DIM)
        return y, jnp.stack(ssm_out), jnp.stack(conv_out), k_win.reshape(kshape), v_win.reshape(kshape)

    y_p, p_ssm, p_conv, p_k, p_v_win = run(x_prompt, None, None, None, None)
    y_s, s_ssm, s_conv, s_k, s_v = run(x_sample, state_ssm, state_conv, cache_k_win, cache_v_win)
    return (y_p, y_s, p_ssm, p_conv, p_k, p_v_win, s_ssm, s_conv, s_k, s_v)
```

```python
import functools

import jax
import jax.numpy as jnp
from jax import lax
from jax.experimental import pallas as pl
from jax.experimental.pallas import tpu as pltpu

F32 = jnp.float32
BF16 = jnp.bfloat16
EPS = 1e-6

WINDOW = 128
SSM_HEAD_DIM = 64
SSM_STATE = 128
SSM_GROUPS = 8
CONV_WIDTH = 4
SSD_BLOCK = 128
ATTN_CHUNK = 64
ATTN_HEAD_DIM = 64
KV_HEADS = 4
PEER_HEADS = 8
PEER_NKEYS = 128
PEER_TOPK = 16
LANES = 128
SUBLANES = 8
MIB = 1024 * 1024

_NT = (((1,), (1,)), ((), ()))


def _cparams(n_axes, vmem_mib):
    return pltpu.CompilerParams(
        dimension_semantics=("arbitrary",) * n_axes, vmem_limit_bytes=vmem_mib * MIB)


def _tile(n, prefs):
    for p in prefs:
        if n % p == 0:
            return p
    return n


def _rms(x, g):
    ms = jnp.mean(x * x, axis=-1, keepdims=True)
    return x * lax.rsqrt(ms + EPS) * g


def _norm_matmul_kernel(x_ref, g_ref, w_ref, b_ref, o_ref, xn_ref):
    @pl.when(pl.program_id(1) == 0)
    def _():
        xn_ref[...] = _rms(x_ref[...], g_ref[...]).astype(BF16)

    o_ref[...] = jnp.dot(xn_ref[...], w_ref[...], preferred_element_type=F32) + b_ref[...]


def norm_matmul(x, g, w, b, tn, head_major=False):
    t, d = x.shape
    n = w.shape[1]
    tm = _tile(t, (512, 256))
    if head_major:
        out_shape = jax.ShapeDtypeStruct((n // tn, t, tn), F32)
        out_spec = pl.BlockSpec((None, tm, tn), lambda i, j: (j, i, 0))
    else:
        out_shape = jax.ShapeDtypeStruct((t, n), F32)
        out_spec = pl.BlockSpec((tm, tn), lambda i, j: (i, j))
    return pl.pallas_call(
        _norm_matmul_kernel,
        grid=(t // tm, n // tn),
        in_specs=[pl.BlockSpec((tm, d), lambda i, j: (i, 0)),
                  pl.BlockSpec((1, d), lambda i, j: (0, 0)),
                  pl.BlockSpec((d, tn), lambda i, j: (0, j)),
                  pl.BlockSpec((1, tn), lambda i, j: (0, j))],
        out_specs=out_spec,
        out_shape=out_shape,
        scratch_shapes=[pltpu.VMEM((tm, d), BF16)],
        compiler_params=_cparams(2, 32),
        name="norm_matmul",
    )(x, g.reshape(1, d), w, b.reshape(1, n))


def _matmul_res_kernel(x_ref, w_ref, b_ref, r_ref, o_ref):
    mix = jnp.dot(x_ref[...].astype(BF16), w_ref[...], preferred_element_type=F32) + b_ref[...]
    o_ref[...] = r_ref[...] + mix


def matmul_bias_res(x, w, b, res):
    t, k = x.shape
    n = w.shape[1]
    tm = _tile(t, (512, 256))
    return pl.pallas_call(
        _matmul_res_kernel,
        grid=(t // tm,),
        in_specs=[pl.BlockSpec((tm, k), lambda i: (i, 0)),
                  pl.BlockSpec((k, n), lambda i: (0, 0)),
                  pl.BlockSpec((1, n), lambda i: (0, 0)),
                  pl.BlockSpec((tm, n), lambda i: (i, 0))],
        out_specs=pl.BlockSpec((tm, n), lambda i: (i, 0)),
        out_shape=jax.ShapeDtypeStruct((t, n), F32),
        compiler_params=_cparams(1, 32),
        name="matmul_bias_res",
    )(x, w, b.reshape(1, n), res)


def _rmsnorm_kernel(x_ref, g_ref, o_ref):
    o_ref[...] = _rms(x_ref[...], g_ref[...])


def rmsnorm_rows(x, g):
    t, d = x.shape
    tm = _tile(t, (512, 256))
    return pl.pallas_call(
        _rmsnorm_kernel,
        grid=(t // tm,),
        in_specs=[pl.BlockSpec((tm, d), lambda i: (i, 0)), pl.BlockSpec((1, d), lambda i: (0, 0))],
        out_specs=pl.BlockSpec((tm, d), lambda i: (i, 0)),
        out_shape=jax.ShapeDtypeStruct((t, d), F32),
        compiler_params=_cparams(1, 32),
        name="rmsnorm",
    )(x, g.reshape(1, d))


def _conv_kernel(x_ref, halo_ref, prev_ref, w_ref, b_ref, o_ref, ext_ref, *, tl):
    first = pl.program_id(1) == 0
    ext_ref[0:SUBLANES, :] = jnp.where(first, prev_ref[...], halo_ref[...])
    ext_ref[SUBLANES:SUBLANES + tl, :] = x_ref[...]
    acc = b_ref[...]
    for k in range(CONV_WIDTH):
        off = SUBLANES - (CONV_WIDTH - 1) + k
        acc = acc + ext_ref[off:off + tl, :] * w_ref[k:k + 1, :]
    o_ref[...] = acc * jax.nn.sigmoid(acc)


def causal_conv_silu(proj, prev8, w, b, bsz, seq, col0):
    c = w.shape[1]
    tl = _tile(seq, (256, 128, 64, 32, 16))
    tc = 1024
    nl = seq // tl
    cb0 = col0 // tc
    hpb = tl // SUBLANES
    return pl.pallas_call(
        functools.partial(_conv_kernel, tl=tl),
        grid=(bsz, nl, c // tc),
        in_specs=[pl.BlockSpec((tl, tc), lambda bi, li, ci: (bi * nl + li, cb0 + ci)),
                  pl.BlockSpec((SUBLANES, tc),
                               lambda bi, li, ci: (jnp.maximum((bi * nl + li) * hpb - 1, 0), cb0 + ci)),
                  pl.BlockSpec((None, SUBLANES, tc), lambda bi, li, ci: (bi, 0, ci)),
                  pl.BlockSpec((CONV_WIDTH, tc), lambda bi, li, ci: (0, ci)),
                  pl.BlockSpec((1, tc), lambda bi, li, ci: (0, ci))],
        out_specs=pl.BlockSpec((tl, tc), lambda bi, li, ci: (bi * nl + li, ci)),
        out_shape=jax.ShapeDtypeStruct((bsz * seq, c), F32),
        scratch_shapes=[pltpu.VMEM((SUBLANES + tl, tc), F32)],
        compiler_params=_cparams(3, 32),
        name="conv_silu",
    )(proj, proj, prev8, w, b.reshape(1, c))


def _split3(x):
    hi = x.astype(BF16)
    r1 = x - hi.astype(F32)
    mid = r1.astype(BF16)
    lo = (r1 - mid.astype(F32)).astype(BF16)
    return hi, mid, lo


def _ssd_kernel(*refs, rows, has_init):
    if has_init:
        (x_ref, b_ref, c_ref, dt_ref, bias_ref, alog_ref, dskip_ref, s0_ref,
         y_ref, sout_ref, st_ref) = refs
    else:
        (x_ref, b_ref, c_ref, dt_ref, bias_ref, alog_ref, dskip_ref,
         y_ref, sout_ref, st_ref) = refs
    q = SSD_BLOCK
    gw = x_ref.shape[1] // SSM_GROUPS
    hpg = gw // SSM_HEAD_DIM
    ci = pl.program_id(1)

    @pl.when(ci == 0)
    def _():
        for g in range(SSM_GROUPS):
            if has_init:
                st_ref[g] = s0_ref[g * gw:(g + 1) * gw, :].T
            else:
                st_ref[g] = jnp.zeros((SSM_STATE, gw), F32)

    def pad(a):
        if rows == q:
            return a
        return jnp.concatenate([a, jnp.zeros((q - rows, a.shape[1]), a.dtype)], axis=0)

    raw = dt_ref[...] + bias_ref[...]
    dt = pad(jnp.maximum(raw, 0.0) + jnp.log1p(jnp.exp(-jnp.abs(raw))))
    a_neg = -jnp.exp(alog_ref[...])
    dta = dt * a_neg
    ri = lax.broadcasted_iota(jnp.int32, (q, q), 0)
    cj = lax.broadcasted_iota(jnp.int32, (q, q), 1)
    tri = ri >= cj
    trib = jnp.where(tri, 1.0, 0.0).astype(BF16)
    acum = sum(jnp.dot(trib, part, preferred_element_type=F32) for part in _split3(dta))
    acum_t = acum.T
    a_last = acum[q - 1:q, :]
    dec_end = jnp.exp(a_last - acum)
    ea = jnp.exp(acum)
    cdec = jnp.exp(a_last)
    wgt = dt * dec_end

    xs = pad(x_ref[...])
    bm = pad(b_ref[...])
    cm = pad(c_ref[...])

    def expand(a, g, nrow):
        return jnp.concatenate(
            [jnp.broadcast_to(a[:, h:h + 1], (nrow, SSM_HEAD_DIM)) for h in range(g * hpg, (g + 1) * hpg)],
            axis=1)

    for g in range(SSM_GROUPS):
        bg = bm[:, g * SSM_STATE:(g + 1) * SSM_STATE]
        cgb = cm[:, g * SSM_STATE:(g + 1) * SSM_STATE].astype(BF16)
        cb = lax.dot_general(cgb, bg.astype(BF16), _NT, preferred_element_type=F32)
        xg = xs[:, g * gw:(g + 1) * gw]
        xdt = (xg * expand(dt, g, q)).astype(BF16)
        xw = (xg * expand(wgt, g, q)).astype(BF16)
        st_prev = st_ref[g]
        y_off = jnp.dot(cgb, st_prev.astype(BF16), preferred_element_type=F32) * expand(ea, g, q)
        st_new = jnp.dot(bg.T.astype(BF16), xw, preferred_element_type=F32)
        st_ref[g] = expand(cdec, g, 1) * st_prev + st_new
        yd = []
        for r in range(hpg):
            h = g * hpg + r
            seg = acum[:, h:h + 1] - acum_t[h:h + 1, :]
            lmat = jnp.where(tri, jnp.exp(seg), 0.0)
            m = (cb * lmat).astype(BF16)
            yd.append(jnp.dot(m, xdt[:, r * SSM_HEAD_DIM:(r + 1) * SSM_HEAD_DIM],
                              preferred_element_type=F32))
        yg = jnp.concatenate(yd, axis=1) + y_off + xg * dskip_ref[:, g * gw:(g + 1) * gw]
        y_ref[:, g * gw:(g + 1) * gw] = yg[:rows]

    @pl.when(ci == pl.num_programs(1) - 1)
    def _():
        for g in range(SSM_GROUPS):
            sout_ref[g * gw:(g + 1) * gw, :] = st_ref[g].T


def ssd_scan(xbc, proj, dt_col, dt_bias, a_log, d_skip, s0, bsz, seq):
    inner = d_skip.shape[0] * SSM_HEAD_DIM
    gn = SSM_GROUPS * SSM_STATE
    rows = min(SSD_BLOCK, seq)
    nc = seq // rows
    heads = d_skip.shape[0]
    pad128 = lambda v: jnp.pad(v, (0, LANES - heads)).reshape(1, LANES)
    row_map = lambda bi, ci: bi * nc + ci
    in_specs = [pl.BlockSpec((rows, inner), lambda bi, ci: (row_map(bi, ci), 0)),
                pl.BlockSpec((rows, gn), lambda bi, ci: (row_map(bi, ci), inner // gn)),
                pl.BlockSpec((rows, gn), lambda bi, ci: (row_map(bi, ci), inner // gn + 1)),
                pl.BlockSpec((rows, LANES), lambda bi, ci: (row_map(bi, ci), dt_col // LANES)),
                pl.BlockSpec((1, LANES), lambda bi, ci: (0, 0)),
                pl.BlockSpec((1, LANES), lambda bi, ci: (0, 0)),
                pl.BlockSpec((1, inner), lambda bi, ci: (0, 0))]
    args = [xbc, xbc, xbc, proj, pad128(dt_bias), pad128(a_log),
            jnp.repeat(d_skip, SSM_HEAD_DIM).reshape(1, inner)]
    if s0 is not None:
        in_specs.append(pl.BlockSpec((None, inner, SSM_STATE), lambda bi, ci: (bi, 0, 0)))
        args.append(s0)
    return pl.pallas_call(
        functools.partial(_ssd_kernel, rows=rows, has_init=s0 is not None),
        grid=(bsz, nc),
        in_specs=in_specs,
        out_specs=[pl.BlockSpec((rows, inner), lambda bi, ci: (row_map(bi, ci), 0)),
                   pl.BlockSpec((None, inner, SSM_STATE), lambda bi, ci: (bi, 0, 0))],
        out_shape=[jax.ShapeDtypeStruct((bsz * seq, inner), F32),
                   jax.ShapeDtypeStruct((bsz, inner, SSM_STATE), F32)],
        scratch_shapes=[pltpu.VMEM((SSM_GROUPS, SSM_STATE, inner // SSM_GROUPS), F32)],
        compiler_params=_cparams(2, 40),
        name="ssd_scan",
    )(*args)


def _gated_out_kernel(y_ref, z_ref, g_ref, w_ref, r_ref, o_ref):
    z = z_ref[...]
    y = y_ref[...] * (z * jax.nn.sigmoid(z))
    yn = _rms(y, g_ref[...]).astype(BF16)
    o_ref[...] = r_ref[...] + jnp.dot(yn, w_ref[...], preferred_element_type=F32)


def gated_norm_out(y, proj, g, w, res):
    t, inner = y.shape
    n = w.shape[1]
    tm = _tile(t, (256, 128))
    return pl.pallas_call(
        _gated_out_kernel,
        grid=(t // tm,),
        in_specs=[pl.BlockSpec((tm, inner), lambda i: (i, 0)),
                  pl.BlockSpec((tm, inner), lambda i: (i, 0)),
                  pl.BlockSpec((1, inner), lambda i: (0, 0)),
                  pl.BlockSpec((inner, n), lambda i: (0, 0)),
                  pl.BlockSpec((tm, n), lambda i: (i, 0))],
        out_specs=pl.BlockSpec((tm, n), lambda i: (i, 0)),
        out_shape=jax.ShapeDtypeStruct((t, n), F32),
        compiler_params=_cparams(1, 40),
        name="gated_norm_out",
    )(y, proj, g.reshape(1, inner), w, res)


def _attn_kernel(*refs, n_kv, masked, tq):
    q_ref, sink_ref = refs[0], refs[1]
    k_refs = refs[2:2 + n_kv]
    v_refs = refs[2 + n_kv:2 + 2 * n_kv]
    o_ref = refs[-1]
    hd = ATTN_HEAD_DIM
    qpk = q_ref.shape[1] // hd // KV_HEADS
    scale = hd ** -0.5
    ci = pl.program_id(1)
    outs = []
    for kh in range(KV_HEADS):
        kmat = jnp.concatenate([r[:, kh * hd:(kh + 1) * hd] for r in k_refs], axis=0).astype(BF16)
        vmat = jnp.concatenate([r[:, kh * hd:(kh + 1) * hd] for r in v_refs], axis=0).astype(BF16)
        heads = range(kh * qpk, (kh + 1) * qpk)
        q4 = jnp.concatenate([q_ref[:, h * hd:(h + 1) * hd] for h in heads], axis=0).astype(BF16)
        s = lax.dot_general(q4, kmat, _NT, preferred_element_type=F32) * scale
        if masked:
            kchunk = lax.broadcasted_iota(jnp.int32, (1, s.shape[1]), 1) // ATTN_CHUNK
            s = jnp.where(ci - (n_kv - 1) + kchunk >= 0, s, -jnp.inf)
        sink = jnp.concatenate([jnp.broadcast_to(sink_ref[:, h:h + 1], (tq, 1)) for h in heads], axis=0)
        m = jnp.maximum(jnp.max(s, axis=-1, keepdims=True), sink)
        p = jnp.exp(s - m)
        den = jnp.sum(p, axis=-1, keepdims=True) + jnp.exp(sink - m)
        o4 = jnp.dot((p / den).astype(BF16), vmat, preferred_element_type=F32)
        outs += [o4[g * tq:(g + 1) * tq] for g in range(qpk)]
    o_ref[...] = jnp.concatenate(outs, axis=1)


def swa_attention(q, sinks, k_src, v_src, k_col, v_col, bsz, seq, kv_rows, n_kv, masked):
    t, hqd = q.shape
    kvd = KV_HEADS * ATTN_HEAD_DIM
    tq = min(ATTN_CHUNK, seq)
    nq = seq // tq
    n_heads = sinks.shape[0]

    def kv_spec(col, off):
        if masked:
            return pl.BlockSpec((kv_rows, kvd),
                                lambda bi, ci: (bi * nq + jnp.maximum(ci - off, 0), col))
        return pl.BlockSpec((kv_rows, kvd), lambda bi, ci: (bi, col))

    offs = list(range(n_kv - 1, -1, -1))
    in_specs = ([pl.BlockSpec((tq, hqd), lambda bi, ci: (bi * nq + ci, 0)),
                 pl.BlockSpec((1, LANES), lambda bi, ci: (0, 0))]
                + [kv_spec(k_col, o) for o in offs] + [kv_spec(v_col, o) for o in offs])
    return pl.pallas_call(
        functools.partial(_attn_kernel, n_kv=n_kv, masked=masked, tq=tq),
        grid=(bsz, nq),
        in_specs=in_specs,
        out_specs=pl.BlockSpec((tq, hqd), lambda bi, ci: (bi * nq + ci, 0)),
        out_shape=jax.ShapeDtypeStruct((t, hqd), F32),
        compiler_params=_cparams(2, 32),
        name="swa_attention",
    )(q, jnp.pad(sinks, (0, LANES - n_heads)).reshape(1, LANES),
      *([k_src] * n_kv), *([v_src] * n_kv))


_PAIRS = [(a, b) for a in range(PEER_TOPK) for b in range(PEER_TOPK) if (a + 1) * (b + 1) <= PEER_TOPK]
_CAND_ROWS = -(-len(_PAIRS) // SUBLANES) * SUBLANES


def _extract_topk(s, n_rounds):
    n = s.shape[0]
    iota = lax.broadcasted_iota(jnp.int32, s.shape, 0)
    rank = jnp.full(s.shape, n_rounds, jnp.int32)
    vals = []
    for r in range(n_rounds):
        m = jnp.max(s, axis=0, keepdims=True)
        idx = jnp.min(jnp.where(s == m, iota, n), axis=0, keepdims=True)
        sel = iota == idx
        rank = jnp.where(sel, r, rank)
        s = jnp.where(sel, -jnp.inf, s)
        vals.append(m)
    return rank, vals


def _peer_topk_kernel(q_ref, k1_ref, k2_ref, e1_ref, c1_ref, e2_ref, r2_ref,
                      s_ref, v_ref, cand_ref, sel_ref):
    half = k1_ref.shape[1]
    tm = q_ref.shape[1]

    def per_head(h, carry):
        qh = q_ref[h].astype(BF16)
        s_ref[0] = lax.dot_general(k1_ref[...], qh[:, :half], _NT, preferred_element_type=F32)
        s_ref[1] = lax.dot_general(k2_ref[...], qh[:, half:], _NT, preferred_element_type=F32)
        for lc in range(tm // LANES):
            lanes = slice(lc * LANES, (lc + 1) * LANES)
            rank1, v1 = _extract_topk(s_ref[0, :, lanes], PEER_TOPK)
            rank2, v2 = _extract_topk(s_ref[1, :, lanes], PEER_TOPK)
            c1_ref[h, :, lanes] = rank1.astype(F32)
            r2_ref[h, :, lanes] = rank2.astype(F32).astype(BF16)
            for r in range(PEER_TOPK):
                v_ref[0, r:r + 1, lanes] = v1[r]
                v_ref[1, r:r + 1, lanes] = v2[r]
        cand_ref[...] = jnp.full((_CAND_ROWS, tm), -jnp.inf, F32)
        for i, (a, b) in enumerate(_PAIRS):
            cand_ref[i:i + 1, :] = v_ref[0, a:a + 1, :] + v_ref[1, b:b + 1, :]
        cand = cand_ref[...]
        crank, _ = _extract_topk(cand, PEER_TOPK)
        chosen = crank < PEER_TOPK
        z = jnp.sum(jnp.where(chosen, jnp.exp(cand - cand[0:1]), 0.0), axis=0, keepdims=True)
        sel_ref[...] = jnp.where(chosen, 1.0, 0.0)
        cnts = []
        row = 0
        for a in range(PEER_TOPK):
            nb = sum(1 for (pa, _) in _PAIRS if pa == a)
            cnts.append(jnp.sum(sel_ref[row:row + nb, :], axis=0, keepdims=True))
            row += nb
        for lc in range(tm // LANES):
            lanes = slice(lc * LANES, (lc + 1) * LANES)
            rank1 = c1_ref[h, :, lanes]
            c1 = jnp.zeros(rank1.shape, F32)
            for a in range(PEER_TOPK):
                c1 = jnp.where(rank1 == a, cnts[a][:, lanes], c1)
            c1_ref[h, :, lanes] = c1
            e1_ref[h, :, lanes] = jnp.exp(s_ref[0, :, lanes] - v_ref[0, 0:1, lanes])
            e2_ref[h, :, lanes] = (jnp.exp(s_ref[1, :, lanes] - v_ref[1, 0:1, lanes])
                                   / z[:, lanes]).astype(BF16)
        return carry

    lax.fori_loop(0, q_ref.shape[0], per_head, 0)


def peer_topk(qh, k1, k2):
    nh, t, qd = qh.shape
    nk = k1.shape[0]
    tm = _tile(t, (2 * LANES, LANES))
    out32 = jax.ShapeDtypeStruct((nh, nk, t), F32)
    out16 = jax.ShapeDtypeStruct((nh, nk, t), BF16)
    ospec = pl.BlockSpec((nh, nk, tm), lambda i: (0, 0, i))
    return pl.pallas_call(
        _peer_topk_kernel,
        grid=(t // tm,),
        in_specs=[pl.BlockSpec((nh, tm, qd), lambda i: (0, i, 0)),
                  pl.BlockSpec((nk, qd // 2), lambda i: (0, 0)),
                  pl.BlockSpec((nk, qd // 2), lambda i: (0, 0))],
        out_specs=[ospec] * 4,
        out_shape=[out32, out32, out16, out16],
        scratch_shapes=[pltpu.VMEM((2, nk, tm), F32), pltpu.VMEM((2, PEER_TOPK, tm), F32),
                        pltpu.VMEM((_CAND_ROWS, tm), F32), pltpu.VMEM((_CAND_ROWS, tm), F32)],
        compiler_params=_cparams(1, 32),
        name="peer_topk",
    )(qh, k1, k2)


def _peer_dense_kernel(h_ref, g_ref, u_ref, vt_ref, e1_ref, c1_ref, e2_ref, r2_ref, o_ref,
                       xn_ref, acc_ref, gate_ref):
    ki = pl.program_id(1)
    nh, nkeys, tm = e2_ref.shape
    te = u_ref.shape[0]

    @pl.when(ki == 0)
    def _():
        xn_ref[...] = _rms(h_ref[...], g_ref[...]).astype(BF16)
        acc_ref[...] = jnp.zeros(acc_ref.shape, F32)

    zero = jnp.zeros((nkeys, LANES), BF16)
    for r in range(te // nkeys):
        rows = slice(r * nkeys, (r + 1) * nkeys)
        for lc in range(tm // LANES):
            lanes = slice(lc * LANES, (lc + 1) * LANES)
            w = zero
            for h in range(nh):
                c1b = jnp.broadcast_to(c1_ref[h, r:r + 1, lanes].astype(BF16), (nkeys, LANES))
                e1b = jnp.broadcast_to(e1_ref[h, r:r + 1, lanes].astype(BF16), (nkeys, LANES))
                w = w + jnp.where(r2_ref[h, :, lanes] < c1b, e2_ref[h, :, lanes], zero) * e1b
            gate_ref[rows, lanes] = w
    act = jax.nn.gelu(lax.dot_general(u_ref[...], xn_ref[...], _NT, preferred_element_type=F32),
                      approximate=True)
    wact = gate_ref[...] * act.astype(BF16)
    acc_ref[...] += jnp.dot(vt_ref[...], wact, preferred_element_type=F32)

    @pl.when(ki == pl.num_programs(1) - 1)
    def _():
        o_ref[...] = h_ref[...] + acc_ref[...].T


def peer_dense(h, g, u, vt, e1, c1, e2, r2):
    t, d = h.shape
    ne = u.shape[0]
    nh, nk, _ = e1.shape
    tm = _tile(t, (512, 256))
    te = 1024
    key1_spec = pl.BlockSpec((nh, te // nk, tm), lambda i, k: (0, k, i))
    key2_spec = pl.BlockSpec((nh, nk, tm), lambda i, k: (0, 0, i))
    return pl.pallas_call(
        _peer_dense_kernel,
        grid=(t // tm, ne // te),
        in_specs=[pl.BlockSpec((tm, d), lambda i, k: (i, 0)),
                  pl.BlockSpec((1, d), lambda i, k: (0, 0)),
                  pl.BlockSpec((te, d), lambda i, k: (k, 0)),
                  pl.BlockSpec((d, te), lambda i, k: (0, k)),
                  key1_spec, key1_spec, key2_spec, key2_spec],
        out_specs=pl.BlockSpec((tm, d), lambda i, k: (i, 0)),
        out_shape=jax.ShapeDtypeStruct((t, d), F32),
        scratch_shapes=[pltpu.VMEM((tm, d), BF16), pltpu.VMEM((d, tm), F32),
                        pltpu.VMEM((te, tm), BF16)],
        compiler_params=_cparams(2, 48),
        name="peer_dense",
    )(h, g.reshape(1, d), u, vt, e1, c1, e2, r2)


def kernel(x_prompt, x_sample, state_ssm, state_conv, cache_k_win, cache_v_win, norm_mix, norm_ffn, norm_kv, norm_final, m_w_in, m_conv_w, m_conv_b, m_dt_bias, m_a_log, m_d_skip, m_norm, m_w_out, a_w_kv, a_b_kv, a_w_q, a_b_q, a_sinks, a_w_o, a_b_o, p_w_q, p_sub_k1, p_sub_k2, p_u, p_v):
    depth = norm_mix.shape[0]
    n_a = m_w_in.shape[0]
    d = x_prompt.shape[-1]
    inner = m_w_out.shape[1]
    conv_dim = m_conv_w.shape[2]
    heads = m_d_skip.shape[1]
    kvd = a_w_kv.shape[1] // 2
    qdim = p_w_q.shape[2] // PEER_HEADS
    dt_col = inner + conv_dim
    in_pad = dt_col + LANES

    w_in = jnp.pad(m_w_in, ((0, 0), (0, 0), (0, in_pad - m_w_in.shape[2]))).astype(BF16)
    w_out = m_w_out.astype(BF16)
    w_kv = a_w_kv.astype(BF16)
    w_q = a_w_q.astype(BF16)
    w_o = a_w_o.astype(BF16)
    pw_q = p_w_q.astype(BF16)
    k1 = p_sub_k1.astype(BF16)
    k2 = p_sub_k2.astype(BF16)
    u_tab = p_u.astype(BF16)
    vt_tab = jnp.swapaxes(p_v.astype(BF16), 1, 2)
    zeros_in = jnp.zeros((in_pad,), F32)
    zeros_pq = jnp.zeros((p_w_q.shape[2],), F32)
    in_tn = _tile(in_pad, (896, 128))

    def run(x, ssm_init, conv_init, k_prev, v_prev):
        bsz, seq, _ = x.shape
        h = x.reshape(bsz * seq, d)
        ssm_out, conv_out = [], []
        k_src = v_src = None
        for l in range(depth):
            if l < n_a:
                proj = norm_matmul(h, norm_mix[l], w_in[l], zeros_in, in_tn)
                if conv_init is None:
                    prev8 = jnp.zeros((bsz, SUBLANES, conv_dim), F32)
                else:
                    prev8 = jnp.pad(conv_init[l], ((0, 0), (SUBLANES - (CONV_WIDTH - 1), 0), (0, 0)))
                xbc = causal_conv_silu(proj, prev8, m_conv_w[l], m_conv_b[l], bsz, seq, inner)
                s0 = None if ssm_init is None else ssm_init[l].reshape(bsz, inner, SSM_STATE)
                y, s_new = ssd_scan(xbc, proj, dt_col, m_dt_bias[l], m_a_log[l], m_d_skip[l], s0, bsz, seq)
                ssm_out.append(s_new.reshape(bsz, heads, SSM_HEAD_DIM, SSM_STATE))
                raw = proj[:, inner:dt_col].reshape(bsz, seq, conv_dim)
                conv_out.append(raw[:, seq - (CONV_WIDTH - 1):])
                h = gated_norm_out(y, proj, m_norm[l], w_out[l], h)
            else:
                j = l - n_a
                if j == 0:
                    kv = norm_matmul(h, norm_kv, w_kv, a_b_kv, 2 * kvd)
                    k_new = kv[:, :kvd].reshape(bsz, seq, kvd)
                    v_new = kv[:, kvd:].reshape(bsz, seq, kvd)
                    if k_prev is not None:
                        k_all = jnp.concatenate([k_prev.reshape(bsz, WINDOW, kvd), k_new], axis=1)
                        v_all = jnp.concatenate([v_prev.reshape(bsz, WINDOW, kvd), v_new], axis=1)
                        k_src = k_all.reshape(bsz * (WINDOW + seq), kvd)
                        v_src = v_all.reshape(bsz * (WINDOW + seq), kvd)
                        k_win, v_win = k_all[:, -WINDOW:], v_all[:, -WINDOW:]
                    else:
                        k_win, v_win = k_new[:, -WINDOW:], v_new[:, -WINDOW:]
                qp = norm_matmul(h, norm_mix[l], w_q[j], a_b_q[j], 512)
                if k_prev is None:
                    o = swa_attention(qp, a_sinks[j], kv, kv, 0, 1, bsz, seq, ATTN_CHUNK,
                                      WINDOW // ATTN_CHUNK + 1, True)
                else:
                    o = swa_attention(qp, a_sinks[j], k_src, v_src, 0, 0, bsz, seq, WINDOW + seq, 1, False)
                h = matmul_bias_res(o, w_o[j], a_b_o[j], h)
            qh = norm_matmul(h, norm_ffn[l], pw_q[l], zeros_pq, qdim, head_major=True)
            e1, c1, e2, r2 = peer_topk(qh, k1[l], k2[l])
            h = peer_dense(h, norm_ffn[l], u_tab[l], vt_tab[l], e1, c1, e2, r2)
        y = rmsnorm_rows(h, norm_final).reshape(bsz, seq, d)
        kshape = (bsz, WINDOW, KV_HEADS, ATTN_HEAD_DIM)
        return y, jnp.stack(ssm_out), jnp.stack(conv_out), k_win.reshape(kshape), v_win.reshape(kshape)

    y_p, p_ssm, p_conv, p_k, p_v_win = run(x_prompt, None, None, None, None)
    y_s, s_ssm, s_conv, s_k, s_v = run(x_sample, state_ssm, state_conv, cache_k_win, cache_v_win)
    return (y_p, y_s, p_ssm, p_conv, p_k, p_v_win, s_ssm, s_conv, s_k, s_v)
```

```python
import functools

import jax
import jax.numpy as jnp
from jax import lax
from jax.experimental import pallas as pl
from jax.experimental.pallas import tpu as pltpu

F32 = jnp.float32
BF16 = jnp.bfloat16
EPS = 1e-6

WINDOW = 128
SSM_HEAD_DIM = 64
SSM_STATE = 128
SSM_GROUPS = 8
CONV_WIDTH = 4
SSD_BLOCK = 128
ATTN_CHUNK = 64
ATTN_HEAD_DIM = 64
KV_HEADS = 4
PEER_HEADS = 8
PEER_NKEYS = 128
PEER_TOPK = 16
LANES = 128
SUBLANES = 8
MIB = 1024 * 1024

_NT = (((1,), (1,)), ((), ()))
_GELU_C1 = 0.7978845608028654
_GELU_C3 = _GELU_C1 * 0.044715


def _cparams(n_axes, vmem_mib, flags=None):
    return pltpu.CompilerParams(
        dimension_semantics=("arbitrary",) * n_axes, vmem_limit_bytes=vmem_mib * MIB, flags=flags)


def _tile(n, prefs):
    for p in prefs:
        if n % p == 0:
            return p
    return n


def _rms(x, g):
    ms = jnp.mean(x * x, axis=-1, keepdims=True)
    return x * lax.rsqrt(ms + EPS) * g


def _norm_matmul_kernel(x_ref, g_ref, w_ref, b_ref, o_ref, xn_ref, *, head_dim):
    @pl.when(pl.program_id(1) == 0)
    def _():
        xn_ref[...] = _rms(x_ref[...], g_ref[...]).astype(BF16)

    res = jnp.dot(xn_ref[...], w_ref[...], preferred_element_type=F32) + b_ref[...]
    if head_dim is None:
        o_ref[...] = res
    else:
        for k in range(o_ref.shape[0]):
            o_ref[k] = res[:, k * head_dim:(k + 1) * head_dim]


def norm_matmul(x, g, w, b, tn, head_dim=None):
    t, d = x.shape
    n = w.shape[1]
    tm = _tile(t, (1024, 512, 256))
    if head_dim is not None:
        out_shape = jax.ShapeDtypeStruct((n // head_dim, t, head_dim), F32)
        out_spec = pl.BlockSpec((tn // head_dim, tm, head_dim), lambda i, j: (j, i, 0))
    else:
        out_shape = jax.ShapeDtypeStruct((t, n), F32)
        out_spec = pl.BlockSpec((tm, tn), lambda i, j: (i, j))
    return pl.pallas_call(
        functools.partial(_norm_matmul_kernel, head_dim=head_dim),
        grid=(t // tm, n // tn),
        in_specs=[pl.BlockSpec((tm, d), lambda i, j: (i, 0)),
                  pl.BlockSpec((1, d), lambda i, j: (0, 0)),
                  pl.BlockSpec((d, tn), lambda i, j: (0, j)),
                  pl.BlockSpec((1, tn), lambda i, j: (0, j))],
        out_specs=out_spec,
        out_shape=out_shape,
        scratch_shapes=[pltpu.VMEM((tm, d), BF16)],
        compiler_params=_cparams(2, 40),
        name="norm_matmul",
    )(x, g.reshape(1, d), w, b.reshape(1, n))


def _matmul_res_kernel(x_ref, w_ref, b_ref, r_ref, o_ref):
    mix = jnp.dot(x_ref[...].astype(BF16), w_ref[...], preferred_element_type=F32) + b_ref[...]
    o_ref[...] = r_ref[...] + mix


def matmul_bias_res(x, w, b, res):
    t, k = x.shape
    n = w.shape[1]
    tm = _tile(t, (512, 256))
    return pl.pallas_call(
        _matmul_res_kernel,
        grid=(t // tm,),
        in_specs=[pl.BlockSpec((tm, k), lambda i: (i, 0)),
                  pl.BlockSpec((k, n), lambda i: (0, 0)),
                  pl.BlockSpec((1, n), lambda i: (0, 0)),
                  pl.BlockSpec((tm, n), lambda i: (i, 0))],
        out_specs=pl.BlockSpec((tm, n), lambda i: (i, 0)),
        out_shape=jax.ShapeDtypeStruct((t, n), F32),
        compiler_params=_cparams(1, 32),
        name="matmul_bias_res",
    )(x, w, b.reshape(1, n), res)


def _rmsnorm_kernel(x_ref, g_ref, o_ref):
    o_ref[...] = _rms(x_ref[...], g_ref[...])


def rmsnorm_rows(x, g):
    t, d = x.shape
    tm = _tile(t, (512, 256))
    return pl.pallas_call(
        _rmsnorm_kernel,
        grid=(t // tm,),
        in_specs=[pl.BlockSpec((tm, d), lambda i: (i, 0)), pl.BlockSpec((1, d), lambda i: (0, 0))],
        out_specs=pl.BlockSpec((tm, d), lambda i: (i, 0)),
        out_shape=jax.ShapeDtypeStruct((t, d), F32),
        compiler_params=_cparams(1, 32),
        name="rmsnorm",
    )(x, g.reshape(1, d))


def _conv_kernel(x_ref, halo_ref, prev_ref, w_ref, b_ref, o_ref, ext_ref, *, tl):
    first = pl.program_id(1) == 0
    ext_ref[0:SUBLANES, :] = jnp.where(first, prev_ref[...], halo_ref[...])
    ext_ref[SUBLANES:SUBLANES + tl, :] = x_ref[...]
    acc = b_ref[...]
    for k in range(CONV_WIDTH):
        off = SUBLANES - (CONV_WIDTH - 1) + k
        acc = acc + ext_ref[off:off + tl, :] * w_ref[k:k + 1, :]
    o_ref[...] = acc * jax.nn.sigmoid(acc)


def causal_conv_silu(proj, prev8, w, b, bsz, seq, col0):
    c = w.shape[1]
    tl = _tile(seq, (256, 128, 64, 32, 16))
    tc = 1024
    nl = seq // tl
    cb0 = col0 // tc
    hpb = tl // SUBLANES
    return pl.pallas_call(
        functools.partial(_conv_kernel, tl=tl),
        grid=(bsz, nl, c // tc),
        in_specs=[pl.BlockSpec((tl, tc), lambda bi, li, ci: (bi * nl + li, cb0 + ci)),
                  pl.BlockSpec((SUBLANES, tc),
                               lambda bi, li, ci: (jnp.maximum((bi * nl + li) * hpb - 1, 0), cb0 + ci)),
                  pl.BlockSpec((None, SUBLANES, tc), lambda bi, li, ci: (bi, 0, ci)),
                  pl.BlockSpec((CONV_WIDTH, tc), lambda bi, li, ci: (0, ci)),
                  pl.BlockSpec((1, tc), lambda bi, li, ci: (0, ci))],
        out_specs=pl.BlockSpec((tl, tc), lambda bi, li, ci: (bi * nl + li, ci)),
        out_shape=jax.ShapeDtypeStruct((bsz * seq, c), F32),
        scratch_shapes=[pltpu.VMEM((SUBLANES + tl, tc), F32)],
        compiler_params=_cparams(3, 32),
        name="conv_silu",
    )(proj, proj, prev8, w, b.reshape(1, c))


def _split3(x):
    hi = x.astype(BF16)
    r1 = x - hi.astype(F32)
    mid = r1.astype(BF16)
    lo = (r1 - mid.astype(F32)).astype(BF16)
    return hi, mid, lo


def _ssd_kernel(*refs, rows, has_init):
    if has_init:
        (x_ref, b_ref, c_ref, dt_ref, bias_ref, alog_ref, dskip_ref, s0_ref,
         y_ref, sout_ref, st_ref) = refs
    else:
        (x_ref, b_ref, c_ref, dt_ref, bias_ref, alog_ref, dskip_ref,
         y_ref, sout_ref, st_ref) = refs
    q = SSD_BLOCK
    gw = x_ref.shape[1] // SSM_GROUPS
    hpg = gw // SSM_HEAD_DIM
    ci = pl.program_id(1)

    @pl.when(ci == 0)
    def _():
        for g in range(SSM_GROUPS):
            if has_init:
                st_ref[g] = s0_ref[g * gw:(g + 1) * gw, :].T
            else:
                st_ref[g] = jnp.zeros((SSM_STATE, gw), F32)

    def pad(a):
        if rows == q:
            return a
        return jnp.concatenate([a, jnp.zeros((q - rows, a.shape[1]), a.dtype)], axis=0)

    raw = dt_ref[...] + bias_ref[...]
    dt = pad(jnp.maximum(raw, 0.0) + jnp.log1p(jnp.exp(-jnp.abs(raw))))
    a_neg = -jnp.exp(alog_ref[...])
    dta = dt * a_neg
    ri = lax.broadcasted_iota(jnp.int32, (q, q), 0)
    cj = lax.broadcasted_iota(jnp.int32, (q, q), 1)
    tri = ri >= cj
    trib = jnp.where(tri, 1.0, 0.0).astype(BF16)
    acum = sum(jnp.dot(trib, part, preferred_element_type=F32) for part in _split3(dta))
    acum_t = acum.T
    a_last = acum[q - 1:q, :]
    dec_end = jnp.exp(a_last - acum)
    ea = jnp.exp(acum)
    cdec = jnp.exp(a_last)
    wgt = dt * dec_end

    xs = pad(x_ref[...])
    bm = pad(b_ref[...])
    cm = pad(c_ref[...])

    def expand(a, g, nrow):
        return jnp.concatenate(
            [jnp.broadcast_to(a[:, h:h + 1], (nrow, SSM_HEAD_DIM)) for h in range(g * hpg, (g + 1) * hpg)],
            axis=1)

    for g in range(SSM_GROUPS):
        bg = bm[:, g * SSM_STATE:(g + 1) * SSM_STATE]
        cgb = cm[:, g * SSM_STATE:(g + 1) * SSM_STATE].astype(BF16)
        cb = lax.dot_general(cgb, bg.astype(BF16), _NT, preferred_element_type=F32)
        xg = xs[:, g * gw:(g + 1) * gw]
        xdt = (xg * expand(dt, g, q)).astype(BF16)
        xw = (xg * expand(wgt, g, q)).astype(BF16)
        st_prev = st_ref[g]
        y_off = jnp.dot(cgb, st_prev.astype(BF16), preferred_element_type=F32) * expand(ea, g, q)
        st_new = jnp.dot(bg.T.astype(BF16), xw, preferred_element_type=F32)
        st_ref[g] = expand(cdec, g, 1) * st_prev + st_new
        yd = []
        for r in range(hpg):
            h = g * hpg + r
            seg = acum[:, h:h + 1] - acum_t[h:h + 1, :]
            lmat = jnp.where(tri, jnp.exp(seg), 0.0)
            m = (cb * lmat).astype(BF16)
            yd.append(jnp.dot(m, xdt[:, r * SSM_HEAD_DIM:(r + 1) * SSM_HEAD_DIM],
                              preferred_element_type=F32))
        yg = jnp.concatenate(yd, axis=1) + y_off + xg * dskip_ref[:, g * gw:(g + 1) * gw]
        y_ref[:, g * gw:(g + 1) * gw] = yg[:rows]

    @pl.when(ci == pl.num_programs(1) - 1)
    def _():
        for g in range(SSM_GROUPS):
            sout_ref[g * gw:(g + 1) * gw, :] = st_ref[g].T


def ssd_scan(xbc, proj, dt_col, dt_bias, a_log, d_skip, s0, bsz, seq):
    inner = d_skip.shape[0] * SSM_HEAD_DIM
    gn = SSM_GROUPS * SSM_STATE
    rows = min(SSD_BLOCK, seq)
    nc = seq // rows
    heads = d_skip.shape[0]
    pad128 = lambda v: jnp.pad(v, (0, LANES - heads)).reshape(1, LANES)
    row_map = lambda bi, ci: bi * nc + ci
    in_specs = [pl.BlockSpec((rows, inner), lambda bi, ci: (row_map(bi, ci), 0)),
                pl.BlockSpec((rows, gn), lambda bi, ci: (row_map(bi, ci), inner // gn)),
                pl.BlockSpec((rows, gn), lambda bi, ci: (row_map(bi, ci), inner // gn + 1)),
                pl.BlockSpec((rows, LANES), lambda bi, ci: (row_map(bi, ci), dt_col // LANES)),
                pl.BlockSpec((1, LANES), lambda bi, ci: (0, 0)),
                pl.BlockSpec((1, LANES), lambda bi, ci: (0, 0)),
                pl.BlockSpec((1, inner), lambda bi, ci: (0, 0))]
    args = [xbc, xbc, xbc, proj, pad128(dt_bias), pad128(a_log),
            jnp.repeat(d_skip, SSM_HEAD_DIM).reshape(1, inner)]
    if s0 is not None:
        in_specs.append(pl.BlockSpec((None, inner, SSM_STATE), lambda bi, ci: (bi, 0, 0)))
        args.append(s0)
    return pl.pallas_call(
        functools.partial(_ssd_kernel, rows=rows, has_init=s0 is not None),
        grid=(bsz, nc),
        in_specs=in_specs,
        out_specs=[pl.BlockSpec((rows, inner), lambda bi, ci: (row_map(bi, ci), 0)),
                   pl.BlockSpec((None, inner, SSM_STATE), lambda bi, ci: (bi, 0, 0))],
        out_shape=[jax.ShapeDtypeStruct((bsz * seq, inner), F32),
                   jax.ShapeDtypeStruct((bsz, inner, SSM_STATE), F32)],
        scratch_shapes=[pltpu.VMEM((SSM_GROUPS, SSM_STATE, inner // SSM_GROUPS), F32)],
        compiler_params=_cparams(2, 40),
        name="ssd_scan",
    )(*args)


def _gated_out_kernel(y_ref, z_ref, g_ref, w_ref, r_ref, o_ref):
    z = z_ref[...]
    y = y_ref[...] * (z * jax.nn.sigmoid(z))
    yn = _rms(y, g_ref[...]).astype(BF16)
    o_ref[...] = r_ref[...] + jnp.dot(yn, w_ref[...], preferred_element_type=F32)


def gated_norm_out(y, proj, g, w, res):
    t, inner = y.shape
    n = w.shape[1]
    tm = _tile(t, (256, 128))
    return pl.pallas_call(
        _gated_out_kernel,
        grid=(t // tm,),
        in_specs=[pl.BlockSpec((tm, inner), lambda i: (i, 0)),
                  pl.BlockSpec((tm, inner), lambda i: (i, 0)),
                  pl.BlockSpec((1, inner), lambda i: (0, 0)),
                  pl.BlockSpec((inner, n), lambda i: (0, 0)),
                  pl.BlockSpec((tm, n), lambda i: (i, 0))],
        out_specs=pl.BlockSpec((tm, n), lambda i: (i, 0)),
        out_shape=jax.ShapeDtypeStruct((t, n), F32),
        compiler_params=_cparams(1, 40),
        name="gated_norm_out",
    )(y, proj, g.reshape(1, inner), w, res)


def _attn_banded_kernel(q_ref, sink_ref, kp_ref, kc_ref, vp_ref, vc_ref, o_ref):
    tq = q_ref.shape[0]
    hd = ATTN_HEAD_DIM
    qpk = q_ref.shape[1] // hd // KV_HEADS
    scale = hd ** -0.5
    nk = WINDOW + tq
    back = WINDOW // ATTN_CHUNK
    kc = lax.broadcasted_iota(jnp.int32, (nk, tq), 0) // ATTN_CHUNK
    qc = lax.broadcasted_iota(jnp.int32, (nk, tq), 1) // ATTN_CHUNK
    first_kc = jnp.where(pl.program_id(1) == 0, back, 0)
    valid = (kc >= qc) & (kc <= qc + back) & (kc >= first_kc)
    bias = jnp.concatenate([jnp.where(valid, 0.0, -jnp.inf)] * qpk, axis=1)
    for pair in range(KV_HEADS // 2):
        cols = slice(pair * 2 * hd, (pair + 1) * 2 * hd)
        k2 = jnp.concatenate([kp_ref[:, cols], kc_ref[:, cols]], axis=0)
        v2t = jnp.concatenate([vp_ref[:, cols], vc_ref[:, cols]], axis=0).T
        for sub in range(2):
            kh = pair * 2 + sub
            heads = range(kh * qpk, (kh + 1) * qpk)
            kmat = k2[:, sub * hd:(sub + 1) * hd].astype(BF16)
            vt = v2t[sub * hd:(sub + 1) * hd, :].astype(BF16)
            q4 = jnp.concatenate([q_ref[:, h * hd:(h + 1) * hd] for h in heads], axis=0).astype(BF16)
            s = lax.dot_general(kmat, q4, _NT, preferred_element_type=F32) * scale + bias
            sink = jnp.concatenate([jnp.broadcast_to(sink_ref[:, h:h + 1], (1, tq)) for h in heads], axis=1)
            m = jnp.maximum(jnp.max(s, axis=0, keepdims=True), sink)
            p = jnp.exp(s - m)
            den = jnp.sum(p, axis=0, keepdims=True) + jnp.exp(sink - m)
            ot = jnp.dot(vt, (p * (1.0 / den)).astype(BF16), preferred_element_type=F32)
            for g in range(0, qpk, 2):
                two = jnp.concatenate([ot[:, g * tq:(g + 1) * tq], ot[:, (g + 1) * tq:(g + 2) * tq]], axis=0)
                o_ref[:, (kh * qpk + g) * hd:(kh * qpk + g + 2) * hd] = two.T


def _attn_rows_kernel(q_ref, sink_ref, k_ref, v_ref, o_ref):
    tq = q_ref.shape[0]
    hd = ATTN_HEAD_DIM
    qpk = q_ref.shape[1] // hd // KV_HEADS
    scale = hd ** -0.5
    outs = []
    for kh in range(KV_HEADS):
        kmat = k_ref[:, kh * hd:(kh + 1) * hd].astype(BF16)
        vmat = v_ref[:, kh * hd:(kh + 1) * hd].astype(BF16)
        heads = range(kh * qpk, (kh + 1) * qpk)
        q4 = jnp.concatenate([q_ref[:, h * hd:(h + 1) * hd] for h in heads], axis=0).astype(BF16)
        s = lax.dot_general(q4, kmat, _NT, preferred_element_type=F32) * scale
        sink = jnp.concatenate([jnp.broadcast_to(sink_ref[:, h:h + 1], (tq, 1)) for h in heads], axis=0)
        m = jnp.maximum(jnp.max(s, axis=-1, keepdims=True), sink)
        p = jnp.exp(s - m)
        den = jnp.sum(p, axis=-1, keepdims=True) + jnp.exp(sink - m)
        o4 = jnp.dot((p / den).astype(BF16), vmat, preferred_element_type=F32)
        outs += [o4[g * tq:(g + 1) * tq] for g in range(qpk)]
    o_ref[...] = jnp.concatenate(outs, axis=1)


def swa_attention(q, sinks, k_src, v_src, k_col, v_col, bsz, seq, kv_rows):
    t, hqd = q.shape
    kvd = KV_HEADS * ATTN_HEAD_DIM
    n_heads = sinks.shape[0]
    if kv_rows is None:
        body = _attn_banded_kernel
        tq = _tile(seq, (256, WINDOW))
        nq = seq // tq
        per = tq // WINDOW
        kv_specs = lambda col: [
            pl.BlockSpec((WINDOW, kvd), lambda bi, ci: ((bi * nq + ci) * per - jnp.minimum(ci, 1), col)),
            pl.BlockSpec((tq, kvd), lambda bi, ci: (bi * nq + ci, col))]
    else:
        body = _attn_rows_kernel
        tq, nq = seq, 1
        kv_specs = lambda col: [pl.BlockSpec((kv_rows, kvd), lambda bi, ci: (bi, col))]
    k_specs, v_specs = kv_specs(k_col), kv_specs(v_col)
    in_specs = ([pl.BlockSpec((tq, hqd), lambda bi, ci: (bi * nq + ci, 0)),
                 pl.BlockSpec((1, LANES), lambda bi, ci: (0, 0))] + k_specs + v_specs)
    return pl.pallas_call(
        body,
        grid=(bsz, nq),
        in_specs=in_specs,
        out_specs=pl.BlockSpec((tq, hqd), lambda bi, ci: (bi * nq + ci, 0)),
        out_shape=jax.ShapeDtypeStruct((t, hqd), F32),
        compiler_params=_cparams(2, 40),
        name="swa_attention",
    )(q, jnp.pad(sinks, (0, LANES - n_heads)).reshape(1, LANES),
      *([k_src] * len(k_specs)), *([v_src] * len(v_specs)))


_PAIRS = [(a, b) for a in range(PEER_TOPK) for b in range(PEER_TOPK) if (a + 1) * (b + 1) <= PEER_TOPK]
_CAND_ROWS = -(-len(_PAIRS) // SUBLANES) * SUBLANES


def _extract_topk(s, n_rounds, break_ties):
    n = s.shape[0]
    iota = lax.broadcasted_iota(jnp.int32, s.shape, 0)
    rank = jnp.full(s.shape, n_rounds, jnp.int32)
    vals = []
    for r in range(n_rounds):
        m = jnp.max(s, axis=0, keepdims=True)
        sel = s == m
        if break_ties:
            sel = iota == jnp.min(jnp.where(sel, iota, n), axis=0, keepdims=True)
        rank = jnp.where(sel, r, rank)
        s = jnp.where(sel, -jnp.inf, s)
        vals.append(m)
    count = jnp.sum(jnp.where(rank < n_rounds, 1.0, 0.0), axis=0, keepdims=True)
    return rank, vals, count


def _peer_topk_kernel(q_ref, k1_ref, k2_ref, e1_ref, c1_ref, e2_ref, r2_ref,
                      s_ref, v_ref, cand_ref, sel_ref):
    half = k1_ref.shape[1]
    tm = q_ref.shape[1]

    def select(h, break_ties):
        tied = jnp.zeros((1, tm), F32)
        for lc in range(tm // LANES):
            lanes = slice(lc * LANES, (lc + 1) * LANES)
            rank1, v1, n1 = _extract_topk(s_ref[0, :, lanes], PEER_TOPK, break_ties)
            rank2, v2, n2 = _extract_topk(s_ref[1, :, lanes], PEER_TOPK, break_ties)
            c1_ref[h, :, lanes] = rank1.astype(F32)
            r2_ref[h, :, lanes] = rank2.astype(F32).astype(BF16)
            for r in range(PEER_TOPK):
                v_ref[0, r:r + 1, lanes] = v1[r]
                v_ref[1, r:r + 1, lanes] = v2[r]
            sel_ref[0:1, lanes] = jnp.where((n1 != PEER_TOPK) | (n2 != PEER_TOPK), 1.0, 0.0)
        tied = sel_ref[0:1, :]
        cand_ref[...] = jnp.full((_CAND_ROWS, tm), -jnp.inf, F32)
        for i, (a, b) in enumerate(_PAIRS):
            cand_ref[i:i + 1, :] = v_ref[0, a:a + 1, :] + v_ref[1, b:b + 1, :]
        cand = cand_ref[...]
        crank, _, nc = _extract_topk(cand, PEER_TOPK, break_ties)
        tied = jnp.maximum(tied, jnp.where(nc != PEER_TOPK, 1.0, 0.0))
        chosen = crank < PEER_TOPK
        z = 2.0 * jnp.sum(jnp.where(chosen, jnp.exp(cand - cand[0:1]), 0.0), axis=0, keepdims=True)
        sel_ref[...] = jnp.where(chosen, 1.0, 0.0)
        cnts = []
        row = 0
        for a in range(PEER_TOPK):
            nb = sum(1 for (pa, _) in _PAIRS if pa == a)
            cnts.append(jnp.sum(sel_ref[row:row + nb, :], axis=0, keepdims=True))
            row += nb
        for lc in range(tm // LANES):
            lanes = slice(lc * LANES, (lc + 1) * LANES)
            rank1 = c1_ref[h, :, lanes]
            c1 = jnp.zeros(rank1.shape, F32)
            for a in range(PEER_TOPK):
                c1 = jnp.where(rank1 == a, cnts[a][:, lanes], c1)
            c1_ref[h, :, lanes] = c1
            e1_ref[h, :, lanes] = jnp.exp(s_ref[0, :, lanes] - v_ref[0, 0:1, lanes])
            e2_ref[h, :, lanes] = (jnp.exp(s_ref[1, :, lanes] - v_ref[1, 0:1, lanes])
                                   / z[:, lanes]).astype(BF16)
        return tied

    def per_head(h, carry):
        qh = q_ref[h].astype(BF16)
        s_ref[0] = lax.dot_general(k1_ref[...], qh[:, :half], _NT, preferred_element_type=F32)
        s_ref[1] = lax.dot_general(k2_ref[...], qh[:, half:], _NT, preferred_element_type=F32)
        tied = select(h, break_ties=False)

        @pl.when(jnp.max(tied) > 0.0)
        def _():
            select(h, break_ties=True)

        return carry

    lax.fori_loop(0, q_ref.shape[0], per_head, 0)


def peer_topk(qh, k1, k2):
    nh, t, qd = qh.shape
    nk = k1.shape[0]
    tm = _tile(t, (2 * LANES, LANES))
    out32 = jax.ShapeDtypeStruct((nh, nk, t), F32)
    out16 = jax.ShapeDtypeStruct((nh, nk, t), BF16)
    ospec = pl.BlockSpec((nh, nk, tm), lambda i: (0, 0, i))
    return pl.pallas_call(
        _peer_topk_kernel,
        grid=(t // tm,),
        in_specs=[pl.BlockSpec((nh, tm, qd), lambda i: (0, i, 0)),
                  pl.BlockSpec((nk, qd // 2), lambda i: (0, 0)),
                  pl.BlockSpec((nk, qd // 2), lambda i: (0, 0))],
        out_specs=[ospec] * 4,
        out_shape=[out32, out32, out16, out16],
        scratch_shapes=[pltpu.VMEM((2, nk, tm), F32), pltpu.VMEM((2, PEER_TOPK, tm), F32),
                        pltpu.VMEM((_CAND_ROWS, tm), F32), pltpu.VMEM((_CAND_ROWS, tm), F32)],
        compiler_params=_cparams(1, 32),
        name="peer_topk",
    )(qh, k1, k2)


def _peer_dense_kernel(h_ref, g_ref, u_ref, vt_ref, e1_ref, c1_ref, e2_ref, r2_ref, o_ref,
                       xn_ref, acc_ref, gate_ref):
    ki = pl.program_id(1)
    nh, nkeys, tm = e2_ref.shape
    te = u_ref.shape[0]

    @pl.when(ki == 0)
    def _():
        xn_ref[...] = _rms(h_ref[...], g_ref[...]).astype(BF16)
        acc_ref[...] = jnp.zeros(acc_ref.shape, F32)

    zero = jnp.zeros((nkeys, LANES), BF16)
    for r in range(te // nkeys):
        rows = slice(r * nkeys, (r + 1) * nkeys)
        for lc in range(tm // LANES):
            lanes = slice(lc * LANES, (lc + 1) * LANES)
            w = zero
            for h in range(nh):
                c1b = jnp.broadcast_to(c1_ref[h, r:r + 1, lanes].astype(BF16), (nkeys, LANES))
                e1b = jnp.broadcast_to(e1_ref[h, r:r + 1, lanes].astype(BF16), (nkeys, LANES))
                w = w + jnp.where(r2_ref[h, :, lanes] < c1b, e2_ref[h, :, lanes], zero) * e1b
            gate_ref[rows, lanes] = w
    x = lax.dot_general(u_ref[...], xn_ref[...], _NT, preferred_element_type=F32).astype(BF16)
    act2 = x + x * jnp.tanh(x * (_GELU_C1 + _GELU_C3 * (x * x)))
    wact = gate_ref[...] * act2
    acc_ref[...] += jnp.dot(vt_ref[...], wact, preferred_element_type=F32)

    @pl.when(ki == pl.num_programs(1) - 1)
    def _():
        o_ref[...] = h_ref[...] + acc_ref[...].T


def peer_dense(h, g, u, vt, e1, c1, e2, r2):
    t, d = h.shape
    ne = u.shape[0]
    nh, nk, _ = e1.shape
    tm = _tile(t, (512, 256))
    te = 1024
    key1_spec = pl.BlockSpec((nh, te // nk, tm), lambda i, k: (0, k, i))
    key2_spec = pl.BlockSpec((nh, nk, tm), lambda i, k: (0, 0, i))
    return pl.pallas_call(
        _peer_dense_kernel,
        grid=(t // tm, ne // te),
        in_specs=[pl.BlockSpec((tm, d), lambda i, k: (i, 0)),
                  pl.BlockSpec((1, d), lambda i, k: (0, 0)),
                  pl.BlockSpec((te, d), lambda i, k: (k, 0)),
                  pl.BlockSpec((d, te), lambda i, k: (0, k)),
                  key1_spec, key1_spec, key2_spec, key2_spec],
        out_specs=pl.BlockSpec((tm, d), lambda i, k: (i, 0)),
        out_shape=jax.ShapeDtypeStruct((t, d), F32),
        scratch_shapes=[pltpu.VMEM((tm, d), BF16), pltpu.VMEM((d, tm), F32),
                        pltpu.VMEM((te, tm), BF16)],
        compiler_params=_cparams(2, 48),
        name="peer_dense",
    )(h, g.reshape(1, d), u, vt, e1, c1, e2, r2)


def kernel(x_prompt, x_sample, state_ssm, state_conv, cache_k_win, cache_v_win, norm_mix, norm_ffn, norm_kv, norm_final, m_w_in, m_conv_w, m_conv_b, m_dt_bias, m_a_log, m_d_skip, m_norm, m_w_out, a_w_kv, a_b_kv, a_w_q, a_b_q, a_sinks, a_w_o, a_b_o, p_w_q, p_sub_k1, p_sub_k2, p_u, p_v):
    depth = norm_mix.shape[0]
    n_a = m_w_in.shape[0]
    d = x_prompt.shape[-1]
    inner = m_w_out.shape[1]
    conv_dim = m_conv_w.shape[2]
    heads = m_d_skip.shape[1]
    kvd = a_w_kv.shape[1] // 2
    qdim = p_w_q.shape[2] // PEER_HEADS
    dt_col = inner + conv_dim
    in_pad = dt_col + LANES

    w_in = jnp.pad(m_w_in, ((0, 0), (0, 0), (0, in_pad - m_w_in.shape[2]))).astype(BF16)
    w_out = m_w_out.astype(BF16)
    w_kv = a_w_kv.astype(BF16)
    w_q = a_w_q.astype(BF16)
    w_o = a_w_o.astype(BF16)
    pw_q = p_w_q.astype(BF16)
    k1 = p_sub_k1.astype(BF16)
    k2 = p_sub_k2.astype(BF16)
    u_tab = p_u.astype(BF16)
    vt_tab = jnp.swapaxes(p_v.astype(BF16), 1, 2)
    zeros_in = jnp.zeros((in_pad,), F32)
    zeros_pq = jnp.zeros((p_w_q.shape[2],), F32)
    in_tn = _tile(in_pad, (896, 128))

    def run(x, ssm_init, conv_init, k_prev, v_prev):
        bsz, seq, _ = x.shape
        h = x.reshape(bsz * seq, d)
        ssm_out, conv_out = [], []
        k_src = v_src = None
        for l in range(depth):
            if l < n_a:
                proj = norm_matmul(h, norm_mix[l], w_in[l], zeros_in, in_tn)
                if conv_init is None:
                    prev8 = jnp.zeros((bsz, SUBLANES, conv_dim), F32)
                else:
                    prev8 = jnp.pad(conv_init[l], ((0, 0), (SUBLANES - (CONV_WIDTH - 1), 0), (0, 0)))
                xbc = causal_conv_silu(proj, prev8, m_conv_w[l], m_conv_b[l], bsz, seq, inner)
                s0 = None if ssm_init is None else ssm_init[l].reshape(bsz, inner, SSM_STATE)
                y, s_new = ssd_scan(xbc, proj, dt_col, m_dt_bias[l], m_a_log[l], m_d_skip[l], s0, bsz, seq)
                ssm_out.append(s_new.reshape(bsz, heads, SSM_HEAD_DIM, SSM_STATE))
                raw = proj[:, inner:dt_col].reshape(bsz, seq, conv_dim)
                conv_out.append(raw[:, seq - (CONV_WIDTH - 1):])
                h = gated_norm_out(y, proj, m_norm[l], w_out[l], h)
            else:
                j = l - n_a
                if j == 0:
                    kv = norm_matmul(h, norm_kv, w_kv, a_b_kv, 2 * kvd)
                    kv3 = kv.reshape(bsz, seq, 2 * kvd)
                    if k_prev is not None:
                        k_all = jnp.concatenate([k_prev.reshape(bsz, WINDOW, kvd), kv3[:, :, :kvd]], axis=1)
                        v_all = jnp.concatenate([v_prev.reshape(bsz, WINDOW, kvd), kv3[:, :, kvd:]], axis=1)
                        k_src = k_all.reshape(bsz * (WINDOW + seq), kvd)
                        v_src = v_all.reshape(bsz * (WINDOW + seq), kvd)
                        k_win, v_win = k_all[:, -WINDOW:], v_all[:, -WINDOW:]
                    else:
                        k_win, v_win = kv3[:, -WINDOW:, :kvd], kv3[:, -WINDOW:, kvd:]
                qp = norm_matmul(h, norm_mix[l], w_q[j], a_b_q[j], w_q.shape[2])
                if k_prev is None:
                    o = swa_attention(qp, a_sinks[j], kv, kv, 0, 1, bsz, seq, None)
                else:
                    o = swa_attention(qp, a_sinks[j], k_src, v_src, 0, 0, bsz, seq, WINDOW + seq)
                h = matmul_bias_res(o, w_o[j], a_b_o[j], h)
            qh = norm_matmul(h, norm_ffn[l], pw_q[l], zeros_pq, pw_q.shape[2] // 2, head_dim=qdim)
            e1, c1, e2, r2 = peer_topk(qh, k1[l], k2[l])
            h = peer_dense(h, norm_ffn[l], u_tab[l], vt_tab[l], e1, c1, e2, r2)
        y = rmsnorm_rows(h, norm_final).reshape(bsz, seq, d)
        kshape = (bsz, WINDOW, KV_HEADS, ATTN_HEAD_DIM)
        return y, jnp.stack(ssm_out), jnp.stack(conv_out), k_win.reshape(kshape), v_win.reshape(kshape)

    y_p, p_ssm, p_conv, p_k, p_v_win = run(x_prompt, None, None, None, None)
    y_s, s_ssm, s_conv, s_k, s_v = run(x_sample, state_ssm, state_conv, cache_k_win, cache_v_win)
    return (y_p, y_s, p_ssm, p_conv, p_k, p_v_win, s_ssm, s_conv, s_k, s_v)
```

```python
import functools

import jax
import jax.numpy as jnp
from jax import lax
from jax.experimental import pallas as pl
from jax.experimental.pallas import tpu as pltpu

F32 = jnp.float32
BF16 = jnp.bfloat16
EPS = 1e-6

WINDOW = 128
SSM_HEAD_DIM = 64
SSM_STATE = 128
SSM_GROUPS = 8
CONV_WIDTH = 4
SSD_BLOCK = 128
ATTN_CHUNK = 64
ATTN_HEAD_DIM = 64
KV_HEADS = 4
PEER_HEADS = 8
PEER_NKEYS = 128
PEER_TOPK = 16
LANES = 128
SUBLANES = 8
MIB = 1024 * 1024

_NT = (((1,), (1,)), ((), ()))
_GELU_C1 = 0.7978845608028654
_GELU_C3 = _GELU_C1 * 0.044715


def _cparams(n_axes, vmem_mib, flags=None):
    return pltpu.CompilerParams(
        dimension_semantics=("arbitrary",) * n_axes, vmem_limit_bytes=vmem_mib * MIB, flags=flags)


def _tile(n, prefs):
    for p in prefs:
        if n % p == 0:
            return p
    return n


def _rms(x, g):
    ms = jnp.mean(x * x, axis=-1, keepdims=True)
    return x * lax.rsqrt(ms + EPS) * g


def _norm_matmul_kernel(x_ref, g_ref, w_ref, b_ref, o_ref, xn_ref, *, head_dim):
    @pl.when(pl.program_id(1) == 0)
    def _():
        xn_ref[...] = _rms(x_ref[...], g_ref[...]).astype(BF16)

    res = jnp.dot(xn_ref[...], w_ref[...], preferred_element_type=F32) + b_ref[...]
    if head_dim is None:
        o_ref[...] = res
    else:
        for k in range(o_ref.shape[0]):
            o_ref[k] = res[:, k * head_dim:(k + 1) * head_dim]


def norm_matmul(x, g, w, b, tn, head_dim=None):
    t, d = x.shape
    n = w.shape[1]
    tm = _tile(t, (1024, 512, 256))
    if head_dim is not None:
        out_shape = jax.ShapeDtypeStruct((n // head_dim, t, head_dim), F32)
        out_spec = pl.BlockSpec((tn // head_dim, tm, head_dim), lambda i, j: (j, i, 0))
    else:
        out_shape = jax.ShapeDtypeStruct((t, n), F32)
        out_spec = pl.BlockSpec((tm, tn), lambda i, j: (i, j))
    return pl.pallas_call(
        functools.partial(_norm_matmul_kernel, head_dim=head_dim),
        grid=(t // tm, n // tn),
        in_specs=[pl.BlockSpec((tm, d), lambda i, j: (i, 0)),
                  pl.BlockSpec((1, d), lambda i, j: (0, 0)),
                  pl.BlockSpec((d, tn), lambda i, j: (0, j)),
                  pl.BlockSpec((1, tn), lambda i, j: (0, j))],
        out_specs=out_spec,
        out_shape=out_shape,
        scratch_shapes=[pltpu.VMEM((tm, d), BF16)],
        compiler_params=_cparams(2, 40),
        name="norm_matmul",
    )(x, g.reshape(1, d), w, b.reshape(1, n))


def _matmul_res_kernel(x_ref, w_ref, b_ref, r_ref, o_ref):
    mix = jnp.dot(x_ref[...].astype(BF16), w_ref[...], preferred_element_type=F32) + b_ref[...]
    o_ref[...] = r_ref[...] + mix


def matmul_bias_res(x, w, b, res):
    t, k = x.shape
    n = w.shape[1]
    tm = _tile(t, (512, 256))
    return pl.pallas_call(
        _matmul_res_kernel,
        grid=(t // tm,),
        in_specs=[pl.BlockSpec((tm, k), lambda i: (i, 0)),
                  pl.BlockSpec((k, n), lambda i: (0, 0)),
                  pl.BlockSpec((1, n), lambda i: (0, 0)),
                  pl.BlockSpec((tm, n), lambda i: (i, 0))],
        out_specs=pl.BlockSpec((tm, n), lambda i: (i, 0)),
        out_shape=jax.ShapeDtypeStruct((t, n), F32),
        compiler_params=_cparams(1, 32),
        name="matmul_bias_res",
    )(x, w, b.reshape(1, n), res)


def _rmsnorm_kernel(x_ref, g_ref, o_ref):
    o_ref[...] = _rms(x_ref[...], g_ref[...])


def rmsnorm_rows(x, g):
    t, d = x.shape
    tm = _tile(t, (512, 256))
    return pl.pallas_call(
        _rmsnorm_kernel,
        grid=(t // tm,),
        in_specs=[pl.BlockSpec((tm, d), lambda i: (i, 0)), pl.BlockSpec((1, d), lambda i: (0, 0))],
        out_specs=pl.BlockSpec((tm, d), lambda i: (i, 0)),
        out_shape=jax.ShapeDtypeStruct((t, d), F32),
        compiler_params=_cparams(1, 32),
        name="rmsnorm",
    )(x, g.reshape(1, d))


def _conv_kernel(x_ref, halo_ref, prev_ref, w_ref, b_ref, o_ref):
    first = pl.program_id(1) == 0
    x = x_ref[...]
    ext = jnp.concatenate([jnp.where(first, prev_ref[...], halo_ref[...]), x], axis=0)
    acc = b_ref[...]
    for k in range(CONV_WIDTH - 1):
        tap = pltpu.roll(ext, CONV_WIDTH - 1 - k, 0)[SUBLANES:]
        acc = acc + tap * w_ref[k:k + 1, :]
    acc = acc + x * w_ref[CONV_WIDTH - 1:CONV_WIDTH, :]
    o_ref[...] = acc * jax.nn.sigmoid(acc)


def causal_conv_silu(proj, prev8, w, b, bsz, seq, col0):
    c = w.shape[1]
    tl = _tile(seq, (256, 128, 64, 32, 16))
    tc = 1024
    nl = seq // tl
    cb0 = col0 // tc
    hpb = tl // SUBLANES
    return pl.pallas_call(
        _conv_kernel,
        grid=(bsz, nl, c // tc),
        in_specs=[pl.BlockSpec((tl, tc), lambda bi, li, ci: (bi * nl + li, cb0 + ci)),
                  pl.BlockSpec((SUBLANES, tc),
                               lambda bi, li, ci: (jnp.maximum((bi * nl + li) * hpb - 1, 0), cb0 + ci)),
                  pl.BlockSpec((None, SUBLANES, tc), lambda bi, li, ci: (bi, 0, ci)),
                  pl.BlockSpec((CONV_WIDTH, tc), lambda bi, li, ci: (0, ci)),
                  pl.BlockSpec((1, tc), lambda bi, li, ci: (0, ci))],
        out_specs=pl.BlockSpec((tl, tc), lambda bi, li, ci: (bi * nl + li, ci)),
        out_shape=jax.ShapeDtypeStruct((bsz * seq, c), F32),
        compiler_params=_cparams(3, 32),
        name="conv_silu",
    )(proj, proj, prev8, w, b.reshape(1, c))


def _split3(x):
    hi = x.astype(BF16)
    r1 = x - hi.astype(F32)
    mid = r1.astype(BF16)
    lo = (r1 - mid.astype(F32)).astype(BF16)
    return hi, mid, lo


def _ssd_kernel(*refs, rows, has_init):
    if has_init:
        (x_ref, b_ref, c_ref, dt_ref, bias_ref, alog_ref, dskip_ref, echan_ref, ecol_ref, s0_ref,
         y_ref, sout_ref, st_ref) = refs
    else:
        (x_ref, b_ref, c_ref, dt_ref, bias_ref, alog_ref, dskip_ref, echan_ref, ecol_ref,
         y_ref, sout_ref, st_ref) = refs
    q = SSD_BLOCK
    gw = x_ref.shape[1] // SSM_GROUPS
    hpg = gw // SSM_HEAD_DIM
    ci = pl.program_id(1)

    @pl.when(ci == 0)
    def _():
        for g in range(SSM_GROUPS):
            if has_init:
                st_ref[g] = s0_ref[g * gw:(g + 1) * gw, :].T
            else:
                st_ref[g] = jnp.zeros((SSM_STATE, gw), F32)

    def pad(a):
        if rows == q:
            return a
        return jnp.concatenate([a, jnp.zeros((q - rows, a.shape[1]), a.dtype)], axis=0)

    raw = dt_ref[...] + bias_ref[...]
    dt = pad(jnp.maximum(raw, 0.0) + jnp.log1p(jnp.exp(-jnp.abs(raw))))
    a_neg = -jnp.exp(alog_ref[...])
    dta = dt * a_neg
    ri = lax.broadcasted_iota(jnp.int32, (q, q), 0)
    cj = lax.broadcasted_iota(jnp.int32, (q, q), 1)
    tri = ri >= cj
    trib = jnp.where(tri, 1.0, 0.0).astype(BF16)
    acum = sum(jnp.dot(trib, part, preferred_element_type=F32) for part in _split3(dta))
    acum_t = acum.T
    a_last = acum[q - 1:q, :]
    dec_end = jnp.exp(a_last - acum)
    ea = jnp.exp(acum)
    wgt = dt * dec_end

    two = lambda a: jnp.concatenate(_split3(a)[:2], axis=1)
    chan = jnp.dot(jnp.concatenate([two(dt), two(wgt), two(ea)], axis=0), echan_ref[...],
                   preferred_element_type=F32)
    dt_e, wgt_e, ea_e = chan[0:q], chan[q:2 * q], chan[2 * q:3 * q]
    acum_cols = jnp.dot(jnp.concatenate(_split3(acum), axis=1), ecol_ref[...],
                        preferred_element_type=F32)

    xs = pad(x_ref[...])
    bm = pad(b_ref[...])
    cm = pad(c_ref[...])

    for g in range(SSM_GROUPS):
        gsl = slice(g * gw, (g + 1) * gw)
        bg = bm[:, g * SSM_STATE:(g + 1) * SSM_STATE]
        cgb = cm[:, g * SSM_STATE:(g + 1) * SSM_STATE].astype(BF16)
        cb = lax.dot_general(cgb, bg.astype(BF16), _NT, preferred_element_type=F32)
        xg = xs[:, gsl]
        xdt = (xg * dt_e[:, gsl]).astype(BF16)
        xw = (xg * wgt_e[:, gsl]).astype(BF16)
        st_prev = st_ref[g]
        y_off = jnp.dot(cgb, st_prev.astype(BF16), preferred_element_type=F32) * ea_e[:, gsl]
        st_new = jnp.dot(bg.T.astype(BF16), xw, preferred_element_type=F32)
        st_ref[g] = ea_e[q - 1:q, gsl] * st_prev + st_new
        yd = []
        for r in range(hpg):
            h = g * hpg + r
            seg = acum_cols[:, h * q:(h + 1) * q] - acum_t[h:h + 1, :]
            lmat = jnp.where(tri, jnp.exp(seg), 0.0)
            m = (cb * lmat).astype(BF16)
            yd.append(jnp.dot(m, xdt[:, r * SSM_HEAD_DIM:(r + 1) * SSM_HEAD_DIM],
                              preferred_element_type=F32))
        yg = jnp.concatenate(yd, axis=1) + y_off + xg * dskip_ref[:, g * gw:(g + 1) * gw]
        y_ref[:, g * gw:(g + 1) * gw] = yg[:rows]

    @pl.when(ci == pl.num_programs(1) - 1)
    def _():
        for g in range(SSM_GROUPS):
            sout_ref[g * gw:(g + 1) * gw, :] = st_ref[g].T


def ssd_scan(xbc, proj, dt_col, dt_bias, a_log, d_skip, s0, bsz, seq):
    inner = d_skip.shape[0] * SSM_HEAD_DIM
    gn = SSM_GROUPS * SSM_STATE
    rows = min(SSD_BLOCK, seq)
    nc = seq // rows
    heads = d_skip.shape[0]
    pad128 = lambda v: jnp.pad(v, (0, LANES - heads)).reshape(1, LANES)
    row_map = lambda bi, ci: bi * nc + ci
    in_specs = [pl.BlockSpec((rows, inner), lambda bi, ci: (row_map(bi, ci), 0)),
                pl.BlockSpec((rows, gn), lambda bi, ci: (row_map(bi, ci), inner // gn)),
                pl.BlockSpec((rows, gn), lambda bi, ci: (row_map(bi, ci), inner // gn + 1)),
                pl.BlockSpec((rows, LANES), lambda bi, ci: (row_map(bi, ci), dt_col // LANES)),
                pl.BlockSpec((1, LANES), lambda bi, ci: (0, 0)),
                pl.BlockSpec((1, LANES), lambda bi, ci: (0, 0)),
                pl.BlockSpec((1, inner), lambda bi, ci: (0, 0)),
                pl.BlockSpec((2 * LANES, inner), lambda bi, ci: (0, 0)),
                pl.BlockSpec((3 * LANES, heads * SSD_BLOCK), lambda bi, ci: (0, 0))]
    head_of = lambda width: (jnp.arange(heads * width) // width)[None, :] == jnp.arange(LANES)[:, None]
    echan = jnp.tile(head_of(SSM_HEAD_DIM).astype(BF16), (2, 1))
    ecol = jnp.tile(head_of(SSD_BLOCK).astype(BF16), (3, 1))
    args = [xbc, xbc, xbc, proj, pad128(dt_bias), pad128(a_log),
            jnp.repeat(d_skip, SSM_HEAD_DIM).reshape(1, inner), echan, ecol]
    if s0 is not None:
        in_specs.append(pl.BlockSpec((None, inner, SSM_STATE), lambda bi, ci: (bi, 0, 0)))
        args.append(s0)
    return pl.pallas_call(
        functools.partial(_ssd_kernel, rows=rows, has_init=s0 is not None),
        grid=(bsz, nc),
        in_specs=in_specs,
        out_specs=[pl.BlockSpec((rows, inner), lambda bi, ci: (row_map(bi, ci), 0)),
                   pl.BlockSpec((None, inner, SSM_STATE), lambda bi, ci: (bi, 0, 0))],
        out_shape=[jax.ShapeDtypeStruct((bsz * seq, inner), F32),
                   jax.ShapeDtypeStruct((bsz, inner, SSM_STATE), F32)],
        scratch_shapes=[pltpu.VMEM((SSM_GROUPS, SSM_STATE, inner // SSM_GROUPS), F32)],
        compiler_params=_cparams(2, 48),
        name="ssd_scan",
    )(*args)


def _gated_out_kernel(y_ref, z_ref, g_ref, w_ref, r_ref, o_ref):
    z = z_ref[...]
    y = y_ref[...] * (z * jax.nn.sigmoid(z))
    yn = _rms(y, g_ref[...]).astype(BF16)
    o_ref[...] = r_ref[...] + jnp.dot(yn, w_ref[...], preferred_element_type=F32)


def gated_norm_out(y, proj, g, w, res):
    t, inner = y.shape
    n = w.shape[1]
    tm = _tile(t, (256, 128))
    return pl.pallas_call(
        _gated_out_kernel,
        grid=(t // tm,),
        in_specs=[pl.BlockSpec((tm, inner), lambda i: (i, 0)),
                  pl.BlockSpec((tm, inner), lambda i: (i, 0)),
                  pl.BlockSpec((1, inner), lambda i: (0, 0)),
                  pl.BlockSpec((inner, n), lambda i: (0, 0)),
                  pl.BlockSpec((tm, n), lambda i: (i, 0))],
        out_specs=pl.BlockSpec((tm, n), lambda i: (i, 0)),
        out_shape=jax.ShapeDtypeStruct((t, n), F32),
        compiler_params=_cparams(1, 40),
        name="gated_norm_out",
    )(y, proj, g.reshape(1, inner), w, res)


def _attn_banded_kernel(q_ref, sink_ref, kp_ref, kc_ref, vp_ref, vc_ref, o_ref):
    tq = q_ref.shape[0]
    hd = ATTN_HEAD_DIM
    qpk = q_ref.shape[1] // hd // KV_HEADS
    scale = hd ** -0.5
    nk = WINDOW + tq
    back = WINDOW // ATTN_CHUNK
    kc = lax.broadcasted_iota(jnp.int32, (nk, tq), 0) // ATTN_CHUNK
    qc = lax.broadcasted_iota(jnp.int32, (nk, tq), 1) // ATTN_CHUNK
    first_kc = jnp.where(pl.program_id(1) == 0, back, 0)
    valid = (kc >= qc) & (kc <= qc + back) & (kc >= first_kc)
    bias = jnp.concatenate([jnp.where(valid, 0.0, -jnp.inf)] * qpk, axis=1)
    for pair in range(KV_HEADS // 2):
        cols = slice(pair * 2 * hd, (pair + 1) * 2 * hd)
        k2 = jnp.concatenate([kp_ref[:, cols], kc_ref[:, cols]], axis=0)
        v2t = jnp.concatenate([vp_ref[:, cols], vc_ref[:, cols]], axis=0).T
        for sub in range(2):
            kh = pair * 2 + sub
            heads = range(kh * qpk, (kh + 1) * qpk)
            kmat = k2[:, sub * hd:(sub + 1) * hd].astype(BF16)
            vt = v2t[sub * hd:(sub + 1) * hd, :].astype(BF16)
            q4 = jnp.concatenate([q_ref[:, h * hd:(h + 1) * hd] for h in heads], axis=0).astype(BF16)
            s = lax.dot_general(kmat, q4, _NT, preferred_element_type=F32) * scale + bias
            sink = jnp.concatenate([jnp.broadcast_to(sink_ref[:, h:h + 1], (1, tq)) for h in heads], axis=1)
            m = jnp.maximum(jnp.max(s, axis=0, keepdims=True), sink)
            p = jnp.exp(s - m)
            den = jnp.sum(p, axis=0, keepdims=True) + jnp.exp(sink - m)
            ot = jnp.dot(vt, (p * (1.0 / den)).astype(BF16), preferred_element_type=F32)
            for g in range(0, qpk, 2):
                two = jnp.concatenate([ot[:, g * tq:(g + 1) * tq], ot[:, (g + 1) * tq:(g + 2) * tq]], axis=0)
                o_ref[:, (kh * qpk + g) * hd:(kh * qpk + g + 2) * hd] = two.T


def _attn_rows_kernel(q_ref, sink_ref, k_ref, v_ref, o_ref):
    tq = q_ref.shape[0]
    hd = ATTN_HEAD_DIM
    qpk = q_ref.shape[1] // hd // KV_HEADS
    scale = hd ** -0.5
    outs = []
    for kh in range(KV_HEADS):
        kmat = k_ref[:, kh * hd:(kh + 1) * hd].astype(BF16)
        vmat = v_ref[:, kh * hd:(kh + 1) * hd].astype(BF16)
        heads = range(kh * qpk, (kh + 1) * qpk)
        q4 = jnp.concatenate([q_ref[:, h * hd:(h + 1) * hd] for h in heads], axis=0).astype(BF16)
        s = lax.dot_general(q4, kmat, _NT, preferred_element_type=F32) * scale
        sink = jnp.concatenate([jnp.broadcast_to(sink_ref[:, h:h + 1], (tq, 1)) for h in heads], axis=0)
        m = jnp.maximum(jnp.max(s, axis=-1, keepdims=True), sink)
        p = jnp.exp(s - m)
        den = jnp.sum(p, axis=-1, keepdims=True) + jnp.exp(sink - m)
        o4 = jnp.dot((p / den).astype(BF16), vmat, preferred_element_type=F32)
        outs += [o4[g * tq:(g + 1) * tq] for g in range(qpk)]
    o_ref[...] = jnp.concatenate(outs, axis=1)


def swa_attention(q, sinks, k_src, v_src, k_col, v_col, bsz, seq, kv_rows):
    t, hqd = q.shape
    kvd = KV_HEADS * ATTN_HEAD_DIM
    n_heads = sinks.shape[0]
    if kv_rows is None:
        body = _attn_banded_kernel
        tq = _tile(seq, (256, WINDOW))
        nq = seq // tq
        per = tq // WINDOW
        kv_specs = lambda col: [
            pl.BlockSpec((WINDOW, kvd), lambda bi, ci: ((bi * nq + ci) * per - jnp.minimum(ci, 1), col)),
            pl.BlockSpec((tq, kvd), lambda bi, ci: (bi * nq + ci, col))]
    else:
        body = _attn_rows_kernel
        tq, nq = seq, 1
        kv_specs = lambda col: [pl.BlockSpec((kv_rows, kvd), lambda bi, ci: (bi, col))]
    k_specs, v_specs = kv_specs(k_col), kv_specs(v_col)
    in_specs = ([pl.BlockSpec((tq, hqd), lambda bi, ci: (bi * nq + ci, 0)),
                 pl.BlockSpec((1, LANES), lambda bi, ci: (0, 0))] + k_specs + v_specs)
    return pl.pallas_call(
        body,
        grid=(bsz, nq),
        in_specs=in_specs,
        out_specs=pl.BlockSpec((tq, hqd), lambda bi, ci: (bi * nq + ci, 0)),
        out_shape=jax.ShapeDtypeStruct((t, hqd), F32),
        compiler_params=_cparams(2, 40),
        name="swa_attention",
    )(q, jnp.pad(sinks, (0, LANES - n_heads)).reshape(1, LANES),
      *([k_src] * len(k_specs)), *([v_src] * len(v_specs)))


_PAIRS = [(a, b) for a in range(PEER_TOPK) for b in range(PEER_TOPK) if (a + 1) * (b + 1) <= PEER_TOPK]
_CAND_ROWS = -(-len(_PAIRS) // SUBLANES) * SUBLANES


def _extract_topk(s, n_rounds, break_ties):
    n = s.shape[0]
    iota = lax.broadcasted_iota(jnp.int32, s.shape, 0)
    rank = jnp.full(s.shape, n_rounds, jnp.int32)
    vals = []
    for r in range(n_rounds):
        m = jnp.max(s, axis=0, keepdims=True)
        sel = s == m
        if break_ties:
            sel = iota == jnp.min(jnp.where(sel, iota, n), axis=0, keepdims=True)
        rank = jnp.where(sel, r, rank)
        s = jnp.where(sel, -jnp.inf, s)
        vals.append(m)
    count = jnp.sum(jnp.where(rank < n_rounds, 1.0, 0.0), axis=0, keepdims=True)
    return rank, vals, count


def _peer_topk_kernel(q_ref, k1_ref, k2_ref, e1_ref, c1_ref, e2_ref, r2_ref,
                      s_ref, v_ref, cand_ref, sel_ref, tie_ref):
    half = k1_ref.shape[1]
    nh, tm = q_ref.shape[0], q_ref.shape[1]

    for h in range(nh):
        qh = q_ref[h].astype(BF16)
        s_ref[2 * h] = lax.dot_general(k1_ref[...], qh[:, :half], _NT, preferred_element_type=F32)
        s_ref[2 * h + 1] = lax.dot_general(k2_ref[...], qh[:, half:], _NT, preferred_element_type=F32)

    def select(h, break_ties):
        s1_ref, s2_ref = s_ref.at[2 * h], s_ref.at[2 * h + 1]
        for lc in range(tm // LANES):
            lanes = slice(lc * LANES, (lc + 1) * LANES)
            rank1, v1, n1 = _extract_topk(s1_ref[:, lanes], PEER_TOPK, break_ties)
            rank2, v2, n2 = _extract_topk(s2_ref[:, lanes], PEER_TOPK, break_ties)
            c1_ref[h, :, lanes] = rank1.astype(F32)
            r2_ref[h, :, lanes] = rank2.astype(F32).astype(BF16)
            for r in range(PEER_TOPK):
                v_ref[0, r:r + 1, lanes] = v1[r]
                v_ref[1, r:r + 1, lanes] = v2[r]
            sel_ref[0:1, lanes] = jnp.where((n1 != PEER_TOPK) | (n2 != PEER_TOPK), 1.0, 0.0)
        tied = sel_ref[0:1, :]
        cand_ref[...] = jnp.full((_CAND_ROWS, tm), -jnp.inf, F32)
        for i, (a, b) in enumerate(_PAIRS):
            cand_ref[i:i + 1, :] = v_ref[0, a:a + 1, :] + v_ref[1, b:b + 1, :]
        cand = cand_ref[...]
        crank, _, nc = _extract_topk(cand, PEER_TOPK, break_ties)
        tied = jnp.maximum(tied, jnp.where(nc != PEER_TOPK, 1.0, 0.0))
        chosen = crank < PEER_TOPK
        z = 2.0 * jnp.sum(jnp.where(chosen, jnp.exp(cand - cand[0:1]), 0.0), axis=0, keepdims=True)
        sel_ref[...] = jnp.where(chosen, 1.0, 0.0)
        cnts = []
        row = 0
        for a in range(PEER_TOPK):
            nb = sum(1 for (pa, _) in _PAIRS if pa == a)
            cnts.append(jnp.sum(sel_ref[row:row + nb, :], axis=0, keepdims=True))
            row += nb
        for lc in range(tm // LANES):
            lanes = slice(lc * LANES, (lc + 1) * LANES)
            rank1 = c1_ref[h, :, lanes]
            c1 = jnp.zeros(rank1.shape, F32)
            for a in range(PEER_TOPK):
                c1 = jnp.where(rank1 == a, cnts[a][:, lanes], c1)
            c1_ref[h, :, lanes] = c1
            e1_ref[h, :, lanes] = jnp.exp(s1_ref[:, lanes] - v_ref[0, 0:1, lanes])
            e2_ref[h, :, lanes] = (jnp.exp(s2_ref[:, lanes] - v_ref[1, 0:1, lanes])
                                   / z[:, lanes]).astype(BF16)
        return tied

    def fast(h, carry):
        tie_ref[0:1, :] = jnp.maximum(tie_ref[0:1, :], select(h, break_ties=False))
        return carry

    def exact(h, carry):
        select(h, break_ties=True)
        return carry

    tie_ref[...] = jnp.zeros(tie_ref.shape, F32)
    lax.fori_loop(0, nh, fast, 0)

    @pl.when(jnp.max(tie_ref[0:1, :]) > 0.0)
    def _():
        lax.fori_loop(0, nh, exact, 0)


def peer_topk(qh, k1, k2):
    nh, t, qd = qh.shape
    nk = k1.shape[0]
    tm = _tile(t, (2 * LANES, LANES))
    out32 = jax.ShapeDtypeStruct((nh, nk, t), F32)
    out16 = jax.ShapeDtypeStruct((nh, nk, t), BF16)
    ospec = pl.BlockSpec((nh, nk, tm), lambda i: (0, 0, i))
    return pl.pallas_call(
        _peer_topk_kernel,
        grid=(t // tm,),
        in_specs=[pl.BlockSpec((nh, tm, qd), lambda i: (0, i, 0)),
                  pl.BlockSpec((nk, qd // 2), lambda i: (0, 0)),
                  pl.BlockSpec((nk, qd // 2), lambda i: (0, 0))],
        out_specs=[ospec] * 4,
        out_shape=[out32, out32, out16, out16],
        scratch_shapes=[pltpu.VMEM((2 * nh, nk, tm), F32), pltpu.VMEM((2, PEER_TOPK, tm), F32),
                        pltpu.VMEM((_CAND_ROWS, tm), F32), pltpu.VMEM((_CAND_ROWS, tm), F32),
                        pltpu.VMEM((SUBLANES, tm), F32)],
        compiler_params=_cparams(1, 32),
        name="peer_topk",
    )(qh, k1, k2)


def _peer_dense_kernel(h_ref, g_ref, u_ref, vt_ref, e1_ref, c1_ref, e2_ref, r2_ref, o_ref,
                       xn_ref, acc_ref, gate_ref):
    ki = pl.program_id(1)
    nh, nkeys, tm = e2_ref.shape
    te = u_ref.shape[0]

    @pl.when(ki == 0)
    def _():
        xn_ref[...] = _rms(h_ref[...], g_ref[...]).astype(BF16)
        acc_ref[...] = jnp.zeros(acc_ref.shape, F32)

    zero = jnp.zeros((nkeys, LANES), BF16)
    for r in range(te // nkeys):
        rows = slice(r * nkeys, (r + 1) * nkeys)
        for lc in range(tm // LANES):
            lanes = slice(lc * LANES, (lc + 1) * LANES)
            w = zero
            for h in range(nh):
                c1b = jnp.broadcast_to(c1_ref[h, r:r + 1, lanes].astype(BF16), (nkeys, LANES))
                e1b = jnp.broadcast_to(e1_ref[h, r:r + 1, lanes].astype(BF16), (nkeys, LANES))
                w = w + jnp.where(r2_ref[h, :, lanes] < c1b, e2_ref[h, :, lanes], zero) * e1b
            gate_ref[rows, lanes] = w
    x = lax.dot_general(u_ref[...], xn_ref[...], _NT, preferred_element_type=F32).astype(BF16)
    act2 = x + x * jnp.tanh(x * (_GELU_C1 + _GELU_C3 * (x * x)))
    wact = gate_ref[...] * act2
    acc_ref[...] += jnp.dot(vt_ref[...], wact, preferred_element_type=F32)

    @pl.when(ki == pl.num_programs(1) - 1)
    def _():
        o_ref[...] = h_ref[...] + acc_ref[...].T


def peer_dense(h, g, u, vt, layer, e1, c1, e2, r2):
    t, d = h.shape
    ne = u.shape[1]
    nh, nk, _ = e1.shape
    tm = _tile(t, (512, 256))
    te = 1024
    key1_spec = pl.BlockSpec((nh, te // nk, tm), lambda i, k: (0, k, i))
    key2_spec = pl.BlockSpec((nh, nk, tm), lambda i, k: (0, 0, i))
    return pl.pallas_call(
        _peer_dense_kernel,
        grid=(t // tm, ne // te),
        in_specs=[pl.BlockSpec((tm, d), lambda i, k: (i, 0)),
                  pl.BlockSpec((1, d), lambda i, k: (0, 0)),
                  pl.BlockSpec((None, te, d), lambda i, k: (layer, k, 0)),
                  pl.BlockSpec((None, d, te), lambda i, k: (layer, 0, k)),
                  key1_spec, key1_spec, key2_spec, key2_spec],
        out_specs=pl.BlockSpec((tm, d), lambda i, k: (i, 0)),
        out_shape=jax.ShapeDtypeStruct((t, d), F32),
        scratch_shapes=[pltpu.VMEM((tm, d), BF16), pltpu.VMEM((d, tm), F32),
                        pltpu.VMEM((te, tm), BF16)],
        compiler_params=_cparams(2, 48),
        name="peer_dense",
    )(h, g.reshape(1, d), u, vt, e1, c1, e2, r2)


def kernel(x_prompt, x_sample, state_ssm, state_conv, cache_k_win, cache_v_win, norm_mix, norm_ffn, norm_kv, norm_final, m_w_in, m_conv_w, m_conv_b, m_dt_bias, m_a_log, m_d_skip, m_norm, m_w_out, a_w_kv, a_b_kv, a_w_q, a_b_q, a_sinks, a_w_o, a_b_o, p_w_q, p_sub_k1, p_sub_k2, p_u, p_v):
    depth = norm_mix.shape[0]
    n_a = m_w_in.shape[0]
    d = x_prompt.shape[-1]
    inner = m_w_out.shape[1]
    conv_dim = m_conv_w.shape[2]
    heads = m_d_skip.shape[1]
    kvd = a_w_kv.shape[1] // 2
    qdim = p_w_q.shape[2] // PEER_HEADS
    dt_col = inner + conv_dim
    in_pad = dt_col + LANES

    w_in = jnp.pad(m_w_in, ((0, 0), (0, 0), (0, in_pad - m_w_in.shape[2]))).astype(BF16)
    w_out = m_w_out.astype(BF16)
    w_kv = a_w_kv.astype(BF16)
    w_q = a_w_q.astype(BF16)
    w_o = a_w_o.astype(BF16)
    pw_q = p_w_q.astype(BF16)
    k1 = p_sub_k1.astype(BF16)
    k2 = p_sub_k2.astype(BF16)
    u_tab = p_u.astype(BF16)
    vt_tab = jnp.swapaxes(p_v.astype(BF16), 1, 2)
    zeros_in = jnp.zeros((in_pad,), F32)
    zeros_pq = jnp.zeros((p_w_q.shape[2],), F32)
    in_tn = _tile(in_pad, (896, 128))

    def run(x, ssm_init, conv_init, k_prev, v_prev):
        bsz, seq, _ = x.shape
        h = x.reshape(bsz * seq, d)
        ssm_out, conv_out = [], []
        k_src = v_src = None
        for l in range(depth):
            if l < n_a:
                proj = norm_matmul(h, norm_mix[l], w_in[l], zeros_in, in_tn)
                if conv_init is None:
                    prev8 = jnp.zeros((bsz, SUBLANES, conv_dim), F32)
                else:
                    prev8 = jnp.pad(conv_init[l], ((0, 0), (SUBLANES - (CONV_WIDTH - 1), 0), (0, 0)))
                xbc = causal_conv_silu(proj, prev8, m_conv_w[l], m_conv_b[l], bsz, seq, inner)
                s0 = None if ssm_init is None else ssm_init[l].reshape(bsz, inner, SSM_STATE)
                y, s_new = ssd_scan(xbc, proj, dt_col, m_dt_bias[l], m_a_log[l], m_d_skip[l], s0, bsz, seq)
                ssm_out.append(s_new.reshape(bsz, heads, SSM_HEAD_DIM, SSM_STATE))
                tail = proj.reshape(bsz, seq, in_pad)[:, seq - (CONV_WIDTH - 1):, inner:dt_col]
                conv_out.append(tail)
                h = gated_norm_out(y, proj, m_norm[l], w_out[l], h)
            else:
                j = l - n_a
                if j == 0:
                    kv = norm_matmul(h, norm_kv, w_kv, a_b_kv, 2 * kvd)
                    kv3 = kv.reshape(bsz, seq, 2 * kvd)
                    if k_prev is not None:
                        k_all = jnp.concatenate([k_prev.reshape(bsz, WINDOW, kvd), kv3[:, :, :kvd]], axis=1)
                        v_all = jnp.concatenate([v_prev.reshape(bsz, WINDOW, kvd), kv3[:, :, kvd:]], axis=1)
                        k_src = k_all.reshape(bsz * (WINDOW + seq), kvd)
                        v_src = v_all.reshape(bsz * (WINDOW + seq), kvd)
                        k_win, v_win = k_all[:, -WINDOW:], v_all[:, -WINDOW:]
                    else:
                        k_win, v_win = kv3[:, -WINDOW:, :kvd], kv3[:, -WINDOW:, kvd:]
                qp = norm_matmul(h, norm_mix[l], w_q[j], a_b_q[j], w_q.shape[2])
                if k_prev is None:
                    o = swa_attention(qp, a_sinks[j], kv, kv, 0, 1, bsz, seq, None)
                else:
                    o = swa_attention(qp, a_sinks[j], k_src, v_src, 0, 0, bsz, seq, WINDOW + seq)
                h = matmul_bias_res(o, w_o[j], a_b_o[j], h)
            qh = norm_matmul(h, norm_ffn[l], pw_q[l], zeros_pq, pw_q.shape[2] // 2, head_dim=qdim)
            e1, c1, e2, r2 = peer_topk(qh, k1[l], k2[l])
            h = peer_dense(h, norm_ffn[l], u_tab, vt_tab, l, e1, c1, e2, r2)
        y = rmsnorm_rows(h, norm_final).reshape(bsz, seq, d)
        kshape = (bsz, WINDOW, KV_HEADS, ATTN_HEAD_DIM)
        return y, jnp.stack(ssm_out), jnp.stack(conv_out), k_win.reshape(kshape), v_win.reshape(kshape)

    y_p, p_ssm, p_conv, p_k, p_v_win = run(x_prompt, None, None, None, None)
    y_s, s_ssm, s_conv, s_k, s_v = run(x_sample, state_ssm, state_conv, cache_k_win, cache_v_win)
    return (y_p, y_s, p_ssm, p_conv, p_k, p_v_win, s_ssm, s_conv, s_k, s_v)
```

```python
import functools

import jax
import jax.numpy as jnp
from jax import lax
from jax.experimental import pallas as pl
from jax.experimental.pallas import tpu as pltpu

F32 = jnp.float32
BF16 = jnp.bfloat16
EPS = 1e-6

WINDOW = 128
SSM_HEAD_DIM = 64
SSM_STATE = 128
SSM_GROUPS = 8
CONV_WIDTH = 4
SSD_BLOCK = 128
ATTN_CHUNK = 64
ATTN_HEAD_DIM = 64
KV_HEADS = 4
PEER_HEADS = 8
PEER_NKEYS = 128
PEER_TOPK = 16
LANES = 128
SUBLANES = 8
MIB = 1024 * 1024

_NT = (((1,), (1,)), ((), ()))
_GELU_C1 = 0.7978845608028654
_GELU_C3 = _GELU_C1 * 0.044715


def _cparams(n_axes, vmem_mib, flags=None):
    return pltpu.CompilerParams(
        dimension_semantics=("arbitrary",) * n_axes, vmem_limit_bytes=vmem_mib * MIB, flags=flags)


def _tile(n, prefs):
    for p in prefs:
        if n % p == 0:
            return p
    return n


def _rms(x, g):
    ms = jnp.mean(x * x, axis=-1, keepdims=True)
    return x * lax.rsqrt(ms + EPS) * g


def _norm_matmul_kernel(x_ref, g_ref, w_ref, b_ref, o_ref, xn_ref, *, head_dim):
    @pl.when(pl.program_id(1) == 0)
    def _():
        xn_ref[...] = _rms(x_ref[...], g_ref[...]).astype(BF16)

    res = jnp.dot(xn_ref[...], w_ref[...], preferred_element_type=F32) + b_ref[...]
    if head_dim is None:
        o_ref[...] = res
    else:
        for k in range(o_ref.shape[0]):
            o_ref[k] = res[:, k * head_dim:(k + 1) * head_dim]


def norm_matmul(x, g, w, b, tn, head_dim=None):
    t, d = x.shape
    n = w.shape[1]
    tm = _tile(t, (1024, 512, 256))
    if head_dim is not None:
        out_shape = jax.ShapeDtypeStruct((n // head_dim, t, head_dim), F32)
        out_spec = pl.BlockSpec((tn // head_dim, tm, head_dim), lambda i, j: (j, i, 0))
    else:
        out_shape = jax.ShapeDtypeStruct((t, n), F32)
        out_spec = pl.BlockSpec((tm, tn), lambda i, j: (i, j))
    return pl.pallas_call(
        functools.partial(_norm_matmul_kernel, head_dim=head_dim),
        grid=(t // tm, n // tn),
        in_specs=[pl.BlockSpec((tm, d), lambda i, j: (i, 0)),
                  pl.BlockSpec((1, d), lambda i, j: (0, 0)),
                  pl.BlockSpec((d, tn), lambda i, j: (0, j)),
                  pl.BlockSpec((1, tn), lambda i, j: (0, j))],
        out_specs=out_spec,
        out_shape=out_shape,
        scratch_shapes=[pltpu.VMEM((tm, d), BF16)],
        compiler_params=_cparams(2, 40),
        name="norm_matmul",
    )(x, g.reshape(1, d), w, b.reshape(1, n))


def _matmul_res_kernel(x_ref, w_ref, b_ref, r_ref, o_ref):
    mix = jnp.dot(x_ref[...].astype(BF16), w_ref[...], preferred_element_type=F32) + b_ref[...]
    o_ref[...] = r_ref[...] + mix


def matmul_bias_res(x, w, b, res):
    t, k = x.shape
    n = w.shape[1]
    tm = _tile(t, (512, 256))
    return pl.pallas_call(
        _matmul_res_kernel,
        grid=(t // tm,),
        in_specs=[pl.BlockSpec((tm, k), lambda i: (i, 0)),
                  pl.BlockSpec((k, n), lambda i: (0, 0)),
                  pl.BlockSpec((1, n), lambda i: (0, 0)),
                  pl.BlockSpec((tm, n), lambda i: (i, 0))],
        out_specs=pl.BlockSpec((tm, n), lambda i: (i, 0)),
        out_shape=jax.ShapeDtypeStruct((t, n), F32),
        compiler_params=_cparams(1, 32),
        name="matmul_bias_res",
    )(x, w, b.reshape(1, n), res)


def _rmsnorm_kernel(x_ref, g_ref, o_ref):
    o_ref[...] = _rms(x_ref[...], g_ref[...])


def rmsnorm_rows(x, g):
    t, d = x.shape
    tm = _tile(t, (512, 256))
    return pl.pallas_call(
        _rmsnorm_kernel,
        grid=(t // tm,),
        in_specs=[pl.BlockSpec((tm, d), lambda i: (i, 0)), pl.BlockSpec((1, d), lambda i: (0, 0))],
        out_specs=pl.BlockSpec((tm, d), lambda i: (i, 0)),
        out_shape=jax.ShapeDtypeStruct((t, d), F32),
        compiler_params=_cparams(1, 32),
        name="rmsnorm",
    )(x, g.reshape(1, d))


def _conv_kernel(x_ref, halo_ref, prev_ref, w_ref, b_ref, o_ref):
    first = pl.program_id(1) == 0
    x = x_ref[...]
    ext = jnp.concatenate([jnp.where(first, prev_ref[...], halo_ref[...]), x], axis=0)
    acc = b_ref[...]
    for k in range(CONV_WIDTH - 1):
        tap = pltpu.roll(ext, CONV_WIDTH - 1 - k, 0)[SUBLANES:]
        acc = acc + tap * w_ref[k:k + 1, :]
    acc = acc + x * w_ref[CONV_WIDTH - 1:CONV_WIDTH, :]
    o_ref[...] = acc * jax.nn.sigmoid(acc)


def causal_conv_silu(proj, prev8, w, b, bsz, seq, col0):
    c = w.shape[1]
    tl = _tile(seq, (256, 128, 64, 32, 16))
    tc = 1024
    nl = seq // tl
    cb0 = col0 // tc
    hpb = tl // SUBLANES
    return pl.pallas_call(
        _conv_kernel,
        grid=(bsz, nl, c // tc),
        in_specs=[pl.BlockSpec((tl, tc), lambda bi, li, ci: (bi * nl + li, cb0 + ci)),
                  pl.BlockSpec((SUBLANES, tc),
                               lambda bi, li, ci: (jnp.maximum((bi * nl + li) * hpb - 1, 0), cb0 + ci)),
                  pl.BlockSpec((None, SUBLANES, tc), lambda bi, li, ci: (bi, 0, ci)),
                  pl.BlockSpec((CONV_WIDTH, tc), lambda bi, li, ci: (0, ci)),
                  pl.BlockSpec((1, tc), lambda bi, li, ci: (0, ci))],
        out_specs=pl.BlockSpec((tl, tc), lambda bi, li, ci: (bi * nl + li, ci)),
        out_shape=jax.ShapeDtypeStruct((bsz * seq, c), F32),
        compiler_params=_cparams(3, 32),
        name="conv_silu",
    )(proj, proj, prev8, w, b.reshape(1, c))


def _split3(x):
    hi = x.astype(BF16)
    r1 = x - hi.astype(F32)
    mid = r1.astype(BF16)
    lo = (r1 - mid.astype(F32)).astype(BF16)
    return hi, mid, lo


def _ssd_kernel(*refs, rows, has_init):
    if has_init:
        (x_ref, b_ref, c_ref, dt_ref, bias_ref, alog_ref, dskip_ref, echan_ref, ecol_ref, s0_ref,
         y_ref, sout_ref, st_ref) = refs
    else:
        (x_ref, b_ref, c_ref, dt_ref, bias_ref, alog_ref, dskip_ref, echan_ref, ecol_ref,
         y_ref, sout_ref, st_ref) = refs
    q = SSD_BLOCK
    gw = x_ref.shape[1] // SSM_GROUPS
    hpg = gw // SSM_HEAD_DIM
    ci = pl.program_id(1)

    @pl.when(ci == 0)
    def _():
        for g in range(SSM_GROUPS):
            if has_init:
                st_ref[g] = s0_ref[g * gw:(g + 1) * gw, :].T
            else:
                st_ref[g] = jnp.zeros((SSM_STATE, gw), F32)

    def pad(a):
        if rows == q:
            return a
        return jnp.concatenate([a, jnp.zeros((q - rows, a.shape[1]), a.dtype)], axis=0)

    raw = dt_ref[...] + bias_ref[...]
    dt = pad(jnp.maximum(raw, 0.0) + jnp.log1p(jnp.exp(-jnp.abs(raw))))
    a_neg = -jnp.exp(alog_ref[...])
    dta = dt * a_neg
    ri = lax.broadcasted_iota(jnp.int32, (q, q), 0)
    cj = lax.broadcasted_iota(jnp.int32, (q, q), 1)
    tri = ri >= cj
    trib = jnp.where(tri, 1.0, 0.0).astype(BF16)
    acum = sum(jnp.dot(trib, part, preferred_element_type=F32) for part in _split3(dta))
    acum_t = acum.T
    a_last = acum[q - 1:q, :]
    dec_end = jnp.exp(a_last - acum)
    ea = jnp.exp(acum)
    wgt = dt * dec_end

    two = lambda a: jnp.concatenate(_split3(a)[:2], axis=1)
    chan = jnp.dot(jnp.concatenate([two(dt), two(wgt), two(ea)], axis=0), echan_ref[...],
                   preferred_element_type=F32)
    dt_e, wgt_e, ea_e = chan[0:q], chan[q:2 * q], chan[2 * q:3 * q]
    acum_cols = jnp.dot(jnp.concatenate(_split3(acum), axis=1), ecol_ref[...],
                        preferred_element_type=F32)

    xs = pad(x_ref[...])
    bm = pad(b_ref[...])
    cm = pad(c_ref[...])

    for g in range(SSM_GROUPS):
        gsl = slice(g * gw, (g + 1) * gw)
        bg = bm[:, g * SSM_STATE:(g + 1) * SSM_STATE]
        cgb = cm[:, g * SSM_STATE:(g + 1) * SSM_STATE].astype(BF16)
        cb = lax.dot_general(cgb, bg.astype(BF16), _NT, preferred_element_type=F32)
        xg = xs[:, gsl]
        xdt = (xg * dt_e[:, gsl]).astype(BF16)
        xw = (xg * wgt_e[:, gsl]).astype(BF16)
        st_prev = st_ref[g]
        y_off = jnp.dot(cgb, st_prev.astype(BF16), preferred_element_type=F32) * ea_e[:, gsl]
        st_new = jnp.dot(bg.T.astype(BF16), xw, preferred_element_type=F32)
        st_ref[g] = ea_e[q - 1:q, gsl] * st_prev + st_new
        yd = []
        for r in range(hpg):
            h = g * hpg + r
            seg = acum_cols[:, h * q:(h + 1) * q] - acum_t[h:h + 1, :]
            lmat = jnp.where(tri, jnp.exp(seg), 0.0)
            m = (cb * lmat).astype(BF16)
            yd.append(jnp.dot(m, xdt[:, r * SSM_HEAD_DIM:(r + 1) * SSM_HEAD_DIM],
                              preferred_element_type=F32))
        yg = jnp.concatenate(yd, axis=1) + y_off + xg * dskip_ref[:, g * gw:(g + 1) * gw]
        y_ref[:, g * gw:(g + 1) * gw] = yg[:rows]

    @pl.when(ci == pl.num_programs(1) - 1)
    def _():
        for g in range(SSM_GROUPS):
            sout_ref[g * gw:(g + 1) * gw, :] = st_ref[g].T


def ssd_scan(xbc, proj, dt_col, dt_bias, a_log, d_skip, s0, bsz, seq):
    inner = d_skip.shape[0] * SSM_HEAD_DIM
    gn = SSM_GROUPS * SSM_STATE
    rows = min(SSD_BLOCK, seq)
    nc = seq // rows
    heads = d_skip.shape[0]
    pad128 = lambda v: jnp.pad(v, (0, LANES - heads)).reshape(1, LANES)
    row_map = lambda bi, ci: bi * nc + ci
    in_specs = [pl.BlockSpec((rows, inner), lambda bi, ci: (row_map(bi, ci), 0)),
                pl.BlockSpec((rows, gn), lambda bi, ci: (row_map(bi, ci), inner // gn)),
                pl.BlockSpec((rows, gn), lambda bi, ci: (row_map(bi, ci), inner // gn + 1)),
                pl.BlockSpec((rows, LANES), lambda bi, ci: (row_map(bi, ci), dt_col // LANES)),
                pl.BlockSpec((1, LANES), lambda bi, ci: (0, 0)),
                pl.BlockSpec((1, LANES), lambda bi, ci: (0, 0)),
                pl.BlockSpec((1, inner), lambda bi, ci: (0, 0)),
                pl.BlockSpec((2 * LANES, inner), lambda bi, ci: (0, 0)),
                pl.BlockSpec((3 * LANES, heads * SSD_BLOCK), lambda bi, ci: (0, 0))]
    head_of = lambda width: (jnp.arange(heads * width) // width)[None, :] == jnp.arange(LANES)[:, None]
    echan = jnp.tile(head_of(SSM_HEAD_DIM).astype(BF16), (2, 1))
    ecol = jnp.tile(head_of(SSD_BLOCK).astype(BF16), (3, 1))
    args = [xbc, xbc, xbc, proj, pad128(dt_bias), pad128(a_log),
            jnp.repeat(d_skip, SSM_HEAD_DIM).reshape(1, inner), echan, ecol]
    if s0 is not None:
        in_specs.append(pl.BlockSpec((None, inner, SSM_STATE), lambda bi, ci: (bi, 0, 0)))
        args.append(s0)
    return pl.pallas_call(
        functools.partial(_ssd_kernel, rows=rows, has_init=s0 is not None),
        grid=(bsz, nc),
        in_specs=in_specs,
        out_specs=[pl.BlockSpec((rows, inner), lambda bi, ci: (row_map(bi, ci), 0)),
                   pl.BlockSpec((None, inner, SSM_STATE), lambda bi, ci: (bi, 0, 0))],
        out_shape=[jax.ShapeDtypeStruct((bsz * seq, inner), F32),
                   jax.ShapeDtypeStruct((bsz, inner, SSM_STATE), F32)],
        scratch_shapes=[pltpu.VMEM((SSM_GROUPS, SSM_STATE, inner // SSM_GROUPS), F32)],
        compiler_params=_cparams(2, 48),
        name="ssd_scan",
    )(*args)


def _gated_out_kernel(y_ref, z_ref, g_ref, w_ref, r_ref, o_ref):
    z = z_ref[...]
    y = y_ref[...] * (z * jax.nn.sigmoid(z))
    yn = _rms(y, g_ref[...]).astype(BF16)
    o_ref[...] = r_ref[...] + jnp.dot(yn, w_ref[...], preferred_element_type=F32)


def gated_norm_out(y, proj, g, w, res):
    t, inner = y.shape
    n = w.shape[1]
    tm = _tile(t, (256, 128))
    return pl.pallas_call(
        _gated_out_kernel,
        grid=(t // tm,),
        in_specs=[pl.BlockSpec((tm, inner), lambda i: (i, 0)),
                  pl.BlockSpec((tm, inner), lambda i: (i, 0)),
                  pl.BlockSpec((1, inner), lambda i: (0, 0)),
                  pl.BlockSpec((inner, n), lambda i: (0, 0)),
                  pl.BlockSpec((tm, n), lambda i: (i, 0))],
        out_specs=pl.BlockSpec((tm, n), lambda i: (i, 0)),
        out_shape=jax.ShapeDtypeStruct((t, n), F32),
        compiler_params=_cparams(1, 40),
        name="gated_norm_out",
    )(y, proj, g.reshape(1, inner), w, res)


def _attn_banded_kernel(q_ref, sink_ref, kp_ref, kc_ref, vp_ref, vc_ref, o_ref):
    tq = q_ref.shape[0]
    hd = ATTN_HEAD_DIM
    qpk = q_ref.shape[1] // hd // KV_HEADS
    scale = hd ** -0.5
    nk = WINDOW + tq
    back = WINDOW // ATTN_CHUNK
    kc = lax.broadcasted_iota(jnp.int32, (nk, tq), 0) // ATTN_CHUNK
    qc = lax.broadcasted_iota(jnp.int32, (nk, tq), 1) // ATTN_CHUNK
    first_kc = jnp.where(pl.program_id(1) == 0, back, 0)
    valid = (kc >= qc) & (kc <= qc + back) & (kc >= first_kc)
    bias = jnp.concatenate([jnp.where(valid, 0.0, -jnp.inf)] * qpk, axis=1)
    for pair in range(KV_HEADS // 2):
        cols = slice(pair * 2 * hd, (pair + 1) * 2 * hd)
        k2 = jnp.concatenate([kp_ref[:, cols], kc_ref[:, cols]], axis=0)
        v2t = jnp.concatenate([vp_ref[:, cols], vc_ref[:, cols]], axis=0).T
        for sub in range(2):
            kh = pair * 2 + sub
            heads = range(kh * qpk, (kh + 1) * qpk)
            kmat = k2[:, sub * hd:(sub + 1) * hd].astype(BF16)
            vt = v2t[sub * hd:(sub + 1) * hd, :].astype(BF16)
            q4 = jnp.concatenate([q_ref[:, h * hd:(h + 1) * hd] for h in heads], axis=0).astype(BF16)
            s = lax.dot_general(kmat, q4, _NT, preferred_element_type=F32) * scale + bias
            sink = jnp.concatenate([jnp.broadcast_to(sink_ref[:, h:h + 1], (1, tq)) for h in heads], axis=1)
            m = jnp.maximum(jnp.max(s, axis=0, keepdims=True), sink)
            p = jnp.exp(s - m)
            den = jnp.sum(p, axis=0, keepdims=True) + jnp.exp(sink - m)
            ot = jnp.dot(vt, (p * (1.0 / den)).astype(BF16), preferred_element_type=F32)
            for g in range(0, qpk, 2):
                two = jnp.concatenate([ot[:, g * tq:(g + 1) * tq], ot[:, (g + 1) * tq:(g + 2) * tq]], axis=0)
                o_ref[:, (kh * qpk + g) * hd:(kh * qpk + g + 2) * hd] = two.T


def _attn_rows_kernel(q_ref, sink_ref, k_ref, v_ref, o_ref):
    tq = q_ref.shape[0]
    hd = ATTN_HEAD_DIM
    qpk = q_ref.shape[1] // hd // KV_HEADS
    scale = hd ** -0.5
    outs = []
    for kh in range(KV_HEADS):
        kmat = k_ref[:, kh * hd:(kh + 1) * hd].astype(BF16)
        vmat = v_ref[:, kh * hd:(kh + 1) * hd].astype(BF16)
        heads = range(kh * qpk, (kh + 1) * qpk)
        q4 = jnp.concatenate([q_ref[:, h * hd:(h + 1) * hd] for h in heads], axis=0).astype(BF16)
        s = lax.dot_general(q4, kmat, _NT, preferred_element_type=F32) * scale
        sink = jnp.concatenate([jnp.broadcast_to(sink_ref[:, h:h + 1], (tq, 1)) for h in heads], axis=0)
        m = jnp.maximum(jnp.max(s, axis=-1, keepdims=True), sink)
        p = jnp.exp(s - m)
        den = jnp.sum(p, axis=-1, keepdims=True) + jnp.exp(sink - m)
        o4 = jnp.dot((p / den).astype(BF16), vmat, preferred_element_type=F32)
        outs += [o4[g * tq:(g + 1) * tq] for g in range(qpk)]
    o_ref[...] = jnp.concatenate(outs, axis=1)


def swa_attention(q, sinks, k_src, v_src, k_col, v_col, bsz, seq, kv_rows):
    t, hqd = q.shape
    kvd = KV_HEADS * ATTN_HEAD_DIM
    n_heads = sinks.shape[0]
    if kv_rows is None:
        body = _attn_banded_kernel
        tq = _tile(seq, (256, WINDOW))
        nq = seq // tq
        per = tq // WINDOW
        kv_specs = lambda col: [
            pl.BlockSpec((WINDOW, kvd), lambda bi, ci: ((bi * nq + ci) * per - jnp.minimum(ci, 1), col)),
            pl.BlockSpec((tq, kvd), lambda bi, ci: (bi * nq + ci, col))]
    else:
        body = _attn_rows_kernel
        tq, nq = seq, 1
        kv_specs = lambda col: [pl.BlockSpec((kv_rows, kvd), lambda bi, ci: (bi, col))]
    k_specs, v_specs = kv_specs(k_col), kv_specs(v_col)
    in_specs = ([pl.BlockSpec((tq, hqd), lambda bi, ci: (bi * nq + ci, 0)),
                 pl.BlockSpec((1, LANES), lambda bi, ci: (0, 0))] + k_specs + v_specs)
    return pl.pallas_call(
        body,
        grid=(bsz, nq),
        in_specs=in_specs,
        out_specs=pl.BlockSpec((tq, hqd), lambda bi, ci: (bi * nq + ci, 0)),
        out_shape=jax.ShapeDtypeStruct((t, hqd), F32),
        compiler_params=_cparams(2, 40),
        name="swa_attention",
    )(q, jnp.pad(sinks, (0, LANES - n_heads)).reshape(1, LANES),
      *([k_src] * len(k_specs)), *([v_src] * len(v_specs)))


_PAIRS = [(a, b) for a in range(PEER_TOPK) for b in range(PEER_TOPK) if (a + 1) * (b + 1) <= PEER_TOPK]
_CAND_ROWS = -(-len(_PAIRS) // SUBLANES) * SUBLANES


def _extract_topk(s, n_rounds, break_ties):
    n = s.shape[0]
    iota = lax.broadcasted_iota(jnp.int32, s.shape, 0)
    rank = jnp.full(s.shape, n_rounds, jnp.int32)
    vals = []
    for r in range(n_rounds):
        m = jnp.max(s, axis=0, keepdims=True)
        sel = s == m
        if break_ties:
            sel = iota == jnp.min(jnp.where(sel, iota, n), axis=0, keepdims=True)
        rank = jnp.where(sel, r, rank)
        s = jnp.where(sel, -jnp.inf, s)
        vals.append(m)
    count = jnp.sum(jnp.where(rank < n_rounds, 1.0, 0.0), axis=0, keepdims=True)
    return rank, vals, count


def _peer_topk_kernel(q_ref, k1_ref, k2_ref, e1_ref, c1_ref, e2_ref, r2_ref,
                      s_ref, v_ref, cand_ref, sel_ref, tie_ref):
    half = k1_ref.shape[1]
    nh, tm = q_ref.shape[0], q_ref.shape[1]

    for h in range(nh):
        qh = q_ref[h].astype(BF16)
        s_ref[2 * h] = lax.dot_general(k1_ref[...], qh[:, :half], _NT, preferred_element_type=F32)
        s_ref[2 * h + 1] = lax.dot_general(k2_ref[...], qh[:, half:], _NT, preferred_element_type=F32)

    def select(h, break_ties):
        s1_ref, s2_ref = s_ref.at[2 * h], s_ref.at[2 * h + 1]
        for lc in range(tm // LANES):
            lanes = slice(lc * LANES, (lc + 1) * LANES)
            rank1, v1, n1 = _extract_topk(s1_ref[:, lanes], PEER_TOPK, break_ties)
            rank2, v2, n2 = _extract_topk(s2_ref[:, lanes], PEER_TOPK, break_ties)
            c1_ref[h, :, lanes] = rank1.astype(F32)
            r2_ref[h, :, lanes] = rank2.astype(F32).astype(BF16)
            for r in range(PEER_TOPK):
                v_ref[0, r:r + 1, lanes] = v1[r]
                v_ref[1, r:r + 1, lanes] = v2[r]
            sel_ref[0:1, lanes] = jnp.where((n1 != PEER_TOPK) | (n2 != PEER_TOPK), 1.0, 0.0)
        tied = sel_ref[0:1, :]
        cand_ref[...] = jnp.full((_CAND_ROWS, tm), -jnp.inf, F32)
        for i, (a, b) in enumerate(_PAIRS):
            cand_ref[i:i + 1, :] = v_ref[0, a:a + 1, :] + v_ref[1, b:b + 1, :]
        cand = cand_ref[...]
        crank, _, nc = _extract_topk(cand, PEER_TOPK, break_ties)
        tied = jnp.maximum(tied, jnp.where(nc != PEER_TOPK, 1.0, 0.0))
        chosen = crank < PEER_TOPK
        z = 2.0 * jnp.sum(jnp.where(chosen, jnp.exp(cand - cand[0:1]), 0.0), axis=0, keepdims=True)
        sel_ref[...] = jnp.where(chosen, 1.0, 0.0)
        cnts = []
        row = 0
        for a in range(PEER_TOPK):
            nb = sum(1 for (pa, _) in _PAIRS if pa == a)
            cnts.append(jnp.sum(sel_ref[row:row + nb, :], axis=0, keepdims=True))
            row += nb
        for lc in range(tm // LANES):
            lanes = slice(lc * LANES, (lc + 1) * LANES)
            rank1 = c1_ref[h, :, lanes]
            c1 = jnp.zeros(rank1.shape, F32)
            for a in range(PEER_TOPK):
                c1 = jnp.where(rank1 == a, cnts[a][:, lanes], c1)
            c1_ref[h, :, lanes] = c1
            e1_ref[h, :, lanes] = jnp.exp(s1_ref[:, lanes] - v_ref[0, 0:1, lanes])
            e2_ref[h, :, lanes] = (jnp.exp(s2_ref[:, lanes] - v_ref[1, 0:1, lanes])
                                   / z[:, lanes]).astype(BF16)
        return tied

    def fast(h, carry):
        tie_ref[pl.ds(h, 1), :] = select(h, break_ties=False)
        return carry

    def exact(h, carry):
        @pl.when(jnp.max(tie_ref[pl.ds(h, 1), :]) > 0.0)
        def _():
            select(h, break_ties=True)

        return carry

    lax.fori_loop(0, nh, fast, 0)

    @pl.when(jnp.max(tie_ref[...]) > 0.0)
    def _():
        lax.fori_loop(0, nh, exact, 0)


def peer_topk(qh, k1, k2):
    nh, t, qd = qh.shape
    nk = k1.shape[0]
    tm = _tile(t, (2 * LANES, LANES))
    out32 = jax.ShapeDtypeStruct((nh, nk, t), F32)
    out16 = jax.ShapeDtypeStruct((nh, nk, t), BF16)
    ospec = pl.BlockSpec((nh, nk, tm), lambda i: (0, 0, i))
    return pl.pallas_call(
        _peer_topk_kernel,
        grid=(t // tm,),
        in_specs=[pl.BlockSpec((nh, tm, qd), lambda i: (0, i, 0)),
                  pl.BlockSpec((nk, qd // 2), lambda i: (0, 0)),
                  pl.BlockSpec((nk, qd // 2), lambda i: (0, 0))],
        out_specs=[ospec] * 4,
        out_shape=[out32, out32, out16, out16],
        scratch_shapes=[pltpu.VMEM((2 * nh, nk, tm), F32), pltpu.VMEM((2, PEER_TOPK, tm), F32),
                        pltpu.VMEM((_CAND_ROWS, tm), F32), pltpu.VMEM((_CAND_ROWS, tm), F32),
                        pltpu.VMEM((nh, tm), F32)],
        compiler_params=_cparams(1, 32),
        name="peer_topk",
    )(qh, k1, k2)


def _peer_dense_kernel(h_ref, g_ref, u_ref, v_ref, e1_ref, c1_ref, e2_ref, r2_ref, o_ref,
                       xn_ref, acc_ref, gate_ref):
    ki = pl.program_id(1)
    nh, nkeys, tm = e2_ref.shape
    te = u_ref.shape[0]

    @pl.when(ki == 0)
    def _():
        xn_ref[...] = _rms(h_ref[...], g_ref[...]).astype(BF16)
        acc_ref[...] = jnp.zeros(acc_ref.shape, F32)

    zero = jnp.zeros((nkeys, LANES), BF16)
    for r in range(te // nkeys):
        rows = slice(r * nkeys, (r + 1) * nkeys)
        for lc in range(tm // LANES):
            lanes = slice(lc * LANES, (lc + 1) * LANES)
            w = zero
            for h in range(nh):
                c1b = jnp.broadcast_to(c1_ref[h, r:r + 1, lanes].astype(BF16), (nkeys, LANES))
                e1b = jnp.broadcast_to(e1_ref[h, r:r + 1, lanes].astype(BF16), (nkeys, LANES))
                w = w + jnp.where(r2_ref[h, :, lanes] < c1b, e2_ref[h, :, lanes], zero) * e1b
            gate_ref[rows, lanes] = w
    x = lax.dot_general(u_ref[...], xn_ref[...], _NT, preferred_element_type=F32).astype(BF16)
    act2 = x + x * jnp.tanh(x * (_GELU_C1 + _GELU_C3 * (x * x)))
    wact = gate_ref[...] * act2
    acc_ref[...] += lax.dot_general(wact, v_ref[...], (((0,), (0,)), ((), ())),
                                    preferred_element_type=F32)

    @pl.when(ki == pl.num_programs(1) - 1)
    def _():
        o_ref[...] = h_ref[...] + acc_ref[...]


def peer_dense(h, g, u, v, layer, e1, c1, e2, r2):
    t, d = h.shape
    ne = u.shape[1]
    nh, nk, _ = e1.shape
    tm = _tile(t, (512, 256))
    te = 1024
    key1_spec = pl.BlockSpec((nh, te // nk, tm), lambda i, k: (0, k, i))
    key2_spec = pl.BlockSpec((nh, nk, tm), lambda i, k: (0, 0, i))
    return pl.pallas_call(
        _peer_dense_kernel,
        grid=(t // tm, ne // te),
        in_specs=[pl.BlockSpec((tm, d), lambda i, k: (i, 0)),
                  pl.BlockSpec((1, d), lambda i, k: (0, 0)),
                  pl.BlockSpec((None, te, d), lambda i, k: (layer, k, 0)),
                  pl.BlockSpec((None, te, d), lambda i, k: (layer, k, 0)),
                  key1_spec, key1_spec, key2_spec, key2_spec],
        out_specs=pl.BlockSpec((tm, d), lambda i, k: (i, 0)),
        out_shape=jax.ShapeDtypeStruct((t, d), F32),
        scratch_shapes=[pltpu.VMEM((tm, d), BF16), pltpu.VMEM((tm, d), F32),
                        pltpu.VMEM((te, tm), BF16)],
        compiler_params=_cparams(2, 48),
        name="peer_dense",
    )(h, g.reshape(1, d), u, v, e1, c1, e2, r2)


def kernel(x_prompt, x_sample, state_ssm, state_conv, cache_k_win, cache_v_win, norm_mix, norm_ffn, norm_kv, norm_final, m_w_in, m_conv_w, m_conv_b, m_dt_bias, m_a_log, m_d_skip, m_norm, m_w_out, a_w_kv, a_b_kv, a_w_q, a_b_q, a_sinks, a_w_o, a_b_o, p_w_q, p_sub_k1, p_sub_k2, p_u, p_v):
    depth = norm_mix.shape[0]
    n_a = m_w_in.shape[0]
    d = x_prompt.shape[-1]
    inner = m_w_out.shape[1]
    conv_dim = m_conv_w.shape[2]
    heads = m_d_skip.shape[1]
    kvd = a_w_kv.shape[1] // 2
    qdim = p_w_q.shape[2] // PEER_HEADS
    dt_col = inner + conv_dim
    in_pad = dt_col + LANES

    w_in = jnp.pad(m_w_in, ((0, 0), (0, 0), (0, in_pad - m_w_in.shape[2]))).astype(BF16)
    w_out = m_w_out.astype(BF16)
    w_kv = a_w_kv.astype(BF16)
    w_q = a_w_q.astype(BF16)
    w_o = a_w_o.astype(BF16)
    pw_q = p_w_q.astype(BF16)
    k1 = p_sub_k1.astype(BF16)
    k2 = p_sub_k2.astype(BF16)
    u_tab = p_u.astype(BF16)
    v_tab = p_v.astype(BF16)
    zeros_in = jnp.zeros((in_pad,), F32)
    zeros_pq = jnp.zeros((p_w_q.shape[2],), F32)
    in_tn = _tile(in_pad, (896, 128))

    def run(x, ssm_init, conv_init, k_prev, v_prev):
        bsz, seq, _ = x.shape
        h = x.reshape(bsz * seq, d)
        ssm_out, conv_out = [], []
        k_src = v_src = None
        for l in range(depth):
            if l < n_a:
                proj = norm_matmul(h, norm_mix[l], w_in[l], zeros_in, in_tn)
                if conv_init is None:
                    prev8 = jnp.zeros((bsz, SUBLANES, conv_dim), F32)
                else:
                    prev8 = jnp.pad(conv_init[l], ((0, 0), (SUBLANES - (CONV_WIDTH - 1), 0), (0, 0)))
                xbc = causal_conv_silu(proj, prev8, m_conv_w[l], m_conv_b[l], bsz, seq, inner)
                s0 = None if ssm_init is None else ssm_init[l].reshape(bsz, inner, SSM_STATE)
                y, s_new = ssd_scan(xbc, proj, dt_col, m_dt_bias[l], m_a_log[l], m_d_skip[l], s0, bsz, seq)
                ssm_out.append(s_new.reshape(bsz, heads, SSM_HEAD_DIM, SSM_STATE))
                tail = proj.reshape(bsz, seq, in_pad)[:, seq - (CONV_WIDTH - 1):, inner:dt_col]
                conv_out.append(tail)
                h = gated_norm_out(y, proj, m_norm[l], w_out[l], h)
            else:
                j = l - n_a
                if j == 0:
                    kv = norm_matmul(h, norm_kv, w_kv, a_b_kv, 2 * kvd)
                    kv3 = kv.reshape(bsz, seq, 2 * kvd)
                    if k_prev is not None:
                        k_all = jnp.concatenate([k_prev.reshape(bsz, WINDOW, kvd), kv3[:, :, :kvd]], axis=1)
                        v_all = jnp.concatenate([v_prev.reshape(bsz, WINDOW, kvd), kv3[:, :, kvd:]], axis=1)
                        k_src = k_all.reshape(bsz * (WINDOW + seq), kvd)
                        v_src = v_all.reshape(bsz * (WINDOW + seq), kvd)
                        k_win, v_win = k_all[:, -WINDOW:], v_all[:, -WINDOW:]
                    else:
                        k_win, v_win = kv3[:, -WINDOW:, :kvd], kv3[:, -WINDOW:, kvd:]
                qp = norm_matmul(h, norm_mix[l], w_q[j], a_b_q[j], w_q.shape[2])
                if k_prev is None:
                    o = swa_attention(qp, a_sinks[j], kv, kv, 0, 1, bsz, seq, None)
                else:
                    o = swa_attention(qp, a_sinks[j], k_src, v_src, 0, 0, bsz, seq, WINDOW + seq)
                h = matmul_bias_res(o, w_o[j], a_b_o[j], h)
            qh = norm_matmul(h, norm_ffn[l], pw_q[l], zeros_pq, pw_q.shape[2] // 2, head_dim=qdim)
            e1, c1, e2, r2 = peer_topk(qh, k1[l], k2[l])
            h = peer_dense(h, norm_ffn[l], u_tab, v_tab, l, e1, c1, e2, r2)
        y = rmsnorm_rows(h, norm_final).reshape(bsz, seq, d)
        kshape = (bsz, WINDOW, KV_HEADS, ATTN_HEAD_DIM)
        return y, jnp.stack(ssm_out), jnp.stack(conv_out), k_win.reshape(kshape), v_win.reshape(kshape)

    y_p, p_ssm, p_conv, p_k, p_v_win = run(x_prompt, None, None, None, None)
    y_s, s_ssm, s_conv, s_k, s_v = run(x_sample, state_ssm, state_conv, cache_k_win, cache_v_win)
    return (y_p, y_s, p_ssm, p_conv, p_k, p_v_win, s_ssm, s_conv, s_k, s_v)
```

```python
import functools

import jax
import jax.numpy as jnp
from jax import lax
from jax.experimental import pallas as pl
from jax.experimental.pallas import tpu as pltpu

F32 = jnp.float32
BF16 = jnp.bfloat16
EPS = 1e-6

WINDOW = 128
SSM_HEAD_DIM = 64
SSM_STATE = 128
SSM_GROUPS = 8
CONV_WIDTH = 4
SSD_BLOCK = 128
ATTN_CHUNK = 64
ATTN_HEAD_DIM = 64
KV_HEADS = 4
PEER_HEADS = 8
PEER_NKEYS = 128
PEER_TOPK = 16
LANES = 128
SUBLANES = 8
MIB = 1024 * 1024

_NT = (((1,), (1,)), ((), ()))
_GELU_C1 = 0.7978845608028654
_GELU_C3 = _GELU_C1 * 0.044715
_GATE_PAIRS_PER_TRIP = 8


def _cparams(n_axes, vmem_mib, flags=None):
    return pltpu.CompilerParams(
        dimension_semantics=("arbitrary",) * n_axes, vmem_limit_bytes=vmem_mib * MIB, flags=flags)


def _tile(n, prefs):
    for p in prefs:
        if n % p == 0:
            return p
    return n


def _rms(x, g):
    ms = jnp.mean(x * x, axis=-1, keepdims=True)
    return x * lax.rsqrt(ms + EPS) * g


def _norm_matmul_kernel(x_ref, g_ref, w_ref, b_ref, o_ref, xn_ref, *, head_dim):
    @pl.when(pl.program_id(1) == 0)
    def _():
        xn_ref[...] = _rms(x_ref[...], g_ref[...]).astype(BF16)

    res = jnp.dot(xn_ref[...], w_ref[...], preferred_element_type=F32) + b_ref[...]
    if head_dim is None:
        o_ref[...] = res
    else:
        for k in range(o_ref.shape[0]):
            o_ref[k] = res[:, k * head_dim:(k + 1) * head_dim]


def norm_matmul(x, g, w, b, tn, head_dim=None):
    t, d = x.shape
    n = w.shape[1]
    tm = _tile(t, (1024, 512, 256))
    if head_dim is not None:
        out_shape = jax.ShapeDtypeStruct((n // head_dim, t, head_dim), F32)
        out_spec = pl.BlockSpec((tn // head_dim, tm, head_dim), lambda i, j: (j, i, 0))
    else:
        out_shape = jax.ShapeDtypeStruct((t, n), F32)
        out_spec = pl.BlockSpec((tm, tn), lambda i, j: (i, j))
    return pl.pallas_call(
        functools.partial(_norm_matmul_kernel, head_dim=head_dim),
        grid=(t // tm, n // tn),
        in_specs=[pl.BlockSpec((tm, d), lambda i, j: (i, 0)),
                  pl.BlockSpec((1, d), lambda i, j: (0, 0)),
                  pl.BlockSpec((d, tn), lambda i, j: (0, j)),
                  pl.BlockSpec((1, tn), lambda i, j: (0, j))],
        out_specs=out_spec,
        out_shape=out_shape,
        scratch_shapes=[pltpu.VMEM((tm, d), BF16)],
        compiler_params=_cparams(2, 40),
        name="norm_matmul",
    )(x, g.reshape(1, d), w, b.reshape(1, n))


def _matmul_res_kernel(x_ref, w_ref, b_ref, r_ref, o_ref):
    mix = jnp.dot(x_ref[...].astype(BF16), w_ref[...], preferred_element_type=F32) + b_ref[...]
    o_ref[...] = r_ref[...] + mix


def matmul_bias_res(x, w, b, res):
    t, k = x.shape
    n = w.shape[1]
    tm = _tile(t, (512, 256))
    return pl.pallas_call(
        _matmul_res_kernel,
        grid=(t // tm,),
        in_specs=[pl.BlockSpec((tm, k), lambda i: (i, 0)),
                  pl.BlockSpec((k, n), lambda i: (0, 0)),
                  pl.BlockSpec((1, n), lambda i: (0, 0)),
                  pl.BlockSpec((tm, n), lambda i: (i, 0))],
        out_specs=pl.BlockSpec((tm, n), lambda i: (i, 0)),
        out_shape=jax.ShapeDtypeStruct((t, n), F32),
        compiler_params=_cparams(1, 32),
        name="matmul_bias_res",
    )(x, w, b.reshape(1, n), res)


def _rmsnorm_kernel(x_ref, g_ref, o_ref):
    o_ref[...] = _rms(x_ref[...], g_ref[...])


def rmsnorm_rows(x, g):
    t, d = x.shape
    tm = _tile(t, (512, 256))
    return pl.pallas_call(
        _rmsnorm_kernel,
        grid=(t // tm,),
        in_specs=[pl.BlockSpec((tm, d), lambda i: (i, 0)), pl.BlockSpec((1, d), lambda i: (0, 0))],
        out_specs=pl.BlockSpec((tm, d), lambda i: (i, 0)),
        out_shape=jax.ShapeDtypeStruct((t, d), F32),
        compiler_params=_cparams(1, 32),
        name="rmsnorm",
    )(x, g.reshape(1, d))


def _conv_kernel(x_ref, halo_ref, prev_ref, w_ref, b_ref, o_ref):
    first = pl.program_id(1) == 0
    x = x_ref[...]
    ext = jnp.concatenate([jnp.where(first, prev_ref[...], halo_ref[...]), x], axis=0)
    acc = b_ref[...]
    for k in range(CONV_WIDTH - 1):
        tap = pltpu.roll(ext, CONV_WIDTH - 1 - k, 0)[SUBLANES:]
        acc = acc + tap * w_ref[k:k + 1, :]
    acc = acc + x * w_ref[CONV_WIDTH - 1:CONV_WIDTH, :]
    o_ref[...] = acc * jax.nn.sigmoid(acc)


def causal_conv_silu(proj, prev8, w, b, bsz, seq, col0):
    c = w.shape[1]
    tl = _tile(seq, (256, 128, 64, 32, 16))
    tc = 1024
    nl = seq // tl
    cb0 = col0 // tc
    hpb = tl // SUBLANES
    return pl.pallas_call(
        _conv_kernel,
        grid=(bsz, nl, c // tc),
        in_specs=[pl.BlockSpec((tl, tc), lambda bi, li, ci: (bi * nl + li, cb0 + ci)),
                  pl.BlockSpec((SUBLANES, tc),
                               lambda bi, li, ci: (jnp.maximum((bi * nl + li) * hpb - 1, 0), cb0 + ci)),
                  pl.BlockSpec((None, SUBLANES, tc), lambda bi, li, ci: (bi, 0, ci)),
                  pl.BlockSpec((CONV_WIDTH, tc), lambda bi, li, ci: (0, ci)),
                  pl.BlockSpec((1, tc), lambda bi, li, ci: (0, ci))],
        out_specs=pl.BlockSpec((tl, tc), lambda bi, li, ci: (bi * nl + li, ci)),
        out_shape=jax.ShapeDtypeStruct((bsz * seq, c), F32),
        compiler_params=_cparams(3, 32),
        name="conv_silu",
    )(proj, proj, prev8, w, b.reshape(1, c))


def _split3(x):
    hi = x.astype(BF16)
    r1 = x - hi.astype(F32)
    mid = r1.astype(BF16)
    lo = (r1 - mid.astype(F32)).astype(BF16)
    return hi, mid, lo


def _ssd_kernel(*refs, rows, has_init):
    if has_init:
        (x_ref, b_ref, c_ref, dt_ref, bias_ref, alog_ref, dskip_ref, echan_ref, ecol_ref, s0_ref,
         y_ref, sout_ref, st_ref) = refs
    else:
        (x_ref, b_ref, c_ref, dt_ref, bias_ref, alog_ref, dskip_ref, echan_ref, ecol_ref,
         y_ref, sout_ref, st_ref) = refs
    q = SSD_BLOCK
    gw = x_ref.shape[1] // SSM_GROUPS
    hpg = gw // SSM_HEAD_DIM
    ci = pl.program_id(1)

    @pl.when(ci == 0)
    def _():
        for g in range(SSM_GROUPS):
            if has_init:
                st_ref[g] = s0_ref[g * gw:(g + 1) * gw, :].T
            else:
                st_ref[g] = jnp.zeros((SSM_STATE, gw), F32)

    def pad(a):
        if rows == q:
            return a
        return jnp.concatenate([a, jnp.zeros((q - rows, a.shape[1]), a.dtype)], axis=0)

    raw = dt_ref[...] + bias_ref[...]
    dt = pad(jnp.maximum(raw, 0.0) + jnp.log1p(jnp.exp(-jnp.abs(raw))))
    a_neg = -jnp.exp(alog_ref[...])
    dta = dt * a_neg
    ri = lax.broadcasted_iota(jnp.int32, (q, q), 0)
    cj = lax.broadcasted_iota(jnp.int32, (q, q), 1)
    tri = ri >= cj
    trib = jnp.where(tri, 1.0, 0.0).astype(BF16)
    acum = sum(jnp.dot(trib, part, preferred_element_type=F32) for part in _split3(dta))
    acum_t = acum.T
    a_last = acum[q - 1:q, :]
    dec_end = jnp.exp(a_last - acum)
    ea = jnp.exp(acum)
    wgt = dt * dec_end

    two = lambda a: jnp.concatenate(_split3(a)[:2], axis=1)
    chan = jnp.dot(jnp.concatenate([two(dt), two(wgt), two(ea)], axis=0), echan_ref[...],
                   preferred_element_type=F32)
    dt_e, wgt_e, ea_e = chan[0:q], chan[q:2 * q], chan[2 * q:3 * q]
    acum_cols = jnp.dot(jnp.concatenate(_split3(acum), axis=1), ecol_ref[...],
                        preferred_element_type=F32)

    xs = pad(x_ref[...])
    bm = pad(b_ref[...])
    cm = pad(c_ref[...])

    for g in range(SSM_GROUPS):
        gsl = slice(g * gw, (g + 1) * gw)
        bg = bm[:, g * SSM_STATE:(g + 1) * SSM_STATE]
        cgb = cm[:, g * SSM_STATE:(g + 1) * SSM_STATE].astype(BF16)
        cb = lax.dot_general(cgb, bg.astype(BF16), _NT, preferred_element_type=F32)
        xg = xs[:, gsl]
        xdt = (xg * dt_e[:, gsl]).astype(BF16)
        xw = (xg * wgt_e[:, gsl]).astype(BF16)
        st_prev = st_ref[g]
        y_off = jnp.dot(cgb, st_prev.astype(BF16), preferred_element_type=F32) * ea_e[:, gsl]
        st_new = jnp.dot(bg.T.astype(BF16), xw, preferred_element_type=F32)
        st_ref[g] = ea_e[q - 1:q, gsl] * st_prev + st_new
        yd = []
        for r in range(hpg):
            h = g * hpg + r
            seg = acum_cols[:, h * q:(h + 1) * q] - acum_t[h:h + 1, :]
            lmat = jnp.where(tri, jnp.exp(seg), 0.0)
            m = (cb * lmat).astype(BF16)
            yd.append(jnp.dot(m, xdt[:, r * SSM_HEAD_DIM:(r + 1) * SSM_HEAD_DIM],
                              preferred_element_type=F32))
        yg = jnp.concatenate(yd, axis=1) + y_off + xg * dskip_ref[:, g * gw:(g + 1) * gw]
        y_ref[:, g * gw:(g + 1) * gw] = yg[:rows]

    @pl.when(ci == pl.num_programs(1) - 1)
    def _():
        for g in range(SSM_GROUPS):
            sout_ref[g * gw:(g + 1) * gw, :] = st_ref[g].T


def ssd_scan(xbc, proj, dt_col, dt_bias, a_log, d_skip, s0, bsz, seq):
    inner = d_skip.shape[0] * SSM_HEAD_DIM
    gn = SSM_GROUPS * SSM_STATE
    rows = min(SSD_BLOCK, seq)
    nc = seq // rows
    heads = d_skip.shape[0]
    pad128 = lambda v: jnp.pad(v, (0, LANES - heads)).reshape(1, LANES)
    row_map = lambda bi, ci: bi * nc + ci
    in_specs = [pl.BlockSpec((rows, inner), lambda bi, ci: (row_map(bi, ci), 0)),
                pl.BlockSpec((rows, gn), lambda bi, ci: (row_map(bi, ci), inner // gn)),
                pl.BlockSpec((rows, gn), lambda bi, ci: (row_map(bi, ci), inner // gn + 1)),
                pl.BlockSpec((rows, LANES), lambda bi, ci: (row_map(bi, ci), dt_col // LANES)),
                pl.BlockSpec((1, LANES), lambda bi, ci: (0, 0)),
                pl.BlockSpec((1, LANES), lambda bi, ci: (0, 0)),
                pl.BlockSpec((1, inner), lambda bi, ci: (0, 0)),
                pl.BlockSpec((2 * LANES, inner), lambda bi, ci: (0, 0)),
                pl.BlockSpec((3 * LANES, heads * SSD_BLOCK), lambda bi, ci: (0, 0))]
    head_of = lambda width: (jnp.arange(heads * width) // width)[None, :] == jnp.arange(LANES)[:, None]
    echan = jnp.tile(head_of(SSM_HEAD_DIM).astype(BF16), (2, 1))
    ecol = jnp.tile(head_of(SSD_BLOCK).astype(BF16), (3, 1))
    args = [xbc, xbc, xbc, proj, pad128(dt_bias), pad128(a_log),
            jnp.repeat(d_skip, SSM_HEAD_DIM).reshape(1, inner), echan, ecol]
    if s0 is not None:
        in_specs.append(pl.BlockSpec((None, inner, SSM_STATE), lambda bi, ci: (bi, 0, 0)))
        args.append(s0)
    return pl.pallas_call(
        functools.partial(_ssd_kernel, rows=rows, has_init=s0 is not None),
        grid=(bsz, nc),
        in_specs=in_specs,
        out_specs=[pl.BlockSpec((rows, inner), lambda bi, ci: (row_map(bi, ci), 0)),
                   pl.BlockSpec((None, inner, SSM_STATE), lambda bi, ci: (bi, 0, 0))],
        out_shape=[jax.ShapeDtypeStruct((bsz * seq, inner), F32),
                   jax.ShapeDtypeStruct((bsz, inner, SSM_STATE), F32)],
        scratch_shapes=[pltpu.VMEM((SSM_GROUPS, SSM_STATE, inner // SSM_GROUPS), F32)],
        compiler_params=_cparams(2, 48),
        name="ssd_scan",
    )(*args)


def _gated_out_kernel(y_ref, z_ref, g_ref, w_ref, r_ref, o_ref):
    z = z_ref[...]
    y = y_ref[...] * (z * jax.nn.sigmoid(z))
    yn = _rms(y, g_ref[...]).astype(BF16)
    o_ref[...] = r_ref[...] + jnp.dot(yn, w_ref[...], preferred_element_type=F32)


def gated_norm_out(y, proj, g, w, res):
    t, inner = y.shape
    n = w.shape[1]
    tm = _tile(t, (256, 128))
    return pl.pallas_call(
        _gated_out_kernel,
        grid=(t // tm,),
        in_specs=[pl.BlockSpec((tm, inner), lambda i: (i, 0)),
                  pl.BlockSpec((tm, inner), lambda i: (i, 0)),
                  pl.BlockSpec((1, inner), lambda i: (0, 0)),
                  pl.BlockSpec((inner, n), lambda i: (0, 0)),
                  pl.BlockSpec((tm, n), lambda i: (i, 0))],
        out_specs=pl.BlockSpec((tm, n), lambda i: (i, 0)),
        out_shape=jax.ShapeDtypeStruct((t, n), F32),
        compiler_params=_cparams(1, 40),
        name="gated_norm_out",
    )(y, proj, g.reshape(1, inner), w, res)


def _attn_banded_kernel(q_ref, sink_ref, kp_ref, kc_ref, vp_ref, vc_ref, o_ref):
    tq = q_ref.shape[0]
    hd = ATTN_HEAD_DIM
    qpk = q_ref.shape[1] // hd // KV_HEADS
    scale = hd ** -0.5
    nk = WINDOW + tq
    back = WINDOW // ATTN_CHUNK
    kc = lax.broadcasted_iota(jnp.int32, (nk, tq), 0) // ATTN_CHUNK
    qc = lax.broadcasted_iota(jnp.int32, (nk, tq), 1) // ATTN_CHUNK
    first_kc = jnp.where(pl.program_id(1) == 0, back, 0)
    valid = (kc >= qc) & (kc <= qc + back) & (kc >= first_kc)
    bias = jnp.concatenate([jnp.where(valid, 0.0, -jnp.inf)] * qpk, axis=1)
    for pair in range(KV_HEADS // 2):
        cols = slice(pair * 2 * hd, (pair + 1) * 2 * hd)
        k2 = jnp.concatenate([kp_ref[:, cols], kc_ref[:, cols]], axis=0)
        v2t = jnp.concatenate([vp_ref[:, cols], vc_ref[:, cols]], axis=0).T
        for sub in range(2):
            kh = pair * 2 + sub
            heads = range(kh * qpk, (kh + 1) * qpk)
            kmat = k2[:, sub * hd:(sub + 1) * hd].astype(BF16)
            vt = v2t[sub * hd:(sub + 1) * hd, :].astype(BF16)
            q4 = jnp.concatenate([q_ref[:, h * hd:(h + 1) * hd] for h in heads], axis=0).astype(BF16)
            s = lax.dot_general(kmat, q4, _NT, preferred_element_type=F32) * scale + bias
            sink = jnp.concatenate([jnp.broadcast_to(sink_ref[:, h:h + 1], (1, tq)) for h in heads], axis=1)
            m = jnp.maximum(jnp.max(s, axis=0, keepdims=True), sink)
            p = jnp.exp(s - m)
            den = jnp.sum(p, axis=0, keepdims=True) + jnp.exp(sink - m)
            ot = jnp.dot(vt, (p * (1.0 / den)).astype(BF16), preferred_element_type=F32)
            for g in range(0, qpk, 2):
                two = jnp.concatenate([ot[:, g * tq:(g + 1) * tq], ot[:, (g + 1) * tq:(g + 2) * tq]], axis=0)
                o_ref[:, (kh * qpk + g) * hd:(kh * qpk + g + 2) * hd] = two.T


def _attn_rows_kernel(q_ref, sink_ref, k_ref, v_ref, o_ref):
    tq = q_ref.shape[0]
    hd = ATTN_HEAD_DIM
    qpk = q_ref.shape[1] // hd // KV_HEADS
    scale = hd ** -0.5
    outs = []
    for kh in range(KV_HEADS):
        kmat = k_ref[:, kh * hd:(kh + 1) * hd].astype(BF16)
        vmat = v_ref[:, kh * hd:(kh + 1) * hd].astype(BF16)
        heads = range(kh * qpk, (kh + 1) * qpk)
        q4 = jnp.concatenate([q_ref[:, h * hd:(h + 1) * hd] for h in heads], axis=0).astype(BF16)
        s = lax.dot_general(q4, kmat, _NT, preferred_element_type=F32) * scale
        sink = jnp.concatenate([jnp.broadcast_to(sink_ref[:, h:h + 1], (tq, 1)) for h in heads], axis=0)
        m = jnp.maximum(jnp.max(s, axis=-1, keepdims=True), sink)
        p = jnp.exp(s - m)
        den = jnp.sum(p, axis=-1, keepdims=True) + jnp.exp(sink - m)
        o4 = jnp.dot((p / den).astype(BF16), vmat, preferred_element_type=F32)
        outs += [o4[g * tq:(g + 1) * tq] for g in range(qpk)]
    o_ref[...] = jnp.concatenate(outs, axis=1)


def swa_attention(q, sinks, k_src, v_src, k_col, v_col, bsz, seq, kv_rows):
    t, hqd = q.shape
    kvd = KV_HEADS * ATTN_HEAD_DIM
    n_heads = sinks.shape[0]
    if kv_rows is None:
        body = _attn_banded_kernel
        tq = _tile(seq, (256, WINDOW))
        nq = seq // tq
        per = tq // WINDOW
        kv_specs = lambda col: [
            pl.BlockSpec((WINDOW, kvd), lambda bi, ci: ((bi * nq + ci) * per - jnp.minimum(ci, 1), col)),
            pl.BlockSpec((tq, kvd), lambda bi, ci: (bi * nq + ci, col))]
    else:
        body = _attn_rows_kernel
        tq, nq = seq, 1
        kv_specs = lambda col: [pl.BlockSpec((kv_rows, kvd), lambda bi, ci: (bi, col))]
    k_specs, v_specs = kv_specs(k_col), kv_specs(v_col)
    in_specs = ([pl.BlockSpec((tq, hqd), lambda bi, ci: (bi * nq + ci, 0)),
                 pl.BlockSpec((1, LANES), lambda bi, ci: (0, 0))] + k_specs + v_specs)
    return pl.pallas_call(
        body,
        grid=(bsz, nq),
        in_specs=in_specs,
        out_specs=pl.BlockSpec((tq, hqd), lambda bi, ci: (bi * nq + ci, 0)),
        out_shape=jax.ShapeDtypeStruct((t, hqd), F32),
        compiler_params=_cparams(2, 40),
        name="swa_attention",
    )(q, jnp.pad(sinks, (0, LANES - n_heads)).reshape(1, LANES),
      *([k_src] * len(k_specs)), *([v_src] * len(v_specs)))


_PAIRS = [(a, b) for a in range(PEER_TOPK) for b in range(PEER_TOPK) if (a + 1) * (b + 1) <= PEER_TOPK]
_CAND_ROWS = -(-len(_PAIRS) // SUBLANES) * SUBLANES


def _extract_topk(s, n_rounds, break_ties):
    n = s.shape[0]
    iota = lax.broadcasted_iota(jnp.int32, s.shape, 0)
    rank = jnp.full(s.shape, n_rounds, jnp.int32)
    vals = []
    for r in range(n_rounds):
        m = jnp.max(s, axis=0, keepdims=True)
        sel = s == m
        if break_ties:
            sel = iota == jnp.min(jnp.where(sel, iota, n), axis=0, keepdims=True)
        rank = jnp.where(sel, r, rank)
        s = jnp.where(sel, -jnp.inf, s)
        vals.append(m)
    count = jnp.sum(jnp.where(rank < n_rounds, 1.0, 0.0), axis=0, keepdims=True)
    return rank, vals, count


def _peer_topk_kernel(q_ref, k1_ref, k2_ref, e1_ref, c1_ref, e2_ref, r2_ref,
                      s_ref, v_ref, cand_ref, sel_ref, tie_ref, rank_ref):
    half = k1_ref.shape[1]
    nh, tm = q_ref.shape[0], q_ref.shape[1]

    for h in range(nh):
        qh = q_ref[h].astype(BF16)
        s_ref[2 * h] = lax.dot_general(k1_ref[...], qh[:, :half], _NT, preferred_element_type=F32)
        s_ref[2 * h + 1] = lax.dot_general(k2_ref[...], qh[:, half:], _NT, preferred_element_type=F32)

    def select(h, break_ties):
        s1_ref, s2_ref = s_ref.at[2 * h], s_ref.at[2 * h + 1]
        for lc in range(tm // LANES):
            lanes = slice(lc * LANES, (lc + 1) * LANES)
            rank1, v1, n1 = _extract_topk(s1_ref[:, lanes], PEER_TOPK, break_ties)
            rank2, v2, n2 = _extract_topk(s2_ref[:, lanes], PEER_TOPK, break_ties)
            rank_ref[:, lanes] = rank1.astype(F32)
            r2_ref[h, lanes, :] = rank2.astype(F32).T
            for r in range(PEER_TOPK):
                v_ref[0, r:r + 1, lanes] = v1[r]
                v_ref[1, r:r + 1, lanes] = v2[r]
            sel_ref[0:1, lanes] = jnp.where((n1 != PEER_TOPK) | (n2 != PEER_TOPK), 1.0, 0.0)
        tied = sel_ref[0:1, :]
        cand_ref[...] = jnp.full((_CAND_ROWS, tm), -jnp.inf, F32)
        for i, (a, b) in enumerate(_PAIRS):
            cand_ref[i:i + 1, :] = v_ref[0, a:a + 1, :] + v_ref[1, b:b + 1, :]
        cand = cand_ref[...]
        crank, _, nc = _extract_topk(cand, PEER_TOPK, break_ties)
        tied = jnp.maximum(tied, jnp.where(nc != PEER_TOPK, 1.0, 0.0))
        chosen = crank < PEER_TOPK
        z = 2.0 * jnp.sum(jnp.where(chosen, jnp.exp(cand - cand[0:1]), 0.0), axis=0, keepdims=True)
        sel_ref[...] = jnp.where(chosen, 1.0, 0.0)
        cnts = []
        row = 0
        for a in range(PEER_TOPK):
            nb = sum(1 for (pa, _) in _PAIRS if pa == a)
            cnts.append(jnp.sum(sel_ref[row:row + nb, :], axis=0, keepdims=True))
            row += nb
        for lc in range(tm // LANES):
            lanes = slice(lc * LANES, (lc + 1) * LANES)
            rank1 = rank_ref[:, lanes]
            c1 = jnp.zeros(rank1.shape, F32)
            for a in range(PEER_TOPK):
                c1 = jnp.where(rank1 == a, cnts[a][:, lanes], c1)
            c1_ref[h, lanes, :] = c1.T
            e1_ref[h, lanes, :] = jnp.exp(s1_ref[:, lanes] - v_ref[0, 0:1, lanes]).T
            e2_ref[h, lanes, :] = (jnp.exp(s2_ref[:, lanes] - v_ref[1, 0:1, lanes]) / z[:, lanes]).T
        return tied

    def fast(h, carry):
        tie_ref[pl.ds(h, 1), :] = select(h, break_ties=False)
        return carry

    def exact(h, carry):
        @pl.when(jnp.max(tie_ref[pl.ds(h, 1), :]) > 0.0)
        def _():
            select(h, break_ties=True)

        return carry

    lax.fori_loop(0, nh, fast, 0)

    @pl.when(jnp.max(tie_ref[...]) > 0.0)
    def _():
        lax.fori_loop(0, nh, exact, 0)


def peer_topk(qh, k1, k2):
    nh, t, qd = qh.shape
    nk = k1.shape[0]
    tm = _tile(t, (2 * LANES, LANES))
    out32 = jax.ShapeDtypeStruct((nh, t, nk), F32)
    ospec = pl.BlockSpec((nh, tm, nk), lambda i: (0, i, 0))
    return pl.pallas_call(
        _peer_topk_kernel,
        grid=(t // tm,),
        in_specs=[pl.BlockSpec((nh, tm, qd), lambda i: (0, i, 0)),
                  pl.BlockSpec((nk, qd // 2), lambda i: (0, 0)),
                  pl.BlockSpec((nk, qd // 2), lambda i: (0, 0))],
        out_specs=[ospec] * 4,
        out_shape=[out32] * 4,
        scratch_shapes=[pltpu.VMEM((2 * nh, nk, tm), F32), pltpu.VMEM((2, PEER_TOPK, tm), F32),
                        pltpu.VMEM((_CAND_ROWS, tm), F32), pltpu.VMEM((_CAND_ROWS, tm), F32),
                        pltpu.VMEM((nh, tm), F32), pltpu.VMEM((nk, tm), F32)],
        compiler_params=_cparams(1, 32),
        name="peer_topk",
    )(qh, k1, k2)


def _peer_dense_kernel(h_ref, g_ref, u_ref, v_ref, e1_ref, c1_ref, e2_ref, r2_ref, o_ref,
                       xn_ref, acc_ref, gate_ref):
    ki = pl.program_id(1)
    nh, tm, nkeys = e2_ref.shape
    te = u_ref.shape[0]
    half_tm = tm // 2

    @pl.when(ki == 0)
    def _():
        xn_ref[...] = _rms(h_ref[...], g_ref[...]).astype(BF16)
        acc_ref[...] = jnp.zeros(acc_ref.shape, F32)
        level = (lax.broadcasted_iota(jnp.int32, (PEER_TOPK, nkeys), 0) + 1).astype(F32)

        def token_gate(t):
            lhs, rhs = [], []
            for h in range(nh):
                row = pl.ds(t, 1)
                lhs.append(jnp.where(c1_ref[h, row, :] == level, e1_ref[h, row, :], 0.0))
                rhs.append(jnp.where(r2_ref[h, row, :] < level, e2_ref[h, row, :], 0.0))
            lhs = jnp.concatenate(lhs, axis=0).astype(BF16)
            rhs = jnp.concatenate(rhs, axis=0).astype(BF16)
            return lax.dot_general(lhs, rhs, (((0,), (0,)), ((), ())), preferred_element_type=F32)

        def bf16_bits(g):
            return lax.bitcast_convert_type(g.astype(BF16).astype(F32), jnp.uint32)

        def token_pairs(i, carry):
            for j in range(_GATE_PAIRS_PER_TRIP):
                p = i * _GATE_PAIRS_PER_TRIP + j
                even, odd = bf16_bits(token_gate(2 * p)), bf16_bits(token_gate(2 * p + 1))
                words = (odd & jnp.uint32(0xFFFF0000)) | (even >> 16)
                gate_ref[pl.ds(pl.multiple_of(p * nkeys, nkeys), nkeys), :] = words
            return carry

        lax.fori_loop(0, half_tm // _GATE_PAIRS_PER_TRIP, token_pairs, 0)

    x = lax.dot_general(xn_ref[...], u_ref[...], _NT, preferred_element_type=F32).astype(BF16)
    act2 = x + x * jnp.tanh(x * (_GELU_C1 + _GELU_C3 * (x * x)))
    pieces = []
    for r in range(te // nkeys):
        key1 = ki * (te // nkeys) + r
        gate = pltpu.bitcast(gate_ref[pl.ds(key1, half_tm, stride=nkeys), :], BF16)
        pieces.append(gate * act2[:, r * nkeys:(r + 1) * nkeys])
    wact = jnp.concatenate(pieces, axis=1)
    acc_ref[...] += jnp.dot(wact, v_ref[...], preferred_element_type=F32)

    @pl.when(ki == pl.num_programs(1) - 1)
    def _():
        o_ref[...] = h_ref[...] + acc_ref[...]


def peer_dense(h, g, u, v, layer, e1, c1, e2, r2):
    t, d = h.shape
    ne = u.shape[1]
    nh, _, nk = e1.shape
    tm = _tile(t, (512, 256))
    te = 1024
    key_spec = pl.BlockSpec((nh, tm, nk), lambda i, k: (0, i, 0), pipeline_mode=pl.Buffered(1))
    return pl.pallas_call(
        _peer_dense_kernel,
        grid=(t // tm, ne // te),
        in_specs=[pl.BlockSpec((tm, d), lambda i, k: (i, 0)),
                  pl.BlockSpec((1, d), lambda i, k: (0, 0)),
                  pl.BlockSpec((None, te, d), lambda i, k: (layer, k, 0)),
                  pl.BlockSpec((None, te, d), lambda i, k: (layer, k, 0)),
                  key_spec, key_spec, key_spec, key_spec],
        out_specs=pl.BlockSpec((tm, d), lambda i, k: (i, 0)),
        out_shape=jax.ShapeDtypeStruct((t, d), F32),
        scratch_shapes=[pltpu.VMEM((tm, d), BF16), pltpu.VMEM((tm, d), F32),
                        pltpu.VMEM((nk * tm // 2, nk), jnp.uint32)],
        compiler_params=_cparams(2, 56),
        name="peer_dense",
    )(h, g.reshape(1, d), u, v, e1, c1, e2, r2)


def kernel(x_prompt, x_sample, state_ssm, state_conv, cache_k_win, cache_v_win, norm_mix, norm_ffn, norm_kv, norm_final, m_w_in, m_conv_w, m_conv_b, m_dt_bias, m_a_log, m_d_skip, m_norm, m_w_out, a_w_kv, a_b_kv, a_w_q, a_b_q, a_sinks, a_w_o, a_b_o, p_w_q, p_sub_k1, p_sub_k2, p_u, p_v):
    depth = norm_mix.shape[0]
    n_a = m_w_in.shape[0]
    d = x_prompt.shape[-1]
    inner = m_w_out.shape[1]
    conv_dim = m_conv_w.shape[2]
    heads = m_d_skip.shape[1]
    kvd = a_w_kv.shape[1] // 2
    qdim = p_w_q.shape[2] // PEER_HEADS
    dt_col = inner + conv_dim
    in_pad = dt_col + LANES

    w_in = jnp.pad(m_w_in, ((0, 0), (0, 0), (0, in_pad - m_w_in.shape[2]))).astype(BF16)
    w_out = m_w_out.astype(BF16)
    w_kv = a_w_kv.astype(BF16)
    w_q = a_w_q.astype(BF16)
    w_o = a_w_o.astype(BF16)
    pw_q = p_w_q.astype(BF16)
    k1 = p_sub_k1.astype(BF16)
    k2 = p_sub_k2.astype(BF16)
    u_tab = p_u.astype(BF16)
    v_tab = p_v.astype(BF16)
    zeros_in = jnp.zeros((in_pad,), F32)
    zeros_pq = jnp.zeros((p_w_q.shape[2],), F32)
    in_tn = _tile(in_pad, (896, 128))

    def run(x, ssm_init, conv_init, k_prev, v_prev):
        bsz, seq, _ = x.shape
        h = x.reshape(bsz * seq, d)
        ssm_out, conv_out = [], []
        k_src = v_src = None
        for l in range(depth):
            if l < n_a:
                proj = norm_matmul(h, norm_mix[l], w_in[l], zeros_in, in_tn)
                if conv_init is None:
                    prev8 = jnp.zeros((bsz, SUBLANES, conv_dim), F32)
                else:
                    prev8 = jnp.pad(conv_init[l], ((0, 0), (SUBLANES - (CONV_WIDTH - 1), 0), (0, 0)))
                xbc = causal_conv_silu(proj, prev8, m_conv_w[l], m_conv_b[l], bsz, seq, inner)
                s0 = None if ssm_init is None else ssm_init[l].reshape(bsz, inner, SSM_STATE)
                y, s_new = ssd_scan(xbc, proj, dt_col, m_dt_bias[l], m_a_log[l], m_d_skip[l], s0, bsz, seq)
                ssm_out.append(s_new.reshape(bsz, heads, SSM_HEAD_DIM, SSM_STATE))
                tail = proj.reshape(bsz, seq, in_pad)[:, seq - (CONV_WIDTH - 1):, inner:dt_col]
                conv_out.append(tail)
                h = gated_norm_out(y, proj, m_norm[l], w_out[l], h)
            else:
                j = l - n_a
                if j == 0:
                    kv = norm_matmul(h, norm_kv, w_kv, a_b_kv, 2 * kvd)
                    kv3 = kv.reshape(bsz, seq, 2 * kvd)
                    if k_prev is not None:
                        k_all = jnp.concatenate([k_prev.reshape(bsz, WINDOW, kvd), kv3[:, :, :kvd]], axis=1)
                        v_all = jnp.concatenate([v_prev.reshape(bsz, WINDOW, kvd), kv3[:, :, kvd:]], axis=1)
                        k_src = k_all.reshape(bsz * (WINDOW + seq), kvd)
                        v_src = v_all.reshape(bsz * (WINDOW + seq), kvd)
                        k_win, v_win = k_all[:, -WINDOW:], v_all[:, -WINDOW:]
                    else:
                        k_win, v_win = kv3[:, -WINDOW:, :kvd], kv3[:, -WINDOW:, kvd:]
                qp = norm_matmul(h, norm_mix[l], w_q[j], a_b_q[j], w_q.shape[2])
                if k_prev is None:
                    o = swa_attention(qp, a_sinks[j], kv, kv, 0, 1, bsz, seq, None)
                else:
                    o = swa_attention(qp, a_sinks[j], k_src, v_src, 0, 0, bsz, seq, WINDOW + seq)
                h = matmul_bias_res(o, w_o[j], a_b_o[j], h)
            qh = norm_matmul(h, norm_ffn[l], pw_q[l], zeros_pq, pw_q.shape[2] // 2, head_dim=qdim)
            e1, c1, e2, r2 = peer_topk(qh, k1[l], k2[l])
            h = peer_dense(h, norm_ffn[l], u_tab, v_tab, l, e1, c1, e2, r2)
        y = rmsnorm_rows(h, norm_final).reshape(bsz, seq, d)
        kshape = (bsz, WINDOW, KV_HEADS, ATTN_HEAD_DIM)
        return y, jnp.stack(ssm_out), jnp.stack(conv_out), k_win.reshape(kshape), v_win.reshape(kshape)

    y_p, p_ssm, p_conv, p_k, p_v_win = run(x_prompt, None, None, None, None)
    y_s, s_ssm, s_conv, s_k, s_v = run(x_sample, state_ssm, state_conv, cache_k_win, cache_v_win)
    return (y_p, y_s, p_ssm, p_conv, p_k, p_v_win, s_ssm, s_conv, s_k, s_v)
```

```python
import functools

import jax
import jax.numpy as jnp
from jax import lax
from jax.experimental import pallas as pl
from jax.experimental.pallas import tpu as pltpu

F32 = jnp.float32
BF16 = jnp.bfloat16
EPS = 1e-6

WINDOW = 128
SSM_HEAD_DIM = 64
SSM_STATE = 128
SSM_GROUPS = 8
CONV_WIDTH = 4
SSD_BLOCK = 128
ATTN_CHUNK = 64
ATTN_HEAD_DIM = 64
KV_HEADS = 4
PEER_HEADS = 8
PEER_NKEYS = 128
PEER_TOPK = 16
LANES = 128
SUBLANES = 8
MIB = 1024 * 1024

_NT = (((1,), (1,)), ((), ()))
_GELU_C1 = 0.7978845608028654
_GELU_C3 = _GELU_C1 * 0.044715
_GATE_PAIRS_PER_TRIP = 16


def _cparams(n_axes, vmem_mib, flags=None):
    return pltpu.CompilerParams(
        dimension_semantics=("arbitrary",) * n_axes, vmem_limit_bytes=vmem_mib * MIB, flags=flags)


def _tile(n, prefs):
    for p in prefs:
        if n % p == 0:
            return p
    return n


def _rms(x, g):
    ms = jnp.mean(x * x, axis=-1, keepdims=True)
    return x * lax.rsqrt(ms + EPS) * g


def _norm_matmul_kernel(x_ref, g_ref, w_ref, b_ref, o_ref, xn_ref, *, head_dim):
    @pl.when(pl.program_id(1) == 0)
    def _():
        xn_ref[...] = _rms(x_ref[...], g_ref[...]).astype(BF16)

    res = jnp.dot(xn_ref[...], w_ref[...], preferred_element_type=F32) + b_ref[...]
    if head_dim is None:
        o_ref[...] = res
    else:
        for k in range(o_ref.shape[0]):
            o_ref[k] = res[:, k * head_dim:(k + 1) * head_dim].astype(o_ref.dtype)


def norm_matmul(x, g, w, b, tn, head_dim=None):
    t, d = x.shape
    n = w.shape[1]
    tm = _tile(t, (1024, 512, 256))
    if head_dim is not None:
        out_shape = jax.ShapeDtypeStruct((n // head_dim, t, head_dim), BF16)
        out_spec = pl.BlockSpec((tn // head_dim, tm, head_dim), lambda i, j: (j, i, 0))
    else:
        out_shape = jax.ShapeDtypeStruct((t, n), F32)
        out_spec = pl.BlockSpec((tm, tn), lambda i, j: (i, j))
    return pl.pallas_call(
        functools.partial(_norm_matmul_kernel, head_dim=head_dim),
        grid=(t // tm, n // tn),
        in_specs=[pl.BlockSpec((tm, d), lambda i, j: (i, 0)),
                  pl.BlockSpec((1, d), lambda i, j: (0, 0)),
                  pl.BlockSpec((d, tn), lambda i, j: (0, j)),
                  pl.BlockSpec((1, tn), lambda i, j: (0, j))],
        out_specs=out_spec,
        out_shape=out_shape,
        scratch_shapes=[pltpu.VMEM((tm, d), BF16)],
        compiler_params=_cparams(2, 40),
        name="norm_matmul",
    )(x, g.reshape(1, d), w, b.reshape(1, n))


def _matmul_res_kernel(x_ref, w_ref, b_ref, r_ref, o_ref):
    mix = jnp.dot(x_ref[...].astype(BF16), w_ref[...], preferred_element_type=F32) + b_ref[...]
    o_ref[...] = r_ref[...] + mix


def matmul_bias_res(x, w, b, res):
    t, k = x.shape
    n = w.shape[1]
    tm = _tile(t, (512, 256))
    return pl.pallas_call(
        _matmul_res_kernel,
        grid=(t // tm,),
        in_specs=[pl.BlockSpec((tm, k), lambda i: (i, 0)),
                  pl.BlockSpec((k, n), lambda i: (0, 0)),
                  pl.BlockSpec((1, n), lambda i: (0, 0)),
                  pl.BlockSpec((tm, n), lambda i: (i, 0))],
        out_specs=pl.BlockSpec((tm, n), lambda i: (i, 0)),
        out_shape=jax.ShapeDtypeStruct((t, n), F32),
        compiler_params=_cparams(1, 32),
        name="matmul_bias_res",
    )(x, w, b.reshape(1, n), res)


def _rmsnorm_kernel(x_ref, g_ref, o_ref):
    o_ref[...] = _rms(x_ref[...], g_ref[...])


def rmsnorm_rows(x, g):
    t, d = x.shape
    tm = _tile(t, (512, 256))
    return pl.pallas_call(
        _rmsnorm_kernel,
        grid=(t // tm,),
        in_specs=[pl.BlockSpec((tm, d), lambda i: (i, 0)), pl.BlockSpec((1, d), lambda i: (0, 0))],
        out_specs=pl.BlockSpec((tm, d), lambda i: (i, 0)),
        out_shape=jax.ShapeDtypeStruct((t, d), F32),
        compiler_params=_cparams(1, 32),
        name="rmsnorm",
    )(x, g.reshape(1, d))


def _conv_kernel(x_ref, halo_ref, prev_ref, w_ref, b_ref, o_ref):
    first = pl.program_id(1) == 0
    x = x_ref[...]
    ext = jnp.concatenate([jnp.where(first, prev_ref[...], halo_ref[...]), x], axis=0)
    acc = b_ref[...]
    for k in range(CONV_WIDTH - 1):
        tap = pltpu.roll(ext, CONV_WIDTH - 1 - k, 0)[SUBLANES:]
        acc = acc + tap * w_ref[k:k + 1, :]
    acc = acc + x * w_ref[CONV_WIDTH - 1:CONV_WIDTH, :]
    o_ref[...] = acc * jax.nn.sigmoid(acc)


def causal_conv_silu(proj, prev8, w, b, bsz, seq, col0):
    c = w.shape[1]
    tl = _tile(seq, (256, 128, 64, 32, 16))
    tc = 1024
    nl = seq // tl
    cb0 = col0 // tc
    hpb = tl // SUBLANES
    return pl.pallas_call(
        _conv_kernel,
        grid=(bsz, nl, c // tc),
        in_specs=[pl.BlockSpec((tl, tc), lambda bi, li, ci: (bi * nl + li, cb0 + ci)),
                  pl.BlockSpec((SUBLANES, tc),
                               lambda bi, li, ci: (jnp.maximum((bi * nl + li) * hpb - 1, 0), cb0 + ci)),
                  pl.BlockSpec((None, SUBLANES, tc), lambda bi, li, ci: (bi, 0, ci)),
                  pl.BlockSpec((CONV_WIDTH, tc), lambda bi, li, ci: (0, ci)),
                  pl.BlockSpec((1, tc), lambda bi, li, ci: (0, ci))],
        out_specs=pl.BlockSpec((tl, tc), lambda bi, li, ci: (bi * nl + li, ci)),
        out_shape=jax.ShapeDtypeStruct((bsz * seq, c), F32),
        compiler_params=_cparams(3, 32),
        name="conv_silu",
    )(proj, proj, prev8, w, b.reshape(1, c))


def _split3(x):
    hi = x.astype(BF16)
    r1 = x - hi.astype(F32)
    mid = r1.astype(BF16)
    lo = (r1 - mid.astype(F32)).astype(BF16)
    return hi, mid, lo


def _ssd_kernel(*refs, rows, has_init):
    if has_init:
        (x_ref, b_ref, c_ref, dt_ref, bias_ref, alog_ref, dskip_ref, echan_ref, ecol_ref, s0_ref,
         y_ref, sout_ref, st_ref) = refs
    else:
        (x_ref, b_ref, c_ref, dt_ref, bias_ref, alog_ref, dskip_ref, echan_ref, ecol_ref,
         y_ref, sout_ref, st_ref) = refs
    q = SSD_BLOCK
    gw = x_ref.shape[1] // SSM_GROUPS
    hpg = gw // SSM_HEAD_DIM
    ci = pl.program_id(1)

    @pl.when(ci == 0)
    def _():
        for g in range(SSM_GROUPS):
            if has_init:
                st_ref[g] = s0_ref[g * gw:(g + 1) * gw, :].T
            else:
                st_ref[g] = jnp.zeros((SSM_STATE, gw), F32)

    def pad(a):
        if rows == q:
            return a
        return jnp.concatenate([a, jnp.zeros((q - rows, a.shape[1]), a.dtype)], axis=0)

    raw = dt_ref[...] + bias_ref[...]
    dt = pad(jnp.maximum(raw, 0.0) + jnp.log1p(jnp.exp(-jnp.abs(raw))))
    a_neg = -jnp.exp(alog_ref[...])
    dta = dt * a_neg
    ri = lax.broadcasted_iota(jnp.int32, (q, q), 0)
    cj = lax.broadcasted_iota(jnp.int32, (q, q), 1)
    tri = ri >= cj
    trib = jnp.where(tri, 1.0, 0.0).astype(BF16)
    acum = sum(jnp.dot(trib, part, preferred_element_type=F32) for part in _split3(dta))
    acum_t = acum.T
    a_last = acum[q - 1:q, :]
    dec_end = jnp.exp(a_last - acum)
    ea = jnp.exp(acum)
    wgt = dt * dec_end

    two = lambda a: jnp.concatenate(_split3(a)[:2], axis=1)
    chan = jnp.dot(jnp.concatenate([two(dt), two(wgt), two(ea)], axis=0), echan_ref[...],
                   preferred_element_type=F32)
    dt_e, wgt_e, ea_e = chan[0:q], chan[q:2 * q], chan[2 * q:3 * q]
    acum_cols = jnp.dot(jnp.concatenate(_split3(acum), axis=1), ecol_ref[...],
                        preferred_element_type=F32)

    xs = pad(x_ref[...])
    bm = pad(b_ref[...])
    cm = pad(c_ref[...])

    for g in range(SSM_GROUPS):
        gsl = slice(g * gw, (g + 1) * gw)
        bg = bm[:, g * SSM_STATE:(g + 1) * SSM_STATE]
        cgb = cm[:, g * SSM_STATE:(g + 1) * SSM_STATE].astype(BF16)
        cb = lax.dot_general(cgb, bg.astype(BF16), _NT, preferred_element_type=F32)
        xg = xs[:, gsl]
        xdt = (xg * dt_e[:, gsl]).astype(BF16)
        xw = (xg * wgt_e[:, gsl]).astype(BF16)
        st_prev = st_ref[g]
        y_off = jnp.dot(cgb, st_prev.astype(BF16), preferred_element_type=F32) * ea_e[:, gsl]
        st_new = jnp.dot(bg.T.astype(BF16), xw, preferred_element_type=F32)
        st_ref[g] = ea_e[q - 1:q, gsl] * st_prev + st_new
        yd = []
        for r in range(hpg):
            h = g * hpg + r
            seg = acum_cols[:, h * q:(h + 1) * q] - acum_t[h:h + 1, :]
            lmat = jnp.where(tri, jnp.exp(seg), 0.0)
            m = (cb * lmat).astype(BF16)
            yd.append(jnp.dot(m, xdt[:, r * SSM_HEAD_DIM:(r + 1) * SSM_HEAD_DIM],
                              preferred_element_type=F32))
        yg = jnp.concatenate(yd, axis=1) + y_off + xg * dskip_ref[:, g * gw:(g + 1) * gw]
        y_ref[:, g * gw:(g + 1) * gw] = yg[:rows]

    @pl.when(ci == pl.num_programs(1) - 1)
    def _():
        for g in range(SSM_GROUPS):
            sout_ref[g * gw:(g + 1) * gw, :] = st_ref[g].T


def ssd_scan(xbc, proj, dt_col, dt_bias, a_log, d_skip, s0, bsz, seq):
    inner = d_skip.shape[0] * SSM_HEAD_DIM
    gn = SSM_GROUPS * SSM_STATE
    rows = min(SSD_BLOCK, seq)
    nc = seq // rows
    heads = d_skip.shape[0]
    pad128 = lambda v: jnp.pad(v, (0, LANES - heads)).reshape(1, LANES)
    row_map = lambda bi, ci: bi * nc + ci
    in_specs = [pl.BlockSpec((rows, inner), lambda bi, ci: (row_map(bi, ci), 0)),
                pl.BlockSpec((rows, gn), lambda bi, ci: (row_map(bi, ci), inner // gn)),
                pl.BlockSpec((rows, gn), lambda bi, ci: (row_map(bi, ci), inner // gn + 1)),
                pl.BlockSpec((rows, LANES), lambda bi, ci: (row_map(bi, ci), dt_col // LANES)),
                pl.BlockSpec((1, LANES), lambda bi, ci: (0, 0)),
                pl.BlockSpec((1, LANES), lambda bi, ci: (0, 0)),
                pl.BlockSpec((1, inner), lambda bi, ci: (0, 0)),
                pl.BlockSpec((2 * LANES, inner), lambda bi, ci: (0, 0)),
                pl.BlockSpec((3 * LANES, heads * SSD_BLOCK), lambda bi, ci: (0, 0))]
    head_of = lambda width: (jnp.arange(heads * width) // width)[None, :] == jnp.arange(LANES)[:, None]
    echan = jnp.tile(head_of(SSM_HEAD_DIM).astype(BF16), (2, 1))
    ecol = jnp.tile(head_of(SSD_BLOCK).astype(BF16), (3, 1))
    args = [xbc, xbc, xbc, proj, pad128(dt_bias), pad128(a_log),
            jnp.repeat(d_skip, SSM_HEAD_DIM).reshape(1, inner), echan, ecol]
    if s0 is not None:
        in_specs.append(pl.BlockSpec((None, inner, SSM_STATE), lambda bi, ci: (bi, 0, 0)))
        args.append(s0)
    return pl.pallas_call(
        functools.partial(_ssd_kernel, rows=rows, has_init=s0 is not None),
        grid=(bsz, nc),
        in_specs=in_specs,
        out_specs=[pl.BlockSpec((rows, inner), lambda bi, ci: (row_map(bi, ci), 0)),
                   pl.BlockSpec((None, inner, SSM_STATE), lambda bi, ci: (bi, 0, 0))],
        out_shape=[jax.ShapeDtypeStruct((bsz * seq, inner), F32),
                   jax.ShapeDtypeStruct((bsz, inner, SSM_STATE), F32)],
        scratch_shapes=[pltpu.VMEM((SSM_GROUPS, SSM_STATE, inner // SSM_GROUPS), F32)],
        compiler_params=_cparams(2, 48),
        name="ssd_scan",
    )(*args)


def _gated_out_kernel(y_ref, z_ref, g_ref, w_ref, r_ref, o_ref):
    z = z_ref[...]
    y = y_ref[...] * (z * jax.nn.sigmoid(z))
    yn = _rms(y, g_ref[...]).astype(BF16)
    o_ref[...] = r_ref[...] + jnp.dot(yn, w_ref[...], preferred_element_type=F32)


def gated_norm_out(y, proj, g, w, res):
    t, inner = y.shape
    n = w.shape[1]
    tm = _tile(t, (256, 128))
    return pl.pallas_call(
        _gated_out_kernel,
        grid=(t // tm,),
        in_specs=[pl.BlockSpec((tm, inner), lambda i: (i, 0)),
                  pl.BlockSpec((tm, inner), lambda i: (i, 0)),
                  pl.BlockSpec((1, inner), lambda i: (0, 0)),
                  pl.BlockSpec((inner, n), lambda i: (0, 0)),
                  pl.BlockSpec((tm, n), lambda i: (i, 0))],
        out_specs=pl.BlockSpec((tm, n), lambda i: (i, 0)),
        out_shape=jax.ShapeDtypeStruct((t, n), F32),
        compiler_params=_cparams(1, 40),
        name="gated_norm_out",
    )(y, proj, g.reshape(1, inner), w, res)


def _attn_banded_kernel(q_ref, sink_ref, kp_ref, kc_ref, vp_ref, vc_ref, o_ref):
    tq = q_ref.shape[0]
    hd = ATTN_HEAD_DIM
    qpk = q_ref.shape[1] // hd // KV_HEADS
    scale = hd ** -0.5
    nk = WINDOW + tq
    back = WINDOW // ATTN_CHUNK
    kc = lax.broadcasted_iota(jnp.int32, (nk, tq), 0) // ATTN_CHUNK
    qc = lax.broadcasted_iota(jnp.int32, (nk, tq), 1) // ATTN_CHUNK
    first_kc = jnp.where(pl.program_id(1) == 0, back, 0)
    valid = (kc >= qc) & (kc <= qc + back) & (kc >= first_kc)
    bias = jnp.concatenate([jnp.where(valid, 0.0, -jnp.inf)] * qpk, axis=1)
    for pair in range(KV_HEADS // 2):
        cols = slice(pair * 2 * hd, (pair + 1) * 2 * hd)
        k2 = jnp.concatenate([kp_ref[:, cols], kc_ref[:, cols]], axis=0)
        v2t = jnp.concatenate([vp_ref[:, cols], vc_ref[:, cols]], axis=0).T
        for sub in range(2):
            kh = pair * 2 + sub
            heads = range(kh * qpk, (kh + 1) * qpk)
            kmat = k2[:, sub * hd:(sub + 1) * hd].astype(BF16)
            vt = v2t[sub * hd:(sub + 1) * hd, :].astype(BF16)
            q4 = jnp.concatenate([q_ref[:, h * hd:(h + 1) * hd] for h in heads], axis=0).astype(BF16)
            s = lax.dot_general(kmat, q4, _NT, preferred_element_type=F32) * scale + bias
            sink = jnp.concatenate([jnp.broadcast_to(sink_ref[:, h:h + 1], (1, tq)) for h in heads], axis=1)
            m = jnp.maximum(jnp.max(s, axis=0, keepdims=True), sink)
            p = jnp.exp(s - m)
            den = jnp.sum(p, axis=0, keepdims=True) + jnp.exp(sink - m)
            ot = jnp.dot(vt, (p * (1.0 / den)).astype(BF16), preferred_element_type=F32)
            for g in range(0, qpk, 2):
                two = jnp.concatenate([ot[:, g * tq:(g + 1) * tq], ot[:, (g + 1) * tq:(g + 2) * tq]], axis=0)
                o_ref[:, (kh * qpk + g) * hd:(kh * qpk + g + 2) * hd] = two.T


def _attn_rows_kernel(q_ref, sink_ref, k_ref, v_ref, o_ref):
    tq = q_ref.shape[0]
    hd = ATTN_HEAD_DIM
    qpk = q_ref.shape[1] // hd // KV_HEADS
    scale = hd ** -0.5
    outs = []
    for kh in range(KV_HEADS):
        kmat = k_ref[:, kh * hd:(kh + 1) * hd].astype(BF16)
        vmat = v_ref[:, kh * hd:(kh + 1) * hd].astype(BF16)
        heads = range(kh * qpk, (kh + 1) * qpk)
        q4 = jnp.concatenate([q_ref[:, h * hd:(h + 1) * hd] for h in heads], axis=0).astype(BF16)
        s = lax.dot_general(q4, kmat, _NT, preferred_element_type=F32) * scale
        sink = jnp.concatenate([jnp.broadcast_to(sink_ref[:, h:h + 1], (tq, 1)) for h in heads], axis=0)
        m = jnp.maximum(jnp.max(s, axis=-1, keepdims=True), sink)
        p = jnp.exp(s - m)
        den = jnp.sum(p, axis=-1, keepdims=True) + jnp.exp(sink - m)
        o4 = jnp.dot((p / den).astype(BF16), vmat, preferred_element_type=F32)
        outs += [o4[g * tq:(g + 1) * tq] for g in range(qpk)]
    o_ref[...] = jnp.concatenate(outs, axis=1)


def swa_attention(q, sinks, k_src, v_src, k_col, v_col, bsz, seq, kv_rows):
    t, hqd = q.shape
    kvd = KV_HEADS * ATTN_HEAD_DIM
    n_heads = sinks.shape[0]
    if kv_rows is None:
        body = _attn_banded_kernel
        tq = _tile(seq, (256, WINDOW))
        nq = seq // tq
        per = tq // WINDOW
        kv_specs = lambda col: [
            pl.BlockSpec((WINDOW, kvd), lambda bi, ci: ((bi * nq + ci) * per - jnp.minimum(ci, 1), col)),
            pl.BlockSpec((tq, kvd), lambda bi, ci: (bi * nq + ci, col))]
    else:
        body = _attn_rows_kernel
        tq, nq = seq, 1
        kv_specs = lambda col: [pl.BlockSpec((kv_rows, kvd), lambda bi, ci: (bi, col))]
    k_specs, v_specs = kv_specs(k_col), kv_specs(v_col)
    in_specs = ([pl.BlockSpec((tq, hqd), lambda bi, ci: (bi * nq + ci, 0)),
                 pl.BlockSpec((1, LANES), lambda bi, ci: (0, 0))] + k_specs + v_specs)
    return pl.pallas_call(
        body,
        grid=(bsz, nq),
        in_specs=in_specs,
        out_specs=pl.BlockSpec((tq, hqd), lambda bi, ci: (bi * nq + ci, 0)),
        out_shape=jax.ShapeDtypeStruct((t, hqd), F32),
        compiler_params=_cparams(2, 40),
        name="swa_attention",
    )(q, jnp.pad(sinks, (0, LANES - n_heads)).reshape(1, LANES),
      *([k_src] * len(k_specs)), *([v_src] * len(v_specs)))


_PAIRS = [(a, b) for a in range(PEER_TOPK) for b in range(PEER_TOPK) if (a + 1) * (b + 1) <= PEER_TOPK]
_CAND_ROWS = -(-len(_PAIRS) // SUBLANES) * SUBLANES


def _extract_topk(s, n_rounds, break_ties):
    n = s.shape[0]
    iota = lax.broadcasted_iota(jnp.int32, s.shape, 0)
    rank = jnp.full(s.shape, n_rounds, jnp.int32)
    vals = []
    for r in range(n_rounds):
        m = jnp.max(s, axis=0, keepdims=True)
        sel = s == m
        if break_ties:
            sel = iota == jnp.min(jnp.where(sel, iota, n), axis=0, keepdims=True)
        rank = jnp.where(sel, r, rank)
        s = jnp.where(sel, -jnp.inf, s)
        vals.append(m)
    count = jnp.sum(jnp.where(rank < n_rounds, 1.0, 0.0), axis=0, keepdims=True)
    return rank, vals, count


def _peer_topk_kernel(q_ref, k1_ref, k2_ref, e1_ref, c1_ref, e2_ref, r2_ref,
                      s_ref, v_ref, cand_ref, sel_ref, tie_ref, rank_ref):
    half = k1_ref.shape[1]
    nh, tm = q_ref.shape[0], q_ref.shape[1]

    for h in range(nh):
        qh = q_ref[h]
        s_ref[2 * h] = lax.dot_general(k1_ref[...], qh[:, :half], _NT, preferred_element_type=F32)
        s_ref[2 * h + 1] = lax.dot_general(k2_ref[...], qh[:, half:], _NT, preferred_element_type=F32)

    def select(h, break_ties):
        s1_ref, s2_ref = s_ref.at[2 * h], s_ref.at[2 * h + 1]
        for lc in range(tm // LANES):
            lanes = slice(lc * LANES, (lc + 1) * LANES)
            rank1, v1, n1 = _extract_topk(s1_ref[:, lanes], PEER_TOPK, break_ties)
            rank2, v2, n2 = _extract_topk(s2_ref[:, lanes], PEER_TOPK, break_ties)
            rank_ref[:, lanes] = rank1.astype(F32)
            r2_ref[h, lanes, :] = rank2.astype(F32).T
            for r in range(PEER_TOPK):
                v_ref[0, r:r + 1, lanes] = v1[r]
                v_ref[1, r:r + 1, lanes] = v2[r]
            sel_ref[0:1, lanes] = jnp.where((n1 != PEER_TOPK) | (n2 != PEER_TOPK), 1.0, 0.0)
        tied = sel_ref[0:1, :]
        cand_ref[...] = jnp.full((_CAND_ROWS, tm), -jnp.inf, F32)
        for i, (a, b) in enumerate(_PAIRS):
            cand_ref[i:i + 1, :] = v_ref[0, a:a + 1, :] + v_ref[1, b:b + 1, :]
        cand = cand_ref[...]
        crank, _, nc = _extract_topk(cand, PEER_TOPK, break_ties)
        tied = jnp.maximum(tied, jnp.where(nc != PEER_TOPK, 1.0, 0.0))
        chosen = crank < PEER_TOPK
        z = 2.0 * jnp.sum(jnp.where(chosen, jnp.exp(cand - cand[0:1]), 0.0), axis=0, keepdims=True)
        sel_ref[...] = jnp.where(chosen, 1.0, 0.0)
        cnts = []
        row = 0
        for a in range(PEER_TOPK):
            nb = sum(1 for (pa, _) in _PAIRS if pa == a)
            cnts.append(jnp.sum(sel_ref[row:row + nb, :], axis=0, keepdims=True))
            row += nb
        for lc in range(tm // LANES):
            lanes = slice(lc * LANES, (lc + 1) * LANES)
            rank1 = rank_ref[:, lanes]
            c1 = jnp.zeros(rank1.shape, F32)
            for a in range(PEER_TOPK):
                c1 = jnp.where(rank1 == a, cnts[a][:, lanes], c1)
            c1_ref[h, lanes, :] = c1.T
            e1_ref[h, lanes, :] = jnp.exp(s1_ref[:, lanes] - v_ref[0, 0:1, lanes]).T
            e2_ref[h, lanes, :] = (jnp.exp(s2_ref[:, lanes] - v_ref[1, 0:1, lanes]) / z[:, lanes]).T
        return tied

    def fast(h, carry):
        tie_ref[pl.ds(h, 1), :] = select(h, break_ties=False)
        return carry

    def exact(h, carry):
        @pl.when(jnp.max(tie_ref[pl.ds(h, 1), :]) > 0.0)
        def _():
            select(h, break_ties=True)

        return carry

    lax.fori_loop(0, nh, fast, 0)

    @pl.when(jnp.max(tie_ref[...]) > 0.0)
    def _():
        lax.fori_loop(0, nh, exact, 0)


def peer_topk(qh, k1, k2):
    nh, t, qd = qh.shape
    nk = k1.shape[0]
    tm = _tile(t, (2 * LANES, LANES))
    out32 = jax.ShapeDtypeStruct((nh, t, nk), F32)
    ospec = pl.BlockSpec((nh, tm, nk), lambda i: (0, i, 0))
    return pl.pallas_call(
        _peer_topk_kernel,
        grid=(t // tm,),
        in_specs=[pl.BlockSpec((nh, tm, qd), lambda i: (0, i, 0)),
                  pl.BlockSpec((nk, qd // 2), lambda i: (0, 0)),
                  pl.BlockSpec((nk, qd // 2), lambda i: (0, 0))],
        out_specs=[ospec] * 4,
        out_shape=[out32] * 4,
        scratch_shapes=[pltpu.VMEM((2 * nh, nk, tm), F32), pltpu.VMEM((2, PEER_TOPK, tm), F32),
                        pltpu.VMEM((_CAND_ROWS, tm), F32), pltpu.VMEM((_CAND_ROWS, tm), F32),
                        pltpu.VMEM((nh, tm), F32), pltpu.VMEM((nk, tm), F32)],
        compiler_params=_cparams(1, 32),
        name="peer_topk",
    )(qh, k1, k2)


def _peer_dense_kernel(h_ref, g_ref, u_ref, v_ref, e1_ref, c1_ref, e2_ref, r2_ref, o_ref,
                       xn_ref, acc_ref, gate_ref):
    ki = pl.program_id(1)
    nh, tm, nkeys = e2_ref.shape
    te = u_ref.shape[0]
    half_tm = tm // 2

    @pl.when(ki == 0)
    def _():
        xn_ref[...] = _rms(h_ref[...], g_ref[...]).astype(BF16)
        acc_ref[...] = jnp.zeros(acc_ref.shape, F32)
        level = (lax.broadcasted_iota(jnp.int32, (PEER_TOPK, nkeys), 0) + 1).astype(F32)

        def token_gate(t):
            lhs, rhs = [], []
            for h in range(nh):
                row = pl.ds(t, 1)
                lhs.append(jnp.where(c1_ref[h, row, :] == level, e1_ref[h, row, :], 0.0))
                rhs.append(jnp.where(r2_ref[h, row, :] < level, e2_ref[h, row, :], 0.0))
            lhs = jnp.concatenate(lhs, axis=0).astype(BF16)
            rhs = jnp.concatenate(rhs, axis=0).astype(BF16)
            return lax.dot_general(lhs, rhs, (((0,), (0,)), ((), ())), preferred_element_type=F32)

        def bf16_bits(g):
            return lax.bitcast_convert_type(g.astype(BF16).astype(F32), jnp.uint32)

        def token_pairs(i, carry):
            for j in range(_GATE_PAIRS_PER_TRIP):
                p = i * _GATE_PAIRS_PER_TRIP + j
                even, odd = bf16_bits(token_gate(2 * p)), bf16_bits(token_gate(2 * p + 1))
                words = (odd & jnp.uint32(0xFFFF0000)) | (even >> 16)
                gate_ref[pl.ds(pl.multiple_of(p * nkeys, nkeys), nkeys), :] = words
            return carry

        lax.fori_loop(0, half_tm // _GATE_PAIRS_PER_TRIP, token_pairs, 0)

    x = lax.dot_general(xn_ref[...], u_ref[...], _NT, preferred_element_type=F32).astype(BF16)
    act2 = x + x * jnp.tanh(x * (_GELU_C1 + _GELU_C3 * (x * x)))
    pieces = []
    for r in range(te // nkeys):
        key1 = ki * (te // nkeys) + r
        gate = pltpu.bitcast(gate_ref[pl.ds(key1, half_tm, stride=nkeys), :], BF16)
        pieces.append(gate * act2[:, r * nkeys:(r + 1) * nkeys])
    wact = jnp.concatenate(pieces, axis=1)
    acc_ref[...] += jnp.dot(wact, v_ref[...], preferred_element_type=F32)

    @pl.when(ki == pl.num_programs(1) - 1)
    def _():
        o_ref[...] = h_ref[...] + acc_ref[...]


def peer_dense(h, g, u, v, layer, e1, c1, e2, r2):
    t, d = h.shape
    ne = u.shape[1]
    nh, _, nk = e1.shape
    tm = _tile(t, (512, 256))
    te = 1024
    key_spec = pl.BlockSpec((nh, tm, nk), lambda i, k: (0, i, 0))
    return pl.pallas_call(
        _peer_dense_kernel,
        grid=(t // tm, ne // te),
        in_specs=[pl.BlockSpec((tm, d), lambda i, k: (i, 0)),
                  pl.BlockSpec((1, d), lambda i, k: (0, 0)),
                  pl.BlockSpec((None, te, d), lambda i, k: (layer, k, 0)),
                  pl.BlockSpec((None, te, d), lambda i, k: (layer, k, 0)),
                  key_spec, key_spec, key_spec, key_spec],
        out_specs=pl.BlockSpec((tm, d), lambda i, k: (i, 0)),
        out_shape=jax.ShapeDtypeStruct((t, d), F32),
        scratch_shapes=[pltpu.VMEM((tm, d), BF16), pltpu.VMEM((tm, d), F32),
                        pltpu.VMEM((nk * tm // 2, nk), jnp.uint32)],
        compiler_params=_cparams(2, 56),
        name="peer_dense",
    )(h, g.reshape(1, d), u, v, e1, c1, e2, r2)


def kernel(x_prompt, x_sample, state_ssm, state_conv, cache_k_win, cache_v_win, norm_mix, norm_ffn, norm_kv, norm_final, m_w_in, m_conv_w, m_conv_b, m_dt_bias, m_a_log, m_d_skip, m_norm, m_w_out, a_w_kv, a_b_kv, a_w_q, a_b_q, a_sinks, a_w_o, a_b_o, p_w_q, p_sub_k1, p_sub_k2, p_u, p_v):
    depth = norm_mix.shape[0]
    n_a = m_w_in.shape[0]
    d = x_prompt.shape[-1]
    inner = m_w_out.shape[1]
    conv_dim = m_conv_w.shape[2]
    heads = m_d_skip.shape[1]
    kvd = a_w_kv.shape[1] // 2
    qdim = p_w_q.shape[2] // PEER_HEADS
    dt_col = inner + conv_dim
    in_pad = dt_col + LANES

    w_in = jnp.pad(m_w_in, ((0, 0), (0, 0), (0, in_pad - m_w_in.shape[2]))).astype(BF16)
    w_out = m_w_out.astype(BF16)
    w_kv = a_w_kv.astype(BF16)
    w_q = a_w_q.astype(BF16)
    w_o = a_w_o.astype(BF16)
    pw_q = p_w_q.astype(BF16)
    k1 = p_sub_k1.astype(BF16)
    k2 = p_sub_k2.astype(BF16)
    u_tab = p_u.astype(BF16)
    v_tab = p_v.astype(BF16)
    zeros_in = jnp.zeros((in_pad,), F32)
    zeros_pq = jnp.zeros((p_w_q.shape[2],), F32)
    in_tn = _tile(in_pad, (896, 128))

    def run(x, ssm_init, conv_init, k_prev, v_prev):
        bsz, seq, _ = x.shape
        h = x.reshape(bsz * seq, d)
        ssm_out, conv_out = [], []
        k_src = v_src = None
        for l in range(depth):
            if l < n_a:
                proj = norm_matmul(h, norm_mix[l], w_in[l], zeros_in, in_tn)
                if conv_init is None:
                    prev8 = jnp.zeros((bsz, SUBLANES, conv_dim), F32)
                else:
                    prev8 = jnp.pad(conv_init[l], ((0, 0), (SUBLANES - (CONV_WIDTH - 1), 0), (0, 0)))
                xbc = causal_conv_silu(proj, prev8, m_conv_w[l], m_conv_b[l], bsz, seq, inner)
                s0 = None if ssm_init is None else ssm_init[l].reshape(bsz, inner, SSM_STATE)
                y, s_new = ssd_scan(xbc, proj, dt_col, m_dt_bias[l], m_a_log[l], m_d_skip[l], s0, bsz, seq)
                ssm_out.append(s_new.reshape(bsz, heads, SSM_HEAD_DIM, SSM_STATE))
                tail = proj.reshape(bsz, seq, in_pad)[:, seq - (CONV_WIDTH - 1):, inner:dt_col]
                conv_out.append(tail)
                h = gated_norm_out(y, proj, m_norm[l], w_out[l], h)
            else:
                j = l - n_a
                if j == 0:
                    kv = norm_matmul(h, norm_kv, w_kv, a_b_kv, 2 * kvd)
                    kv3 = kv.reshape(bsz, seq, 2 * kvd)
                    if k_prev is not None:
                        k_all = jnp.concatenate([k_prev.reshape(bsz, WINDOW, kvd), kv3[:, :, :kvd]], axis=1)
                        v_all = jnp.concatenate([v_prev.reshape(bsz, WINDOW, kvd), kv3[:, :, kvd:]], axis=1)
                        k_src = k_all.reshape(bsz * (WINDOW + seq), kvd)
                        v_src = v_all.reshape(bsz * (WINDOW + seq), kvd)
                        k_win, v_win = k_all[:, -WINDOW:], v_all[:, -WINDOW:]
                    else:
                        k_win, v_win = kv3[:, -WINDOW:, :kvd], kv3[:, -WINDOW:, kvd:]
                qp = norm_matmul(h, norm_mix[l], w_q[j], a_b_q[j], w_q.shape[2])
                if k_prev is None:
                    o = swa_attention(qp, a_sinks[j], kv, kv, 0, 1, bsz, seq, None)
                else:
                    o = swa_attention(qp, a_sinks[j], k_src, v_src, 0, 0, bsz, seq, WINDOW + seq)
                h = matmul_bias_res(o, w_o[j], a_b_o[j], h)
            qh = norm_matmul(h, norm_ffn[l], pw_q[l], zeros_pq, pw_q.shape[2] // 2, head_dim=qdim)
            e1, c1, e2, r2 = peer_topk(qh, k1[l], k2[l])
            h = peer_dense(h, norm_ffn[l], u_tab, v_tab, l, e1, c1, e2, r2)
        y = rmsnorm_rows(h, norm_final).reshape(bsz, seq, d)
        kshape = (bsz, WINDOW, KV_HEADS, ATTN_HEAD_DIM)
        return y, jnp.stack(ssm_out), jnp.stack(conv_out), k_win.reshape(kshape), v_win.reshape(kshape)

    y_p, p_ssm, p_conv, p_k, p_v_win = run(x_prompt, None, None, None, None)
    y_s, s_ssm, s_conv, s_k, s_v = run(x_sample, state_ssm, state_conv, cache_k_win, cache_v_win)
    return (y_p, y_s, p_ssm, p_conv, p_k, p_v_win, s_ssm, s_conv, s_k, s_v)
```

```python
import functools

import jax
import jax.numpy as jnp
from jax import lax
from jax.experimental import pallas as pl
from jax.experimental.pallas import tpu as pltpu

F32 = jnp.float32
BF16 = jnp.bfloat16
EPS = 1e-6

WINDOW = 128
SSM_HEAD_DIM = 64
SSM_STATE = 128
SSM_GROUPS = 8
CONV_WIDTH = 4
SSD_BLOCK = 128
ATTN_CHUNK = 64
ATTN_HEAD_DIM = 64
KV_HEADS = 4
PEER_HEADS = 8
PEER_NKEYS = 128
PEER_TOPK = 16
LANES = 128
SUBLANES = 8
MIB = 1024 * 1024

_NT = (((1,), (1,)), ((), ()))
_GELU_C1 = 0.7978845608028654
_GELU_C3 = _GELU_C1 * 0.044715
_GATE_PAIRS_PER_TRIP = 16


def _cparams(n_axes, vmem_mib, flags=None):
    return pltpu.CompilerParams(
        dimension_semantics=("arbitrary",) * n_axes, vmem_limit_bytes=vmem_mib * MIB, flags=flags)


def _tile(n, prefs):
    for p in prefs:
        if n % p == 0:
            return p
    return n


def _rms(x, g):
    ms = jnp.mean(x * x, axis=-1, keepdims=True)
    return x * lax.rsqrt(ms + EPS) * g


def _norm_matmul_kernel(x_ref, g_ref, w_ref, b_ref, o_ref, xn_ref, *, head_dim):
    @pl.when(pl.program_id(1) == 0)
    def _():
        xn_ref[...] = _rms(x_ref[...], g_ref[...]).astype(BF16)

    res = jnp.dot(xn_ref[...], w_ref[...], preferred_element_type=F32) + b_ref[...]
    if head_dim is None:
        o_ref[...] = res
    else:
        for k in range(o_ref.shape[0]):
            o_ref[k] = res[:, k * head_dim:(k + 1) * head_dim].astype(o_ref.dtype)


def norm_matmul(x, g, w, b, tn, head_dim=None):
    t, d = x.shape
    n = w.shape[1]
    tm = _tile(t, (1024, 512, 256))
    if head_dim is not None:
        out_shape = jax.ShapeDtypeStruct((n // head_dim, t, head_dim), BF16)
        out_spec = pl.BlockSpec((tn // head_dim, tm, head_dim), lambda i, j: (j, i, 0))
    else:
        out_shape = jax.ShapeDtypeStruct((t, n), F32)
        out_spec = pl.BlockSpec((tm, tn), lambda i, j: (i, j))
    return pl.pallas_call(
        functools.partial(_norm_matmul_kernel, head_dim=head_dim),
        grid=(t // tm, n // tn),
        in_specs=[pl.BlockSpec((tm, d), lambda i, j: (i, 0)),
                  pl.BlockSpec((1, d), lambda i, j: (0, 0)),
                  pl.BlockSpec((d, tn), lambda i, j: (0, j)),
                  pl.BlockSpec((1, tn), lambda i, j: (0, j))],
        out_specs=out_spec,
        out_shape=out_shape,
        scratch_shapes=[pltpu.VMEM((tm, d), BF16)],
        compiler_params=_cparams(2, 40),
        name="norm_matmul",
    )(x, g.reshape(1, d), w, b.reshape(1, n))


def _matmul_res_kernel(x_ref, w_ref, b_ref, r_ref, o_ref):
    mix = jnp.dot(x_ref[...].astype(BF16), w_ref[...], preferred_element_type=F32) + b_ref[...]
    o_ref[...] = r_ref[...] + mix


def matmul_bias_res(x, w, b, res):
    t, k = x.shape
    n = w.shape[1]
    tm = _tile(t, (512, 256))
    return pl.pallas_call(
        _matmul_res_kernel,
        grid=(t // tm,),
        in_specs=[pl.BlockSpec((tm, k), lambda i: (i, 0)),
                  pl.BlockSpec((k, n), lambda i: (0, 0)),
                  pl.BlockSpec((1, n), lambda i: (0, 0)),
                  pl.BlockSpec((tm, n), lambda i: (i, 0))],
        out_specs=pl.BlockSpec((tm, n), lambda i: (i, 0)),
        out_shape=jax.ShapeDtypeStruct((t, n), F32),
        compiler_params=_cparams(1, 32),
        name="matmul_bias_res",
    )(x, w, b.reshape(1, n), res)


def _rmsnorm_kernel(x_ref, g_ref, o_ref):
    o_ref[...] = _rms(x_ref[...], g_ref[...])


def rmsnorm_rows(x, g):
    t, d = x.shape
    tm = _tile(t, (512, 256))
    return pl.pallas_call(
        _rmsnorm_kernel,
        grid=(t // tm,),
        in_specs=[pl.BlockSpec((tm, d), lambda i: (i, 0)), pl.BlockSpec((1, d), lambda i: (0, 0))],
        out_specs=pl.BlockSpec((tm, d), lambda i: (i, 0)),
        out_shape=jax.ShapeDtypeStruct((t, d), F32),
        compiler_params=_cparams(1, 32),
        name="rmsnorm",
    )(x, g.reshape(1, d))


def _conv_kernel(x_ref, halo_ref, prev_ref, w_ref, b_ref, o_ref):
    first = pl.program_id(1) == 0
    x = x_ref[...]
    ext = jnp.concatenate([jnp.where(first, prev_ref[...], halo_ref[...]), x], axis=0)
    acc = b_ref[...]
    for k in range(CONV_WIDTH - 1):
        tap = pltpu.roll(ext, CONV_WIDTH - 1 - k, 0)[SUBLANES:]
        acc = acc + tap * w_ref[k:k + 1, :]
    acc = acc + x * w_ref[CONV_WIDTH - 1:CONV_WIDTH, :]
    o_ref[...] = acc * jax.nn.sigmoid(acc)


def causal_conv_silu(proj, prev8, w, b, bsz, seq, col0):
    c = w.shape[1]
    tl = _tile(seq, (256, 128, 64, 32, 16))
    tc = 1024
    nl = seq // tl
    cb0 = col0 // tc
    hpb = tl // SUBLANES
    return pl.pallas_call(
        _conv_kernel,
        grid=(bsz, nl, c // tc),
        in_specs=[pl.BlockSpec((tl, tc), lambda bi, li, ci: (bi * nl + li, cb0 + ci)),
                  pl.BlockSpec((SUBLANES, tc),
                               lambda bi, li, ci: (jnp.maximum((bi * nl + li) * hpb - 1, 0), cb0 + ci)),
                  pl.BlockSpec((None, SUBLANES, tc), lambda bi, li, ci: (bi, 0, ci)),
                  pl.BlockSpec((CONV_WIDTH, tc), lambda bi, li, ci: (0, ci)),
                  pl.BlockSpec((1, tc), lambda bi, li, ci: (0, ci))],
        out_specs=pl.BlockSpec((tl, tc), lambda bi, li, ci: (bi * nl + li, ci)),
        out_shape=jax.ShapeDtypeStruct((bsz * seq, c), F32),
        compiler_params=_cparams(3, 32),
        name="conv_silu",
    )(proj, proj, prev8, w, b.reshape(1, c))


def _split3(x):
    hi = x.astype(BF16)
    r1 = x - hi.astype(F32)
    mid = r1.astype(BF16)
    lo = (r1 - mid.astype(F32)).astype(BF16)
    return hi, mid, lo


def _ssd_kernel(*refs, rows, has_init):
    if has_init:
        (x_ref, b_ref, c_ref, dt_ref, bias_ref, alog_ref, dskip_ref, echan_ref, ecol_ref, s0_ref,
         y_ref, sout_ref, st_ref) = refs
    else:
        (x_ref, b_ref, c_ref, dt_ref, bias_ref, alog_ref, dskip_ref, echan_ref, ecol_ref,
         y_ref, sout_ref, st_ref) = refs
    q = SSD_BLOCK
    gw = x_ref.shape[1] // SSM_GROUPS
    hpg = gw // SSM_HEAD_DIM
    ci = pl.program_id(1)

    @pl.when(ci == 0)
    def _():
        for g in range(SSM_GROUPS):
            if has_init:
                st_ref[g] = s0_ref[g * gw:(g + 1) * gw, :].T
            else:
                st_ref[g] = jnp.zeros((SSM_STATE, gw), F32)

    def pad(a):
        if rows == q:
            return a
        return jnp.concatenate([a, jnp.zeros((q - rows, a.shape[1]), a.dtype)], axis=0)

    raw = dt_ref[...] + bias_ref[...]
    dt = pad(jnp.maximum(raw, 0.0) + jnp.log1p(jnp.exp(-jnp.abs(raw))))
    a_neg = -jnp.exp(alog_ref[...])
    dta = dt * a_neg
    ri = lax.broadcasted_iota(jnp.int32, (q, q), 0)
    cj = lax.broadcasted_iota(jnp.int32, (q, q), 1)
    tri = ri >= cj
    trib = jnp.where(tri, 1.0, 0.0).astype(BF16)
    acum = sum(jnp.dot(trib, part, preferred_element_type=F32) for part in _split3(dta))
    acum_t = acum.T
    a_last = acum[q - 1:q, :]
    dec_end = jnp.exp(a_last - acum)
    ea = jnp.exp(acum)
    wgt = dt * dec_end

    two = lambda a: jnp.concatenate(_split3(a)[:2], axis=1)
    chan = jnp.dot(jnp.concatenate([two(dt), two(wgt), two(ea)], axis=0), echan_ref[...],
                   preferred_element_type=F32)
    dt_e, wgt_e, ea_e = chan[0:q], chan[q:2 * q], chan[2 * q:3 * q]
    acum_cols = jnp.dot(jnp.concatenate(_split3(acum), axis=1), ecol_ref[...],
                        preferred_element_type=F32)

    xs = pad(x_ref[...])
    bm = pad(b_ref[...])
    cm = pad(c_ref[...])

    for g in range(SSM_GROUPS):
        gsl = slice(g * gw, (g + 1) * gw)
        bg = bm[:, g * SSM_STATE:(g + 1) * SSM_STATE]
        cgb = cm[:, g * SSM_STATE:(g + 1) * SSM_STATE].astype(BF16)
        cb = lax.dot_general(cgb, bg.astype(BF16), _NT, preferred_element_type=F32)
        xg = xs[:, gsl]
        xdt = (xg * dt_e[:, gsl]).astype(BF16)
        xw = (xg * wgt_e[:, gsl]).astype(BF16)
        st_prev = st_ref[g]
        y_off = jnp.dot(cgb, st_prev.astype(BF16), preferred_element_type=F32) * ea_e[:, gsl]
        st_new = jnp.dot(bg.T.astype(BF16), xw, preferred_element_type=F32)
        st_ref[g] = ea_e[q - 1:q, gsl] * st_prev + st_new
        yd = []
        for r in range(hpg):
            h = g * hpg + r
            seg = acum_cols[:, h * q:(h + 1) * q] - acum_t[h:h + 1, :]
            lmat = jnp.where(tri, jnp.exp(seg), 0.0)
            m = (cb * lmat).astype(BF16)
            yd.append(jnp.dot(m, xdt[:, r * SSM_HEAD_DIM:(r + 1) * SSM_HEAD_DIM],
                              preferred_element_type=F32))
        yg = jnp.concatenate(yd, axis=1) + y_off + xg * dskip_ref[:, g * gw:(g + 1) * gw]
        y_ref[:, g * gw:(g + 1) * gw] = yg[:rows]

    @pl.when(ci == pl.num_programs(1) - 1)
    def _():
        for g in range(SSM_GROUPS):
            sout_ref[g * gw:(g + 1) * gw, :] = st_ref[g].T


def ssd_scan(xbc, proj, dt_col, dt_bias, a_log, d_skip, s0, bsz, seq):
    inner = d_skip.shape[0] * SSM_HEAD_DIM
    gn = SSM_GROUPS * SSM_STATE
    rows = min(SSD_BLOCK, seq)
    nc = seq // rows
    heads = d_skip.shape[0]
    pad128 = lambda v: jnp.pad(v, (0, LANES - heads)).reshape(1, LANES)
    row_map = lambda bi, ci: bi * nc + ci
    in_specs = [pl.BlockSpec((rows, inner), lambda bi, ci: (row_map(bi, ci), 0)),
                pl.BlockSpec((rows, gn), lambda bi, ci: (row_map(bi, ci), inner // gn)),
                pl.BlockSpec((rows, gn), lambda bi, ci: (row_map(bi, ci), inner // gn + 1)),
                pl.BlockSpec((rows, LANES), lambda bi, ci: (row_map(bi, ci), dt_col // LANES)),
                pl.BlockSpec((1, LANES), lambda bi, ci: (0, 0)),
                pl.BlockSpec((1, LANES), lambda bi, ci: (0, 0)),
                pl.BlockSpec((1, inner), lambda bi, ci: (0, 0)),
                pl.BlockSpec((2 * LANES, inner), lambda bi, ci: (0, 0)),
                pl.BlockSpec((3 * LANES, heads * SSD_BLOCK), lambda bi, ci: (0, 0))]
    head_of = lambda width: (jnp.arange(heads * width) // width)[None, :] == jnp.arange(LANES)[:, None]
    echan = jnp.tile(head_of(SSM_HEAD_DIM).astype(BF16), (2, 1))
    ecol = jnp.tile(head_of(SSD_BLOCK).astype(BF16), (3, 1))
    args = [xbc, xbc, xbc, proj, pad128(dt_bias), pad128(a_log),
            jnp.repeat(d_skip, SSM_HEAD_DIM).reshape(1, inner), echan, ecol]
    if s0 is not None:
        in_specs.append(pl.BlockSpec((None, inner, SSM_STATE), lambda bi, ci: (bi, 0, 0)))
        args.append(s0)
    return pl.pallas_call(
        functools.partial(_ssd_kernel, rows=rows, has_init=s0 is not None),
        grid=(bsz, nc),
        in_specs=in_specs,
        out_specs=[pl.BlockSpec((rows, inner), lambda bi, ci: (row_map(bi, ci), 0)),
                   pl.BlockSpec((None, inner, SSM_STATE), lambda bi, ci: (bi, 0, 0))],
        out_shape=[jax.ShapeDtypeStruct((bsz * seq, inner), F32),
                   jax.ShapeDtypeStruct((bsz, inner, SSM_STATE), F32)],
        scratch_shapes=[pltpu.VMEM((SSM_GROUPS, SSM_STATE, inner // SSM_GROUPS), F32)],
        compiler_params=_cparams(2, 48),
        name="ssd_scan",
    )(*args)


def _gated_out_kernel(y_ref, z_ref, g_ref, w_ref, r_ref, o_ref):
    z = z_ref[...]
    y = y_ref[...] * (z * jax.nn.sigmoid(z))
    yn = _rms(y, g_ref[...]).astype(BF16)
    o_ref[...] = r_ref[...] + jnp.dot(yn, w_ref[...], preferred_element_type=F32)


def gated_norm_out(y, proj, g, w, res):
    t, inner = y.shape
    n = w.shape[1]
    tm = _tile(t, (256, 128))
    return pl.pallas_call(
        _gated_out_kernel,
        grid=(t // tm,),
        in_specs=[pl.BlockSpec((tm, inner), lambda i: (i, 0)),
                  pl.BlockSpec((tm, inner), lambda i: (i, 0)),
                  pl.BlockSpec((1, inner), lambda i: (0, 0)),
                  pl.BlockSpec((inner, n), lambda i: (0, 0)),
                  pl.BlockSpec((tm, n), lambda i: (i, 0))],
        out_specs=pl.BlockSpec((tm, n), lambda i: (i, 0)),
        out_shape=jax.ShapeDtypeStruct((t, n), F32),
        compiler_params=_cparams(1, 40),
        name="gated_norm_out",
    )(y, proj, g.reshape(1, inner), w, res)


def _attn_banded_kernel(q_ref, sink_ref, kp_ref, kc_ref, vp_ref, vc_ref, o_ref):
    tq = q_ref.shape[0]
    hd = ATTN_HEAD_DIM
    qpk = q_ref.shape[1] // hd // KV_HEADS
    scale = hd ** -0.5
    nk = WINDOW + tq
    back = WINDOW // ATTN_CHUNK
    kc = lax.broadcasted_iota(jnp.int32, (nk, tq), 0) // ATTN_CHUNK
    qc = lax.broadcasted_iota(jnp.int32, (nk, tq), 1) // ATTN_CHUNK
    first_kc = jnp.where(pl.program_id(1) == 0, back, 0)
    valid = (kc >= qc) & (kc <= qc + back) & (kc >= first_kc)
    bias = jnp.concatenate([jnp.where(valid, 0.0, -jnp.inf)] * qpk, axis=1)
    for pair in range(KV_HEADS // 2):
        cols = slice(pair * 2 * hd, (pair + 1) * 2 * hd)
        k2 = jnp.concatenate([kp_ref[:, cols], kc_ref[:, cols]], axis=0)
        v2t = jnp.concatenate([vp_ref[:, cols], vc_ref[:, cols]], axis=0).T
        for sub in range(2):
            kh = pair * 2 + sub
            heads = range(kh * qpk, (kh + 1) * qpk)
            kmat = k2[:, sub * hd:(sub + 1) * hd].astype(BF16)
            vt = v2t[sub * hd:(sub + 1) * hd, :].astype(BF16)
            q4 = jnp.concatenate([q_ref[:, h * hd:(h + 1) * hd] for h in heads], axis=0).astype(BF16)
            s = lax.dot_general(kmat, q4, _NT, preferred_element_type=F32) * scale + bias
            sink = jnp.concatenate([jnp.broadcast_to(sink_ref[:, h:h + 1], (1, tq)) for h in heads], axis=1)
            m = jnp.maximum(jnp.max(s, axis=0, keepdims=True), sink)
            p = jnp.exp(s - m)
            den = jnp.sum(p, axis=0, keepdims=True) + jnp.exp(sink - m)
            ot = jnp.dot(vt, (p * (1.0 / den)).astype(BF16), preferred_element_type=F32)
            for g in range(0, qpk, 2):
                two = jnp.concatenate([ot[:, g * tq:(g + 1) * tq], ot[:, (g + 1) * tq:(g + 2) * tq]], axis=0)
                o_ref[:, (kh * qpk + g) * hd:(kh * qpk + g + 2) * hd] = two.T


def _attn_rows_kernel(q_ref, sink_ref, k_ref, v_ref, o_ref):
    tq = q_ref.shape[0]
    hd = ATTN_HEAD_DIM
    qpk = q_ref.shape[1] // hd // KV_HEADS
    scale = hd ** -0.5
    outs = []
    for kh in range(KV_HEADS):
        kmat = k_ref[:, kh * hd:(kh + 1) * hd].astype(BF16)
        vmat = v_ref[:, kh * hd:(kh + 1) * hd].astype(BF16)
        heads = range(kh * qpk, (kh + 1) * qpk)
        q4 = jnp.concatenate([q_ref[:, h * hd:(h + 1) * hd] for h in heads], axis=0).astype(BF16)
        s = lax.dot_general(q4, kmat, _NT, preferred_element_type=F32) * scale
        sink = jnp.concatenate([jnp.broadcast_to(sink_ref[:, h:h + 1], (tq, 1)) for h in heads], axis=0)
        m = jnp.maximum(jnp.max(s, axis=-1, keepdims=True), sink)
        p = jnp.exp(s - m)
        den = jnp.sum(p, axis=-1, keepdims=True) + jnp.exp(sink - m)
        o4 = jnp.dot((p / den).astype(BF16), vmat, preferred_element_type=F32)
        outs += [o4[g * tq:(g + 1) * tq] for g in range(qpk)]
    o_ref[...] = jnp.concatenate(outs, axis=1)


def swa_attention(q, sinks, k_src, v_src, k_col, v_col, bsz, seq, kv_rows):
    t, hqd = q.shape
    kvd = KV_HEADS * ATTN_HEAD_DIM
    n_heads = sinks.shape[0]
    if kv_rows is None:
        body = _attn_banded_kernel
        tq = _tile(seq, (256, WINDOW))
        nq = seq // tq
        per = tq // WINDOW
        kv_specs = lambda col: [
            pl.BlockSpec((WINDOW, kvd), lambda bi, ci: ((bi * nq + ci) * per - jnp.minimum(ci, 1), col)),
            pl.BlockSpec((tq, kvd), lambda bi, ci: (bi * nq + ci, col))]
    else:
        body = _attn_rows_kernel
        tq, nq = seq, 1
        kv_specs = lambda col: [pl.BlockSpec((kv_rows, kvd), lambda bi, ci: (bi, col))]
    k_specs, v_specs = kv_specs(k_col), kv_specs(v_col)
    in_specs = ([pl.BlockSpec((tq, hqd), lambda bi, ci: (bi * nq + ci, 0)),
                 pl.BlockSpec((1, LANES), lambda bi, ci: (0, 0))] + k_specs + v_specs)
    return pl.pallas_call(
        body,
        grid=(bsz, nq),
        in_specs=in_specs,
        out_specs=pl.BlockSpec((tq, hqd), lambda bi, ci: (bi * nq + ci, 0)),
        out_shape=jax.ShapeDtypeStruct((t, hqd), F32),
        compiler_params=_cparams(2, 40),
        name="swa_attention",
    )(q, jnp.pad(sinks, (0, LANES - n_heads)).reshape(1, LANES),
      *([k_src] * len(k_specs)), *([v_src] * len(v_specs)))


_PAIRS = [(a, b) for a in range(PEER_TOPK) for b in range(PEER_TOPK) if (a + 1) * (b + 1) <= PEER_TOPK]
_CAND_ROWS = -(-len(_PAIRS) // SUBLANES) * SUBLANES
_MARK_BASE = -(2.0 ** 127)
_MARK_STEP = 2.0 ** 115
_MARK_CUT = -1.5 * 2.0 ** 126
_CAND_PAD = -(2.0 ** 126)


def _extract_topk(s, n_rounds, break_ties):
    n = s.shape[0]
    iota = lax.broadcasted_iota(jnp.int32, s.shape, 0)
    vals = []
    for r in range(n_rounds):
        m = jnp.max(s, axis=0, keepdims=True)
        sel = s == m
        if break_ties:
            sel = iota == jnp.min(jnp.where(sel, iota, n), axis=0, keepdims=True)
        s = jnp.where(sel, _MARK_BASE + r * _MARK_STEP, s)
        vals.append(m)
    extracted = s < _MARK_CUT
    rank = jnp.where(extracted, (s - _MARK_BASE) * (1.0 / _MARK_STEP), float(n_rounds))
    count = jnp.sum(jnp.where(extracted, 1.0, 0.0), axis=0, keepdims=True)
    return rank, vals, count


def _peer_topk_kernel(q_ref, k1_ref, k2_ref, e1_ref, c1_ref, e2_ref, r2_ref,
                      s_ref, v_ref, cand_ref, sel_ref, tie_ref, rank_ref):
    half = k1_ref.shape[1]
    nh, tm = q_ref.shape[0], q_ref.shape[1]

    for h in range(nh):
        qh = q_ref[h]
        s_ref[2 * h] = lax.dot_general(k1_ref[...], qh[:, :half], _NT, preferred_element_type=F32)
        s_ref[2 * h + 1] = lax.dot_general(k2_ref[...], qh[:, half:], _NT, preferred_element_type=F32)

    def halves(h, break_ties):
        s1_ref, s2_ref = s_ref.at[2 * h], s_ref.at[2 * h + 1]
        tied = []
        for lc in range(tm // LANES):
            lanes = slice(lc * LANES, (lc + 1) * LANES)
            rank1, v1, n1 = _extract_topk(s1_ref[:, lanes], PEER_TOPK, break_ties)
            rank2, v2, n2 = _extract_topk(s2_ref[:, lanes], PEER_TOPK, break_ties)
            rank_ref[h, :, lanes] = rank1
            r2_ref[h, lanes, :] = rank2.T
            for r in range(PEER_TOPK):
                v_ref[2 * h, r:r + 1, lanes] = v1[r]
                v_ref[2 * h + 1, r:r + 1, lanes] = v2[r]
            tied.append(jnp.where((n1 != PEER_TOPK) | (n2 != PEER_TOPK), 1.0, 0.0))
        return jnp.concatenate(tied, axis=1)

    def joint(h, break_ties):
        s1_ref, s2_ref = s_ref.at[2 * h], s_ref.at[2 * h + 1]
        v1_ref, v2_ref = v_ref.at[2 * h], v_ref.at[2 * h + 1]
        cand_ref[...] = jnp.full((_CAND_ROWS, tm), _CAND_PAD, F32)
        for i, (a, b) in enumerate(_PAIRS):
            cand_ref[i:i + 1, :] = v1_ref[a:a + 1, :] + v2_ref[b:b + 1, :]
        cand = cand_ref[...]
        crank, _, nc = _extract_topk(cand, PEER_TOPK, break_ties)
        chosen = crank < PEER_TOPK
        z = 2.0 * jnp.sum(jnp.where(chosen, jnp.exp(cand - cand[0:1]), 0.0), axis=0, keepdims=True)
        sel_ref[...] = jnp.where(chosen, 1.0, 0.0)
        cnts = []
        row = 0
        for a in range(PEER_TOPK):
            nb = sum(1 for (pa, _) in _PAIRS if pa == a)
            cnts.append(jnp.sum(sel_ref[row:row + nb, :], axis=0, keepdims=True))
            row += nb
        for lc in range(tm // LANES):
            lanes = slice(lc * LANES, (lc + 1) * LANES)
            rank1 = rank_ref[h, :, lanes]
            c1 = jnp.zeros(rank1.shape, F32)
            for a in range(PEER_TOPK):
                c1 = jnp.where(rank1 == a, cnts[a][:, lanes], c1)
            c1_ref[h, lanes, :] = c1.T
            e1_ref[h, lanes, :] = jnp.exp(s1_ref[:, lanes] - v1_ref[0:1, lanes]).T
            e2_ref[h, lanes, :] = (jnp.exp(s2_ref[:, lanes] - v2_ref[0:1, lanes]) / z[:, lanes]).T
        return jnp.where(nc != PEER_TOPK, 1.0, 0.0)

    tie_ref[0:1, :] = halves(0, break_ties=False)

    def fast(h, carry):
        tie_ref[pl.ds(h, 1), :] = halves(h, break_ties=False)
        tie_ref[pl.ds(h - 1, 1), :] = jnp.maximum(tie_ref[pl.ds(h - 1, 1), :], joint(h - 1, break_ties=False))
        return carry

    lax.fori_loop(1, nh, fast, 0)
    tie_ref[nh - 1:nh, :] = jnp.maximum(tie_ref[nh - 1:nh, :], joint(nh - 1, break_ties=False))

    def exact(h, carry):
        @pl.when(jnp.max(tie_ref[pl.ds(h, 1), :]) > 0.0)
        def _():
            halves(h, break_ties=True)
            joint(h, break_ties=True)

        return carry

    @pl.when(jnp.max(tie_ref[...]) > 0.0)
    def _():
        lax.fori_loop(0, nh, exact, 0)


def peer_topk(qh, k1, k2):
    nh, t, qd = qh.shape
    nk = k1.shape[0]
    tm = _tile(t, (2 * LANES, LANES))
    out32 = jax.ShapeDtypeStruct((nh, t, nk), F32)
    ospec = pl.BlockSpec((nh, tm, nk), lambda i: (0, i, 0))
    return pl.pallas_call(
        _peer_topk_kernel,
        grid=(t // tm,),
        in_specs=[pl.BlockSpec((nh, tm, qd), lambda i: (0, i, 0)),
                  pl.BlockSpec((nk, qd // 2), lambda i: (0, 0)),
                  pl.BlockSpec((nk, qd // 2), lambda i: (0, 0))],
        out_specs=[ospec] * 4,
        out_shape=[out32] * 4,
        scratch_shapes=[pltpu.VMEM((2 * nh, nk, tm), F32), pltpu.VMEM((2 * nh, PEER_TOPK, tm), F32),
                        pltpu.VMEM((_CAND_ROWS, tm), F32), pltpu.VMEM((_CAND_ROWS, tm), F32),
                        pltpu.VMEM((nh, tm), F32), pltpu.VMEM((nh, nk, tm), F32)],
        compiler_params=_cparams(1, 32),
        name="peer_topk",
    )(qh, k1, k2)


def _peer_dense_kernel(h_ref, g_ref, u_ref, v_ref, e1_ref, c1_ref, e2_ref, r2_ref, o_ref,
                       xn_ref, acc_ref, gate_ref):
    ki = pl.program_id(1)
    nh, tm, nkeys = e2_ref.shape
    te = u_ref.shape[0]
    half_tm = tm // 2

    @pl.when(ki == 0)
    def _():
        xn_ref[...] = _rms(h_ref[...], g_ref[...]).astype(BF16)
        acc_ref[...] = jnp.zeros(acc_ref.shape, F32)
        level = (lax.broadcasted_iota(jnp.int32, (PEER_TOPK, nkeys), 0) + 1).astype(F32)

        def token_gate(t):
            lhs, rhs = [], []
            for h in range(nh):
                row = pl.ds(t, 1)
                lhs.append(jnp.where(c1_ref[h, row, :] == level, e1_ref[h, row, :], 0.0))
                rhs.append(jnp.where(r2_ref[h, row, :] < level, e2_ref[h, row, :], 0.0))
            lhs = jnp.concatenate(lhs, axis=0).astype(BF16)
            rhs = jnp.concatenate(rhs, axis=0).astype(BF16)
            return lax.dot_general(lhs, rhs, (((0,), (0,)), ((), ())), preferred_element_type=F32)

        def bf16_bits(g):
            return lax.bitcast_convert_type(g.astype(BF16).astype(F32), jnp.uint32)

        def token_pairs(i, carry):
            for j in range(_GATE_PAIRS_PER_TRIP):
                p = i * _GATE_PAIRS_PER_TRIP + j
                even, odd = bf16_bits(token_gate(2 * p)), bf16_bits(token_gate(2 * p + 1))
                words = (odd & jnp.uint32(0xFFFF0000)) | (even >> 16)
                gate_ref[pl.ds(pl.multiple_of(p * nkeys, nkeys), nkeys), :] = words
            return carry

        lax.fori_loop(0, half_tm // _GATE_PAIRS_PER_TRIP, token_pairs, 0)

    x = lax.dot_general(xn_ref[...], u_ref[...], _NT, preferred_element_type=F32).astype(BF16)
    act2 = x + x * jnp.tanh(x * (_GELU_C1 + _GELU_C3 * (x * x)))
    pieces = []
    for r in range(te // nkeys):
        key1 = ki * (te // nkeys) + r
        gate = pltpu.bitcast(gate_ref[pl.ds(key1, half_tm, stride=nkeys), :], BF16)
        pieces.append(gate * act2[:, r * nkeys:(r + 1) * nkeys])
    wact = jnp.concatenate(pieces, axis=1)
    acc_ref[...] += jnp.dot(wact, v_ref[...], preferred_element_type=F32)

    @pl.when(ki == pl.num_programs(1) - 1)
    def _():
        o_ref[...] = h_ref[...] + acc_ref[...]


def peer_dense(h, g, u, v, layer, e1, c1, e2, r2):
    t, d = h.shape
    ne = u.shape[1]
    nh, _, nk = e1.shape
    tm = _tile(t, (512, 256))
    te = 1024
    key_spec = pl.BlockSpec((nh, tm, nk), lambda i, k: (0, i, 0))
    return pl.pallas_call(
        _peer_dense_kernel,
        grid=(t // tm, ne // te),
        in_specs=[pl.BlockSpec((tm, d), lambda i, k: (i, 0)),
                  pl.BlockSpec((1, d), lambda i, k: (0, 0)),
                  pl.BlockSpec((None, te, d), lambda i, k: (layer, k, 0)),
                  pl.BlockSpec((None, te, d), lambda i, k: (layer, k, 0)),
                  key_spec, key_spec, key_spec, key_spec],
        out_specs=pl.BlockSpec((tm, d), lambda i, k: (i, 0)),
        out_shape=jax.ShapeDtypeStruct((t, d), F32),
        scratch_shapes=[pltpu.VMEM((tm, d), BF16), pltpu.VMEM((tm, d), F32),
                        pltpu.VMEM((nk * tm // 2, nk), jnp.uint32)],
        compiler_params=_cparams(2, 56),
        name="peer_dense",
    )(h, g.reshape(1, d), u, v, e1, c1, e2, r2)


def kernel(x_prompt, x_sample, state_ssm, state_conv, cache_k_win, cache_v_win, norm_mix, norm_ffn, norm_kv, norm_final, m_w_in, m_conv_w, m_conv_b, m_dt_bias, m_a_log, m_d_skip, m_norm, m_w_out, a_w_kv, a_b_kv, a_w_q, a_b_q, a_sinks, a_w_o, a_b_o, p_w_q, p_sub_k1, p_sub_k2, p_u, p_v):
    depth = norm_mix.shape[0]
    n_a = m_w_in.shape[0]
    d = x_prompt.shape[-1]
    inner = m_w_out.shape[1]
    conv_dim = m_conv_w.shape[2]
    heads = m_d_skip.shape[1]
    kvd = a_w_kv.shape[1] // 2
    qdim = p_w_q.shape[2] // PEER_HEADS
    dt_col = inner + conv_dim
    in_pad = dt_col + LANES

    w_in = jnp.pad(m_w_in, ((0, 0), (0, 0), (0, in_pad - m_w_in.shape[2]))).astype(BF16)
    w_out = m_w_out.astype(BF16)
    w_kv = a_w_kv.astype(BF16)
    w_q = a_w_q.astype(BF16)
    w_o = a_w_o.astype(BF16)
    pw_q = p_w_q.astype(BF16)
    k1 = p_sub_k1.astype(BF16)
    k2 = p_sub_k2.astype(BF16)
    u_tab = p_u.astype(BF16)
    v_tab = p_v.astype(BF16)
    zeros_in = jnp.zeros((in_pad,), F32)
    zeros_pq = jnp.zeros((p_w_q.shape[2],), F32)
    in_tn = _tile(in_pad, (896, 128))

    def run(x, ssm_init, conv_init, k_prev, v_prev):
        bsz, seq, _ = x.shape
        h = x.reshape(bsz * seq, d)
        ssm_out, conv_out = [], []
        k_src = v_src = None
        for l in range(depth):
            if l < n_a:
                proj = norm_matmul(h, norm_mix[l], w_in[l], zeros_in, in_tn)
                if conv_init is None:
                    prev8 = jnp.zeros((bsz, SUBLANES, conv_dim), F32)
                else:
                    prev8 = jnp.pad(conv_init[l], ((0, 0), (SUBLANES - (CONV_WIDTH - 1), 0), (0, 0)))
                xbc = causal_conv_silu(proj, prev8, m_conv_w[l], m_conv_b[l], bsz, seq, inner)
                s0 = None if ssm_init is None else ssm_init[l].reshape(bsz, inner, SSM_STATE)
                y, s_new = ssd_scan(xbc, proj, dt_col, m_dt_bias[l], m_a_log[l], m_d_skip[l], s0, bsz, seq)
                ssm_out.append(s_new.reshape(bsz, heads, SSM_HEAD_DIM, SSM_STATE))
                tail = proj.reshape(bsz, seq, in_pad)[:, seq - (CONV_WIDTH - 1):, inner:dt_col]
                conv_out.append(tail)
                h = gated_norm_out(y, proj, m_norm[l], w_out[l], h)
            else:
                j = l - n_a
                if j == 0:
                    kv = norm_matmul(h, norm_kv, w_kv, a_b_kv, 2 * kvd)
                    kv3 = kv.reshape(bsz, seq, 2 * kvd)
                    if k_prev is not None:
                        k_all = jnp.concatenate([k_prev.reshape(bsz, WINDOW, kvd), kv3[:, :, :kvd]], axis=1)
                        v_all = jnp.concatenate([v_prev.reshape(bsz, WINDOW, kvd), kv3[:, :, kvd:]], axis=1)
                        k_src = k_all.reshape(bsz * (WINDOW + seq), kvd)
                        v_src = v_all.reshape(bsz * (WINDOW + seq), kvd)
                        k_win, v_win = k_all[:, -WINDOW:], v_all[:, -WINDOW:]
                    else:
                        k_win, v_win = kv3[:, -WINDOW:, :kvd], kv3[:, -WINDOW:, kvd:]
                qp = norm_matmul(h, norm_mix[l], w_q[j], a_b_q[j], w_q.shape[2])
                if k_prev is None:
                    o = swa_attention(qp, a_sinks[j], kv, kv, 0, 1, bsz, seq, None)
                else:
                    o = swa_attention(qp, a_sinks[j], k_src, v_src, 0, 0, bsz, seq, WINDOW + seq)
                h = matmul_bias_res(o, w_o[j], a_b_o[j], h)
            qh = norm_matmul(h, norm_ffn[l], pw_q[l], zeros_pq, pw_q.shape[2] // 2, head_dim=qdim)
            e1, c1, e2, r2 = peer_topk(qh, k1[l], k2[l])
            h = peer_dense(h, norm_ffn[l], u_tab, v_tab, l, e1, c1, e2, r2)
        y = rmsnorm_rows(h, norm_final).reshape(bsz, seq, d)
        kshape = (bsz, WINDOW, KV_HEADS, ATTN_HEAD_DIM)
        return y, jnp.stack(ssm_out), jnp.stack(conv_out), k_win.reshape(kshape), v_win.reshape(kshape)

    y_p, p_ssm, p_conv, p_k, p_v_win = run(x_prompt, None, None, None, None)
    y_s, s_ssm, s_conv, s_k, s_v = run(x_sample, state_ssm, state_conv, cache_k_win, cache_v_win)
    return (y_p, y_s, p_ssm, p_conv, p_k, p_v_win, s_ssm, s_conv, s_k, s_v)
```

```python
import functools

import jax
import jax.numpy as jnp
from jax import lax
from jax.experimental import pallas as pl
from jax.experimental.pallas import tpu as pltpu

F32 = jnp.float32
BF16 = jnp.bfloat16
EPS = 1e-6

WINDOW = 128
SSM_HEAD_DIM = 64
SSM_STATE = 128
SSM_GROUPS = 8
CONV_WIDTH = 4
SSD_BLOCK = 128
ATTN_CHUNK = 64
ATTN_HEAD_DIM = 64
KV_HEADS = 4
PEER_HEADS = 8
PEER_NKEYS = 128
PEER_TOPK = 16
LANES = 128
SUBLANES = 8
MIB = 1024 * 1024

_NT = (((1,), (1,)), ((), ()))
_GELU_C1 = 0.7978845608028654
_GELU_C3 = _GELU_C1 * 0.044715
_GATE_PAIRS_PER_TRIP = 16


def _cparams(n_axes, vmem_mib, flags=None):
    return pltpu.CompilerParams(
        dimension_semantics=("arbitrary",) * n_axes, vmem_limit_bytes=vmem_mib * MIB, flags=flags)


def _tile(n, prefs):
    for p in prefs:
        if n % p == 0:
            return p
    return n


def _rms(x, g):
    ms = jnp.mean(x * x, axis=-1, keepdims=True)
    return x * lax.rsqrt(ms + EPS) * g


def _norm_matmul_kernel(x_ref, g_ref, w_ref, b_ref, o_ref, xn_ref, *, head_dim):
    @pl.when(pl.program_id(1) == 0)
    def _():
        xn_ref[...] = _rms(x_ref[...], g_ref[...]).astype(BF16)

    res = jnp.dot(xn_ref[...], w_ref[...], preferred_element_type=F32) + b_ref[...]
    if head_dim is None:
        o_ref[...] = res
    else:
        for k in range(o_ref.shape[0]):
            o_ref[k] = res[:, k * head_dim:(k + 1) * head_dim].astype(o_ref.dtype)


def norm_matmul(x, g, w, b, tn, head_dim=None):
    t, d = x.shape
    n = w.shape[1]
    tm = _tile(t, (1024, 512, 256))
    if head_dim is not None:
        out_shape = jax.ShapeDtypeStruct((n // head_dim, t, head_dim), BF16)
        out_spec = pl.BlockSpec((tn // head_dim, tm, head_dim), lambda i, j: (j, i, 0))
    else:
        out_shape = jax.ShapeDtypeStruct((t, n), F32)
        out_spec = pl.BlockSpec((tm, tn), lambda i, j: (i, j))
    return pl.pallas_call(
        functools.partial(_norm_matmul_kernel, head_dim=head_dim),
        grid=(t // tm, n // tn),
        in_specs=[pl.BlockSpec((tm, d), lambda i, j: (i, 0)),
                  pl.BlockSpec((1, d), lambda i, j: (0, 0)),
                  pl.BlockSpec((d, tn), lambda i, j: (0, j)),
                  pl.BlockSpec((1, tn), lambda i, j: (0, j))],
        out_specs=out_spec,
        out_shape=out_shape,
        scratch_shapes=[pltpu.VMEM((tm, d), BF16)],
        compiler_params=_cparams(2, 40),
        name="norm_matmul",
    )(x, g.reshape(1, d), w, b.reshape(1, n))


def _matmul_res_kernel(x_ref, w_ref, b_ref, r_ref, o_ref):
    mix = jnp.dot(x_ref[...].astype(BF16), w_ref[...], preferred_element_type=F32) + b_ref[...]
    o_ref[...] = r_ref[...] + mix


def matmul_bias_res(x, w, b, res):
    t, k = x.shape
    n = w.shape[1]
    tm = _tile(t, (512, 256))
    return pl.pallas_call(
        _matmul_res_kernel,
        grid=(t // tm,),
        in_specs=[pl.BlockSpec((tm, k), lambda i: (i, 0)),
                  pl.BlockSpec((k, n), lambda i: (0, 0)),
                  pl.BlockSpec((1, n), lambda i: (0, 0)),
                  pl.BlockSpec((tm, n), lambda i: (i, 0))],
        out_specs=pl.BlockSpec((tm, n), lambda i: (i, 0)),
        out_shape=jax.ShapeDtypeStruct((t, n), F32),
        compiler_params=_cparams(1, 32),
        name="matmul_bias_res",
    )(x, w, b.reshape(1, n), res)


def _rmsnorm_kernel(x_ref, g_ref, o_ref):
    o_ref[...] = _rms(x_ref[...], g_ref[...])


def rmsnorm_rows(x, g):
    t, d = x.shape
    tm = _tile(t, (512, 256))
    return pl.pallas_call(
        _rmsnorm_kernel,
        grid=(t // tm,),
        in_specs=[pl.BlockSpec((tm, d), lambda i: (i, 0)), pl.BlockSpec((1, d), lambda i: (0, 0))],
        out_specs=pl.BlockSpec((tm, d), lambda i: (i, 0)),
        out_shape=jax.ShapeDtypeStruct((t, d), F32),
        compiler_params=_cparams(1, 32),
        name="rmsnorm",
    )(x, g.reshape(1, d))


def _conv_kernel(x_ref, halo_ref, prev_ref, w_ref, b_ref, o_ref):
    first = pl.program_id(1) == 0
    x = x_ref[...]
    ext = jnp.concatenate([jnp.where(first, prev_ref[...], halo_ref[...]), x], axis=0)
    acc = b_ref[...]
    for k in range(CONV_WIDTH - 1):
        tap = pltpu.roll(ext, CONV_WIDTH - 1 - k, 0)[SUBLANES:]
        acc = acc + tap * w_ref[k:k + 1, :]
    acc = acc + x * w_ref[CONV_WIDTH - 1:CONV_WIDTH, :]
    o_ref[...] = acc * jax.nn.sigmoid(acc)


def causal_conv_silu(proj, prev8, w, b, bsz, seq, col0):
    c = w.shape[1]
    tl = _tile(seq, (256, 128, 64, 32, 16))
    tc = 1024
    nl = seq // tl
    cb0 = col0 // tc
    hpb = tl // SUBLANES
    return pl.pallas_call(
        _conv_kernel,
        grid=(bsz, nl, c // tc),
        in_specs=[pl.BlockSpec((tl, tc), lambda bi, li, ci: (bi * nl + li, cb0 + ci)),
                  pl.BlockSpec((SUBLANES, tc),
                               lambda bi, li, ci: (jnp.maximum((bi * nl + li) * hpb - 1, 0), cb0 + ci)),
                  pl.BlockSpec((None, SUBLANES, tc), lambda bi, li, ci: (bi, 0, ci)),
                  pl.BlockSpec((CONV_WIDTH, tc), lambda bi, li, ci: (0, ci)),
                  pl.BlockSpec((1, tc), lambda bi, li, ci: (0, ci))],
        out_specs=pl.BlockSpec((tl, tc), lambda bi, li, ci: (bi * nl + li, ci)),
        out_shape=jax.ShapeDtypeStruct((bsz * seq, c), F32),
        compiler_params=_cparams(3, 32),
        name="conv_silu",
    )(proj, proj, prev8, w, b.reshape(1, c))


def _split3(x):
    hi = x.astype(BF16)
    r1 = x - hi.astype(F32)
    mid = r1.astype(BF16)
    lo = (r1 - mid.astype(F32)).astype(BF16)
    return hi, mid, lo


def _ssd_kernel(*refs, rows, has_init):
    if has_init:
        (x_ref, b_ref, c_ref, dt_ref, bias_ref, alog_ref, dskip_ref, echan_ref, ecol_ref, s0_ref,
         y_ref, sout_ref, st_ref) = refs
    else:
        (x_ref, b_ref, c_ref, dt_ref, bias_ref, alog_ref, dskip_ref, echan_ref, ecol_ref,
         y_ref, sout_ref, st_ref) = refs
    q = SSD_BLOCK
    gw = x_ref.shape[1] // SSM_GROUPS
    hpg = gw // SSM_HEAD_DIM
    ci = pl.program_id(1)

    @pl.when(ci == 0)
    def _():
        for g in range(SSM_GROUPS):
            if has_init:
                st_ref[g] = s0_ref[g * gw:(g + 1) * gw, :].T
            else:
                st_ref[g] = jnp.zeros((SSM_STATE, gw), F32)

    def pad(a):
        if rows == q:
            return a
        return jnp.concatenate([a, jnp.zeros((q - rows, a.shape[1]), a.dtype)], axis=0)

    raw = dt_ref[...] + bias_ref[...]
    dt = pad(jnp.maximum(raw, 0.0) + jnp.log1p(jnp.exp(-jnp.abs(raw))))
    a_neg = -jnp.exp(alog_ref[...])
    dta = dt * a_neg
    ri = lax.broadcasted_iota(jnp.int32, (q, q), 0)
    cj = lax.broadcasted_iota(jnp.int32, (q, q), 1)
    tri = ri >= cj
    trib = jnp.where(tri, 1.0, 0.0).astype(BF16)
    acum = sum(jnp.dot(trib, part, preferred_element_type=F32) for part in _split3(dta))
    acum_t = acum.T
    a_last = acum[q - 1:q, :]
    dec_end = jnp.exp(a_last - acum)
    ea = jnp.exp(acum)
    wgt = dt * dec_end

    two = lambda a: jnp.concatenate(_split3(a)[:2], axis=1)
    chan = jnp.dot(jnp.concatenate([two(dt), two(wgt), two(ea)], axis=0), echan_ref[...],
                   preferred_element_type=F32)
    dt_e, wgt_e, ea_e = chan[0:q], chan[q:2 * q], chan[2 * q:3 * q]
    acum_cols = jnp.dot(jnp.concatenate(_split3(acum), axis=1), ecol_ref[...],
                        preferred_element_type=F32)

    xs = pad(x_ref[...])
    bm = pad(b_ref[...])
    cm = pad(c_ref[...])

    for g in range(SSM_GROUPS):
        gsl = slice(g * gw, (g + 1) * gw)
        bg = bm[:, g * SSM_STATE:(g + 1) * SSM_STATE]
        cgb = cm[:, g * SSM_STATE:(g + 1) * SSM_STATE].astype(BF16)
        cb = lax.dot_general(cgb, bg.astype(BF16), _NT, preferred_element_type=F32)
        xg = xs[:, gsl]
        xdt = (xg * dt_e[:, gsl]).astype(BF16)
        xw = (xg * wgt_e[:, gsl]).astype(BF16)
        st_prev = st_ref[g]
        y_off = jnp.dot(cgb, st_prev.astype(BF16), preferred_element_type=F32) * ea_e[:, gsl]
        st_new = jnp.dot(bg.T.astype(BF16), xw, preferred_element_type=F32)
        st_ref[g] = ea_e[q - 1:q, gsl] * st_prev + st_new
        yd = []
        for r in range(hpg):
            h = g * hpg + r
            seg = acum_cols[:, h * q:(h + 1) * q] - acum_t[h:h + 1, :]
            lmat = jnp.where(tri, jnp.exp(seg), 0.0)
            m = (cb * lmat).astype(BF16)
            yd.append(jnp.dot(m, xdt[:, r * SSM_HEAD_DIM:(r + 1) * SSM_HEAD_DIM],
                              preferred_element_type=F32))
        yg = jnp.concatenate(yd, axis=1) + y_off + xg * dskip_ref[:, g * gw:(g + 1) * gw]
        y_ref[:, g * gw:(g + 1) * gw] = yg[:rows]

    @pl.when(ci == pl.num_programs(1) - 1)
    def _():
        for g in range(SSM_GROUPS):
            sout_ref[g * gw:(g + 1) * gw, :] = st_ref[g].T


def ssd_scan(xbc, proj, dt_col, dt_bias, a_log, d_skip, s0, bsz, seq):
    inner = d_skip.shape[0] * SSM_HEAD_DIM
    gn = SSM_GROUPS * SSM_STATE
    rows = min(SSD_BLOCK, seq)
    nc = seq // rows
    heads = d_skip.shape[0]
    pad128 = lambda v: jnp.pad(v, (0, LANES - heads)).reshape(1, LANES)
    row_map = lambda bi, ci: bi * nc + ci
    in_specs = [pl.BlockSpec((rows, inner), lambda bi, ci: (row_map(bi, ci), 0)),
                pl.BlockSpec((rows, gn), lambda bi, ci: (row_map(bi, ci), inner // gn)),
                pl.BlockSpec((rows, gn), lambda bi, ci: (row_map(bi, ci), inner // gn + 1)),
                pl.BlockSpec((rows, LANES), lambda bi, ci: (row_map(bi, ci), dt_col // LANES)),
                pl.BlockSpec((1, LANES), lambda bi, ci: (0, 0)),
                pl.BlockSpec((1, LANES), lambda bi, ci: (0, 0)),
                pl.BlockSpec((1, inner), lambda bi, ci: (0, 0)),
                pl.BlockSpec((2 * LANES, inner), lambda bi, ci: (0, 0)),
                pl.BlockSpec((3 * LANES, heads * SSD_BLOCK), lambda bi, ci: (0, 0))]
    head_of = lambda width: (jnp.arange(heads * width) // width)[None, :] == jnp.arange(LANES)[:, None]
    echan = jnp.tile(head_of(SSM_HEAD_DIM).astype(BF16), (2, 1))
    ecol = jnp.tile(head_of(SSD_BLOCK).astype(BF16), (3, 1))
    args = [xbc, xbc, xbc, proj, pad128(dt_bias), pad128(a_log),
            jnp.repeat(d_skip, SSM_HEAD_DIM).reshape(1, inner), echan, ecol]
    if s0 is not None:
        in_specs.append(pl.BlockSpec((None, inner, SSM_STATE), lambda bi, ci: (bi, 0, 0)))
        args.append(s0)
    return pl.pallas_call(
        functools.partial(_ssd_kernel, rows=rows, has_init=s0 is not None),
        grid=(bsz, nc),
        in_specs=in_specs,
        out_specs=[pl.BlockSpec((rows, inner), lambda bi, ci: (row_map(bi, ci), 0)),
                   pl.BlockSpec((None, inner, SSM_STATE), lambda bi, ci: (bi, 0, 0))],
        out_shape=[jax.ShapeDtypeStruct((bsz * seq, inner), F32),
                   jax.ShapeDtypeStruct((bsz, inner, SSM_STATE), F32)],
        scratch_shapes=[pltpu.VMEM((SSM_GROUPS, SSM_STATE, inner // SSM_GROUPS), F32)],
        compiler_params=_cparams(2, 48),
        name="ssd_scan",
    )(*args)


def _gated_out_kernel(y_ref, z_ref, g_ref, w_ref, r_ref, o_ref):
    z = z_ref[...]
    y = y_ref[...] * (z * jax.nn.sigmoid(z))
    yn = _rms(y, g_ref[...]).astype(BF16)
    o_ref[...] = r_ref[...] + jnp.dot(yn, w_ref[...], preferred_element_type=F32)


def gated_norm_out(y, proj, g, w, res):
    t, inner = y.shape
    n = w.shape[1]
    tm = _tile(t, (256, 128))
    return pl.pallas_call(
        _gated_out_kernel,
        grid=(t // tm,),
        in_specs=[pl.BlockSpec((tm, inner), lambda i: (i, 0)),
                  pl.BlockSpec((tm, inner), lambda i: (i, 0)),
                  pl.BlockSpec((1, inner), lambda i: (0, 0)),
                  pl.BlockSpec((inner, n), lambda i: (0, 0)),
                  pl.BlockSpec((tm, n), lambda i: (i, 0))],
        out_specs=pl.BlockSpec((tm, n), lambda i: (i, 0)),
        out_shape=jax.ShapeDtypeStruct((t, n), F32),
        compiler_params=_cparams(1, 40),
        name="gated_norm_out",
    )(y, proj, g.reshape(1, inner), w, res)


def _attn_banded_kernel(q_ref, sink_ref, kp_ref, kc_ref, vp_ref, vc_ref, o_ref):
    tq = q_ref.shape[0]
    hd = ATTN_HEAD_DIM
    qpk = q_ref.shape[1] // hd // KV_HEADS
    scale = hd ** -0.5
    nk = WINDOW + tq
    back = WINDOW // ATTN_CHUNK
    kc = lax.broadcasted_iota(jnp.int32, (nk, tq), 0) // ATTN_CHUNK
    qc = lax.broadcasted_iota(jnp.int32, (nk, tq), 1) // ATTN_CHUNK
    first_kc = jnp.where(pl.program_id(1) == 0, back, 0)
    valid = (kc >= qc) & (kc <= qc + back) & (kc >= first_kc)
    bias = jnp.concatenate([jnp.where(valid, 0.0, -jnp.inf)] * qpk, axis=1)
    for pair in range(KV_HEADS // 2):
        cols = slice(pair * 2 * hd, (pair + 1) * 2 * hd)
        k2 = jnp.concatenate([kp_ref[:, cols], kc_ref[:, cols]], axis=0)
        v2t = jnp.concatenate([vp_ref[:, cols], vc_ref[:, cols]], axis=0).T
        for sub in range(2):
            kh = pair * 2 + sub
            heads = range(kh * qpk, (kh + 1) * qpk)
            kmat = k2[:, sub * hd:(sub + 1) * hd].astype(BF16)
            vt = v2t[sub * hd:(sub + 1) * hd, :].astype(BF16)
            q4 = jnp.concatenate([q_ref[:, h * hd:(h + 1) * hd] for h in heads], axis=0).astype(BF16)
            s = lax.dot_general(kmat, q4, _NT, preferred_element_type=F32) * scale + bias
            sink = jnp.concatenate([jnp.broadcast_to(sink_ref[:, h:h + 1], (1, tq)) for h in heads], axis=1)
            m = jnp.maximum(jnp.max(s, axis=0, keepdims=True), sink)
            p = jnp.exp(s - m)
            den = jnp.sum(p, axis=0, keepdims=True) + jnp.exp(sink - m)
            ot = jnp.dot(vt, (p * (1.0 / den)).astype(BF16), preferred_element_type=F32)
            for g in range(0, qpk, 2):
                two = jnp.concatenate([ot[:, g * tq:(g + 1) * tq], ot[:, (g + 1) * tq:(g + 2) * tq]], axis=0)
                o_ref[:, (kh * qpk + g) * hd:(kh * qpk + g + 2) * hd] = two.T


def _attn_rows_kernel(q_ref, sink_ref, k_ref, v_ref, o_ref):
    tq = q_ref.shape[0]
    hd = ATTN_HEAD_DIM
    qpk = q_ref.shape[1] // hd // KV_HEADS
    scale = hd ** -0.5
    outs = []
    for kh in range(KV_HEADS):
        kmat = k_ref[:, kh * hd:(kh + 1) * hd].astype(BF16)
        vmat = v_ref[:, kh * hd:(kh + 1) * hd].astype(BF16)
        heads = range(kh * qpk, (kh + 1) * qpk)
        q4 = jnp.concatenate([q_ref[:, h * hd:(h + 1) * hd] for h in heads], axis=0).astype(BF16)
        s = lax.dot_general(q4, kmat, _NT, preferred_element_type=F32) * scale
        sink = jnp.concatenate([jnp.broadcast_to(sink_ref[:, h:h + 1], (tq, 1)) for h in heads], axis=0)
        m = jnp.maximum(jnp.max(s, axis=-1, keepdims=True), sink)
        p = jnp.exp(s - m)
        den = jnp.sum(p, axis=-1, keepdims=True) + jnp.exp(sink - m)
        o4 = jnp.dot((p / den).astype(BF16), vmat, preferred_element_type=F32)
        outs += [o4[g * tq:(g + 1) * tq] for g in range(qpk)]
    o_ref[...] = jnp.concatenate(outs, axis=1)


def swa_attention(q, sinks, k_src, v_src, k_col, v_col, bsz, seq, kv_rows):
    t, hqd = q.shape
    kvd = KV_HEADS * ATTN_HEAD_DIM
    n_heads = sinks.shape[0]
    if kv_rows is None:
        body = _attn_banded_kernel
        tq = _tile(seq, (256, WINDOW))
        nq = seq // tq
        per = tq // WINDOW
        kv_specs = lambda col: [
            pl.BlockSpec((WINDOW, kvd), lambda bi, ci: ((bi * nq + ci) * per - jnp.minimum(ci, 1), col)),
            pl.BlockSpec((tq, kvd), lambda bi, ci: (bi * nq + ci, col))]
    else:
        body = _attn_rows_kernel
        tq, nq = seq, 1
        kv_specs = lambda col: [pl.BlockSpec((kv_rows, kvd), lambda bi, ci: (bi, col))]
    k_specs, v_specs = kv_specs(k_col), kv_specs(v_col)
    in_specs = ([pl.BlockSpec((tq, hqd), lambda bi, ci: (bi * nq + ci, 0)),
                 pl.BlockSpec((1, LANES), lambda bi, ci: (0, 0))] + k_specs + v_specs)
    return pl.pallas_call(
        body,
        grid=(bsz, nq),
        in_specs=in_specs,
        out_specs=pl.BlockSpec((tq, hqd), lambda bi, ci: (bi * nq + ci, 0)),
        out_shape=jax.ShapeDtypeStruct((t, hqd), F32),
        compiler_params=_cparams(2, 40),
        name="swa_attention",
    )(q, jnp.pad(sinks, (0, LANES - n_heads)).reshape(1, LANES),
      *([k_src] * len(k_specs)), *([v_src] * len(v_specs)))


_PAIRS = [(a, b) for a in range(PEER_TOPK) for b in range(PEER_TOPK) if (a + 1) * (b + 1) <= PEER_TOPK]
_CAND_ROWS = -(-len(_PAIRS) // SUBLANES) * SUBLANES
_MARK_BASE = -(2.0 ** 127)
_MARK_STEP = 2.0 ** 115
_MARK_CUT = -1.5 * 2.0 ** 126
_CAND_PAD = -(2.0 ** 126)


def _extract_topk(s, n_rounds, break_ties):
    n = s.shape[0]
    iota = lax.broadcasted_iota(jnp.int32, s.shape, 0)
    vals = []
    for r in range(n_rounds):
        m = jnp.max(s, axis=0, keepdims=True)
        sel = s == m
        if break_ties:
            sel = iota == jnp.min(jnp.where(sel, iota, n), axis=0, keepdims=True)
        s = jnp.where(sel, _MARK_BASE + r * _MARK_STEP, s)
        vals.append(m)
    extracted = s < _MARK_CUT
    rank = jnp.where(extracted, (s - _MARK_BASE) * (1.0 / _MARK_STEP), float(n_rounds))
    count = jnp.sum(jnp.where(extracted, 1.0, 0.0), axis=0, keepdims=True)
    return rank, vals, count


def _peer_topk_kernel(q_ref, k1_ref, k2_ref, e1_ref, c1_ref, e2_ref, r2_ref,
                      s_ref, v_ref, cand_ref, sel_ref, tie_ref, rank_ref):
    half = k1_ref.shape[1]
    nh, tm = q_ref.shape[0], q_ref.shape[1]

    for h in range(nh):
        qh = q_ref[h]
        s_ref[2 * h] = lax.dot_general(k1_ref[...], qh[:, :half], _NT, preferred_element_type=F32)
        s_ref[2 * h + 1] = lax.dot_general(k2_ref[...], qh[:, half:], _NT, preferred_element_type=F32)

    def halves(h, break_ties):
        s1_ref, s2_ref = s_ref.at[2 * h], s_ref.at[2 * h + 1]
        tied = []
        for lc in range(tm // LANES):
            lanes = slice(lc * LANES, (lc + 1) * LANES)
            rank1, v1, n1 = _extract_topk(s1_ref[:, lanes], PEER_TOPK, break_ties)
            rank2, v2, n2 = _extract_topk(s2_ref[:, lanes], PEER_TOPK, break_ties)
            rank_ref[h, :, lanes] = rank1
            r2_ref[h, lanes, :] = rank2.T
            for r in range(PEER_TOPK):
                v_ref[2 * h, r:r + 1, lanes] = v1[r]
                v_ref[2 * h + 1, r:r + 1, lanes] = v2[r]
            tied.append(jnp.where((n1 != PEER_TOPK) | (n2 != PEER_TOPK), 1.0, 0.0))
        return jnp.concatenate(tied, axis=1)

    def joint(h, break_ties):
        s1_ref, s2_ref = s_ref.at[2 * h], s_ref.at[2 * h + 1]
        v1_ref, v2_ref = v_ref.at[2 * h], v_ref.at[2 * h + 1]
        cand_ref[...] = jnp.full((_CAND_ROWS, tm), _CAND_PAD, F32)
        for i, (a, b) in enumerate(_PAIRS):
            cand_ref[i:i + 1, :] = v1_ref[a:a + 1, :] + v2_ref[b:b + 1, :]
        cand = cand_ref[...]
        crank, _, nc = _extract_topk(cand, PEER_TOPK, break_ties)
        chosen = crank < PEER_TOPK
        z = 2.0 * jnp.sum(jnp.where(chosen, jnp.exp(cand - cand[0:1]), 0.0), axis=0, keepdims=True)
        sel_ref[...] = jnp.where(chosen, 1.0, 0.0)
        cnts = []
        row = 0
        for a in range(PEER_TOPK):
            nb = sum(1 for (pa, _) in _PAIRS if pa == a)
            cnts.append(jnp.sum(sel_ref[row:row + nb, :], axis=0, keepdims=True))
            row += nb
        for lc in range(tm // LANES):
            lanes = slice(lc * LANES, (lc + 1) * LANES)
            rank1 = rank_ref[h, :, lanes]
            c1 = jnp.zeros(rank1.shape, F32)
            for a in range(PEER_TOPK):
                c1 = jnp.where(rank1 == a, cnts[a][:, lanes], c1)
            c1_ref[h, lanes, :] = c1.T
            e1_ref[h, lanes, :] = jnp.exp(s1_ref[:, lanes] - v1_ref[0:1, lanes]).T
            e2_ref[h, lanes, :] = (jnp.exp(s2_ref[:, lanes] - v2_ref[0:1, lanes]) / z[:, lanes]).T
        return jnp.where(nc != PEER_TOPK, 1.0, 0.0)

    tied = [halves(0, break_ties=False)]
    for h in range(1, nh):
        tied.append(halves(h, break_ties=False))
        tied[h - 1] = jnp.maximum(tied[h - 1], joint(h - 1, break_ties=False))
    tied[nh - 1] = jnp.maximum(tied[nh - 1], joint(nh - 1, break_ties=False))
    tie_ref[...] = jnp.concatenate(tied, axis=0)

    def exact(h, carry):
        @pl.when(jnp.max(tie_ref[pl.ds(h, 1), :]) > 0.0)
        def _():
            halves(h, break_ties=True)
            joint(h, break_ties=True)

        return carry

    @pl.when(jnp.max(tie_ref[...]) > 0.0)
    def _():
        lax.fori_loop(0, nh, exact, 0)


def peer_topk(qh, k1, k2):
    nh, t, qd = qh.shape
    nk = k1.shape[0]
    tm = _tile(t, (2 * LANES, LANES))
    out32 = jax.ShapeDtypeStruct((nh, t, nk), F32)
    ospec = pl.BlockSpec((nh, tm, nk), lambda i: (0, i, 0))
    return pl.pallas_call(
        _peer_topk_kernel,
        grid=(t // tm,),
        in_specs=[pl.BlockSpec((nh, tm, qd), lambda i: (0, i, 0)),
                  pl.BlockSpec((nk, qd // 2), lambda i: (0, 0)),
                  pl.BlockSpec((nk, qd // 2), lambda i: (0, 0))],
        out_specs=[ospec] * 4,
        out_shape=[out32] * 4,
        scratch_shapes=[pltpu.VMEM((2 * nh, nk, tm), F32), pltpu.VMEM((2 * nh, PEER_TOPK, tm), F32),
                        pltpu.VMEM((_CAND_ROWS, tm), F32), pltpu.VMEM((_CAND_ROWS, tm), F32),
                        pltpu.VMEM((nh, tm), F32), pltpu.VMEM((nh, nk, tm), F32)],
        compiler_params=_cparams(1, 32),
        name="peer_topk",
    )(qh, k1, k2)


def _peer_dense_kernel(h_ref, g_ref, u_ref, v_ref, e1_ref, c1_ref, e2_ref, r2_ref, o_ref,
                       xn_ref, acc_ref, gate_ref):
    ki = pl.program_id(1)
    nh, tm, nkeys = e2_ref.shape
    te = u_ref.shape[0]
    half_tm = tm // 2

    @pl.when(ki == 0)
    def _():
        xn_ref[...] = _rms(h_ref[...], g_ref[...]).astype(BF16)
        acc_ref[...] = jnp.zeros(acc_ref.shape, F32)
        level = (lax.broadcasted_iota(jnp.int32, (PEER_TOPK, nkeys), 0) + 1).astype(F32)

        def token_gate(t):
            lhs, rhs = [], []
            for h in range(nh):
                row = pl.ds(t, 1)
                lhs.append(jnp.where(c1_ref[h, row, :] == level, e1_ref[h, row, :], 0.0))
                rhs.append(jnp.where(r2_ref[h, row, :] < level, e2_ref[h, row, :], 0.0))
            lhs = jnp.concatenate(lhs, axis=0).astype(BF16)
            rhs = jnp.concatenate(rhs, axis=0).astype(BF16)
            return lax.dot_general(lhs, rhs, (((0,), (0,)), ((), ())), preferred_element_type=F32)

        def bf16_bits(g):
            return lax.bitcast_convert_type(g.astype(BF16).astype(F32), jnp.uint32)

        def token_pairs(i, carry):
            for j in range(_GATE_PAIRS_PER_TRIP):
                p = i * _GATE_PAIRS_PER_TRIP + j
                even, odd = bf16_bits(token_gate(2 * p)), bf16_bits(token_gate(2 * p + 1))
                words = (odd & jnp.uint32(0xFFFF0000)) | (even >> 16)
                gate_ref[pl.ds(pl.multiple_of(p * nkeys, nkeys), nkeys), :] = words
            return carry

        lax.fori_loop(0, half_tm // _GATE_PAIRS_PER_TRIP, token_pairs, 0)

    x = lax.dot_general(xn_ref[...], u_ref[...], _NT, preferred_element_type=F32).astype(BF16)
    act2 = x + x * jnp.tanh(x * (_GELU_C1 + _GELU_C3 * (x * x)))
    pieces = []
    for r in range(te // nkeys):
        key1 = ki * (te // nkeys) + r
        gate = pltpu.bitcast(gate_ref[pl.ds(key1, half_tm, stride=nkeys), :], BF16)
        pieces.append(gate * act2[:, r * nkeys:(r + 1) * nkeys])
    wact = jnp.concatenate(pieces, axis=1)
    acc_ref[...] += jnp.dot(wact, v_ref[...], preferred_element_type=F32)

    @pl.when(ki == pl.num_programs(1) - 1)
    def _():
        o_ref[...] = h_ref[...] + acc_ref[...]


def peer_dense(h, g, u, v, layer, e1, c1, e2, r2):
    t, d = h.shape
    ne = u.shape[1]
    nh, _, nk = e1.shape
    tm = _tile(t, (512, 256))
    te = 1024
    key_spec = pl.BlockSpec((nh, tm, nk), lambda i, k: (0, i, 0))
    return pl.pallas_call(
        _peer_dense_kernel,
        grid=(t // tm, ne // te),
        in_specs=[pl.BlockSpec((tm, d), lambda i, k: (i, 0)),
                  pl.BlockSpec((1, d), lambda i, k: (0, 0)),
                  pl.BlockSpec((None, te, d), lambda i, k: (layer, k, 0)),
                  pl.BlockSpec((None, te, d), lambda i, k: (layer, k, 0)),
                  key_spec, key_spec, key_spec, key_spec],
        out_specs=pl.BlockSpec((tm, d), lambda i, k: (i, 0)),
        out_shape=jax.ShapeDtypeStruct((t, d), F32),
        scratch_shapes=[pltpu.VMEM((tm, d), BF16), pltpu.VMEM((tm, d), F32),
                        pltpu.VMEM((nk * tm // 2, nk), jnp.uint32)],
        compiler_params=_cparams(2, 56),
        name="peer_dense",
    )(h, g.reshape(1, d), u, v, e1, c1, e2, r2)


def kernel(x_prompt, x_sample, state_ssm, state_conv, cache_k_win, cache_v_win, norm_mix, norm_ffn, norm_kv, norm_final, m_w_in, m_conv_w, m_conv_b, m_dt_bias, m_a_log, m_d_skip, m_norm, m_w_out, a_w_kv, a_b_kv, a_w_q, a_b_q, a_sinks, a_w_o, a_b_o, p_w_q, p_sub_k1, p_sub_k2, p_u, p_v):
    depth = norm_mix.shape[0]
    n_a = m_w_in.shape[0]
    d = x_prompt.shape[-1]
    inner = m_w_out.shape[1]
    conv_dim = m_conv_w.shape[2]
    heads = m_d_skip.shape[1]
    kvd = a_w_kv.shape[1] // 2
    qdim = p_w_q.shape[2] // PEER_HEADS
    dt_col = inner + conv_dim
    in_pad = dt_col + LANES

    w_in = jnp.pad(m_w_in, ((0, 0), (0, 0), (0, in_pad - m_w_in.shape[2]))).astype(BF16)
    w_out = m_w_out.astype(BF16)
    w_kv = a_w_kv.astype(BF16)
    w_q = a_w_q.astype(BF16)
    w_o = a_w_o.astype(BF16)
    pw_q = p_w_q.astype(BF16)
    k1 = p_sub_k1.astype(BF16)
    k2 = p_sub_k2.astype(BF16)
    u_tab = p_u.astype(BF16)
    v_tab = p_v.astype(BF16)
    zeros_in = jnp.zeros((in_pad,), F32)
    zeros_pq = jnp.zeros((p_w_q.shape[2],), F32)
    in_tn = _tile(in_pad, (896, 128))

    def run(x, ssm_init, conv_init, k_prev, v_prev):
        bsz, seq, _ = x.shape
        h = x.reshape(bsz * seq, d)
        ssm_out, conv_out = [], []
        k_src = v_src = None
        for l in range(depth):
            if l < n_a:
                proj = norm_matmul(h, norm_mix[l], w_in[l], zeros_in, in_tn)
                if conv_init is None:
                    prev8 = jnp.zeros((bsz, SUBLANES, conv_dim), F32)
                else:
                    prev8 = jnp.pad(conv_init[l], ((0, 0), (SUBLANES - (CONV_WIDTH - 1), 0), (0, 0)))
                xbc = causal_conv_silu(proj, prev8, m_conv_w[l], m_conv_b[l], bsz, seq, inner)
                s0 = None if ssm_init is None else ssm_init[l].reshape(bsz, inner, SSM_STATE)
                y, s_new = ssd_scan(xbc, proj, dt_col, m_dt_bias[l], m_a_log[l], m_d_skip[l], s0, bsz, seq)
                ssm_out.append(s_new.reshape(bsz, heads, SSM_HEAD_DIM, SSM_STATE))
                tail = proj.reshape(bsz, seq, in_pad)[:, seq - (CONV_WIDTH - 1):, inner:dt_col]
                conv_out.append(tail)
                h = gated_norm_out(y, proj, m_norm[l], w_out[l], h)
            else:
                j = l - n_a
                if j == 0:
                    kv = norm_matmul(h, norm_kv, w_kv, a_b_kv, 2 * kvd)
                    kv3 = kv.reshape(bsz, seq, 2 * kvd)
                    if k_prev is not None:
                        k_all = jnp.concatenate([k_prev.reshape(bsz, WINDOW, kvd), kv3[:, :, :kvd]], axis=1)
                        v_all = jnp.concatenate([v_prev.reshape(bsz, WINDOW, kvd), kv3[:, :, kvd:]], axis=1)
                        k_src = k_all.reshape(bsz * (WINDOW + seq), kvd)
                        v_src = v_all.reshape(bsz * (WINDOW + seq), kvd)
                        k_win, v_win = k_all[:, -WINDOW:], v_all[:, -WINDOW:]
                    else:
                        k_win, v_win = kv3[:, -WINDOW:, :kvd], kv3[:, -WINDOW:, kvd:]
                qp = norm_matmul(h, norm_mix[l], w_q[j], a_b_q[j], w_q.shape[2])
                if k_prev is None:
                    o = swa_attention(qp, a_sinks[j], kv, kv, 0, 1, bsz, seq, None)
                else:
                    o = swa_attention(qp, a_sinks[j], k_src, v_src, 0, 0, bsz, seq, WINDOW + seq)
                h = matmul_bias_res(o, w_o[j], a_b_o[j], h)
            qh = norm_matmul(h, norm_ffn[l], pw_q[l], zeros_pq, pw_q.shape[2] // 2, head_dim=qdim)
            e1, c1, e2, r2 = peer_topk(qh, k1[l], k2[l])
            h = peer_dense(h, norm_ffn[l], u_tab, v_tab, l, e1, c1, e2, r2)
        y = rmsnorm_rows(h, norm_final).reshape(bsz, seq, d)
        kshape = (bsz, WINDOW, KV_HEADS, ATTN_HEAD_DIM)
        return y, jnp.stack(ssm_out), jnp.stack(conv_out), k_win.reshape(kshape), v_win.reshape(kshape)

    y_p, p_ssm, p_conv, p_k, p_v_win = run(x_prompt, None, None, None, None)
    y_s, s_ssm, s_conv, s_k, s_v = run(x_sample, state_ssm, state_conv, cache_k_win, cache_v_win)
    return (y_p, y_s, p_ssm, p_conv, p_k, p_v_win, s_ssm, s_conv, s_k, s_v)
```

```python
import functools

import jax
import jax.numpy as jnp
from jax import lax
from jax.experimental import pallas as pl
from jax.experimental.pallas import tpu as pltpu

F32 = jnp.float32
BF16 = jnp.bfloat16
EPS = 1e-6

WINDOW = 128
SSM_HEAD_DIM = 64
SSM_STATE = 128
SSM_GROUPS = 8
CONV_WIDTH = 4
SSD_BLOCK = 128
ATTN_CHUNK = 64
ATTN_HEAD_DIM = 64
KV_HEADS = 4
PEER_HEADS = 8
PEER_NKEYS = 128
PEER_TOPK = 16
LANES = 128
SUBLANES = 8
MIB = 1024 * 1024

_NT = (((1,), (1,)), ((), ()))
_GELU_C1 = 0.7978845608028654
_GELU_C3 = _GELU_C1 * 0.044715
_GATE_PAIRS_PER_TRIP = 16

ROW_TILES = (1024, 512, 256)
PEER_ROW_TILES = (512, 256)
PEER_EXPERT_TILE = 1024
CONV_ROW_TILES = (256, 128, 64, 32, 16)
CONV_COL_TILE = 1024


def _cparams(n_axes, vmem_mib):
    return pltpu.CompilerParams(
        dimension_semantics=("arbitrary",) * n_axes, vmem_limit_bytes=vmem_mib * MIB)


def _tile(n, prefs):
    for p in prefs:
        if n % p == 0:
            return p
    return n


def _rms(x, g):
    ms = jnp.mean(x * x, axis=-1, keepdims=True)
    return x * lax.rsqrt(ms + EPS) * g


def _norm_matmul_kernel(x_ref, g_ref, w_ref, b_ref, o_ref, xn_ref, *, head_dim):
    @pl.when(pl.program_id(1) == 0)
    def _():
        xn_ref[...] = _rms(x_ref[...], g_ref[...]).astype(BF16)

    res = jnp.dot(xn_ref[...], w_ref[...], preferred_element_type=F32) + b_ref[...]
    if head_dim is None:
        o_ref[...] = res
    else:
        for k in range(o_ref.shape[0]):
            o_ref[k] = res[:, k * head_dim:(k + 1) * head_dim].astype(o_ref.dtype)


def norm_matmul(x, g, w, b, tn, head_dim=None):
    t, d = x.shape
    n = w.shape[1]
    tm = _tile(t, ROW_TILES)
    if head_dim is not None:
        out_shape = jax.ShapeDtypeStruct((n // head_dim, t, head_dim), BF16)
        out_spec = pl.BlockSpec((tn // head_dim, tm, head_dim), lambda i, j: (j, i, 0))
    else:
        out_shape = jax.ShapeDtypeStruct((t, n), F32)
        out_spec = pl.BlockSpec((tm, tn), lambda i, j: (i, j))
    return pl.pallas_call(
        functools.partial(_norm_matmul_kernel, head_dim=head_dim),
        grid=(t // tm, n // tn),
        in_specs=[pl.BlockSpec((tm, d), lambda i, j: (i, 0)),
                  pl.BlockSpec((1, d), lambda i, j: (0, 0)),
                  pl.BlockSpec((d, tn), lambda i, j: (0, j)),
                  pl.BlockSpec((1, tn), lambda i, j: (0, j))],
        out_specs=out_spec,
        out_shape=out_shape,
        scratch_shapes=[pltpu.VMEM((tm, d), BF16)],
        compiler_params=_cparams(2, 40),
        name="norm_matmul",
    )(x, g.reshape(1, d), w, b.reshape(1, n))


def _matmul_res_kernel(x_ref, w_ref, b_ref, r_ref, o_ref):
    mix = jnp.dot(x_ref[...].astype(BF16), w_ref[...], preferred_element_type=F32) + b_ref[...]
    o_ref[...] = r_ref[...] + mix


def matmul_bias_res(x, w, b, res):
    t, k = x.shape
    n = w.shape[1]
    tm = _tile(t, ROW_TILES[1:])
    return pl.pallas_call(
        _matmul_res_kernel,
        grid=(t // tm,),
        in_specs=[pl.BlockSpec((tm, k), lambda i: (i, 0)),
                  pl.BlockSpec((k, n), lambda i: (0, 0)),
                  pl.BlockSpec((1, n), lambda i: (0, 0)),
                  pl.BlockSpec((tm, n), lambda i: (i, 0))],
        out_specs=pl.BlockSpec((tm, n), lambda i: (i, 0)),
        out_shape=jax.ShapeDtypeStruct((t, n), F32),
        compiler_params=_cparams(1, 32),
        name="matmul_bias_res",
    )(x, w, b.reshape(1, n), res)


def _rmsnorm_kernel(x_ref, g_ref, o_ref):
    o_ref[...] = _rms(x_ref[...], g_ref[...])


def rmsnorm_rows(x, g):
    t, d = x.shape
    tm = _tile(t, ROW_TILES[1:])
    return pl.pallas_call(
        _rmsnorm_kernel,
        grid=(t // tm,),
        in_specs=[pl.BlockSpec((tm, d), lambda i: (i, 0)), pl.BlockSpec((1, d), lambda i: (0, 0))],
        out_specs=pl.BlockSpec((tm, d), lambda i: (i, 0)),
        out_shape=jax.ShapeDtypeStruct((t, d), F32),
        compiler_params=_cparams(1, 32),
        name="rmsnorm",
    )(x, g.reshape(1, d))


def _conv_kernel(x_ref, halo_ref, prev_ref, w_ref, b_ref, o_ref):
    first = pl.program_id(1) == 0
    x = x_ref[...]
    ext = jnp.concatenate([jnp.where(first, prev_ref[...], halo_ref[...]), x], axis=0)
    acc = b_ref[...]
    for k in range(CONV_WIDTH - 1):
        tap = pltpu.roll(ext, CONV_WIDTH - 1 - k, 0)[SUBLANES:]
        acc = acc + tap * w_ref[k:k + 1, :]
    acc = acc + x * w_ref[CONV_WIDTH - 1:CONV_WIDTH, :]
    o_ref[...] = acc * jax.nn.sigmoid(acc)


def causal_conv_silu(proj, prev8, w, b, bsz, seq, col0):
    c = w.shape[1]
    tl = _tile(seq, CONV_ROW_TILES)
    tc = CONV_COL_TILE
    nl = seq // tl
    cb0 = col0 // tc
    hpb = tl // SUBLANES
    return pl.pallas_call(
        _conv_kernel,
        grid=(bsz, nl, c // tc),
        in_specs=[pl.BlockSpec((tl, tc), lambda bi, li, ci: (bi * nl + li, cb0 + ci)),
                  pl.BlockSpec((SUBLANES, tc),
                               lambda bi, li, ci: (jnp.maximum((bi * nl + li) * hpb - 1, 0), cb0 + ci)),
                  pl.BlockSpec((None, SUBLANES, tc), lambda bi, li, ci: (bi, 0, ci)),
                  pl.BlockSpec((CONV_WIDTH, tc), lambda bi, li, ci: (0, ci)),
                  pl.BlockSpec((1, tc), lambda bi, li, ci: (0, ci))],
        out_specs=pl.BlockSpec((tl, tc), lambda bi, li, ci: (bi * nl + li, ci)),
        out_shape=jax.ShapeDtypeStruct((bsz * seq, c), F32),
        compiler_params=_cparams(3, 32),
        name="conv_silu",
    )(proj, proj, prev8, w, b.reshape(1, c))


def _split3(x):
    hi = x.astype(BF16)
    r1 = x - hi.astype(F32)
    mid = r1.astype(BF16)
    lo = (r1 - mid.astype(F32)).astype(BF16)
    return hi, mid, lo


def _ssd_kernel(*refs, rows, has_init):
    if has_init:
        (x_ref, b_ref, c_ref, dt_ref, bias_ref, alog_ref, dskip_ref, echan_ref, ecol_ref, s0_ref,
         y_ref, sout_ref, st_ref) = refs
    else:
        (x_ref, b_ref, c_ref, dt_ref, bias_ref, alog_ref, dskip_ref, echan_ref, ecol_ref,
         y_ref, sout_ref, st_ref) = refs
    q = SSD_BLOCK
    gw = x_ref.shape[1] // SSM_GROUPS
    hpg = gw // SSM_HEAD_DIM
    ci = pl.program_id(1)

    @pl.when(ci == 0)
    def _():
        for g in range(SSM_GROUPS):
            if has_init:
                st_ref[g] = s0_ref[g * gw:(g + 1) * gw, :].T
            else:
                st_ref[g] = jnp.zeros((SSM_STATE, gw), F32)

    def pad(a):
        if rows == q:
            return a
        return jnp.concatenate([a, jnp.zeros((q - rows, a.shape[1]), a.dtype)], axis=0)

    raw = dt_ref[...] + bias_ref[...]
    dt = pad(jnp.maximum(raw, 0.0) + jnp.log1p(jnp.exp(-jnp.abs(raw))))
    a_neg = -jnp.exp(alog_ref[...])
    dta = dt * a_neg
    ri = lax.broadcasted_iota(jnp.int32, (q, q), 0)
    cj = lax.broadcasted_iota(jnp.int32, (q, q), 1)
    tri = ri >= cj
    trib = jnp.where(tri, 1.0, 0.0).astype(BF16)
    acum = sum(jnp.dot(trib, part, preferred_element_type=F32) for part in _split3(dta))
    acum_t = acum.T
    a_last = acum[q - 1:q, :]
    dec_end = jnp.exp(a_last - acum)
    ea = jnp.exp(acum)
    wgt = dt * dec_end

    two = lambda a: jnp.concatenate(_split3(a)[:2], axis=1)
    chan = jnp.dot(jnp.concatenate([two(dt), two(wgt), two(ea)], axis=0), echan_ref[...],
                   preferred_element_type=F32)
    dt_e, wgt_e, ea_e = chan[0:q], chan[q:2 * q], chan[2 * q:3 * q]
    acum_cols = jnp.dot(jnp.concatenate(_split3(acum), axis=1), ecol_ref[...],
                        preferred_element_type=F32)

    xs = pad(x_ref[...])
    bm = pad(b_ref[...])
    cm = pad(c_ref[...])

    for g in range(SSM_GROUPS):
        gsl = slice(g * gw, (g + 1) * gw)
        bg = bm[:, g * SSM_STATE:(g + 1) * SSM_STATE]
        cgb = cm[:, g * SSM_STATE:(g + 1) * SSM_STATE].astype(BF16)
        cb = lax.dot_general(cgb, bg.astype(BF16), _NT, preferred_element_type=F32)
        xg = xs[:, gsl]
        xdt = (xg * dt_e[:, gsl]).astype(BF16)
        xw = (xg * wgt_e[:, gsl]).astype(BF16)
        st_prev = st_ref[g]
        y_off = jnp.dot(cgb, st_prev.astype(BF16), preferred_element_type=F32) * ea_e[:, gsl]
        st_new = jnp.dot(bg.T.astype(BF16), xw, preferred_element_type=F32)
        st_ref[g] = ea_e[q - 1:q, gsl] * st_prev + st_new
        yd = []
        for r in range(hpg):
            h = g * hpg + r
            seg = acum_cols[:, h * q:(h + 1) * q] - acum_t[h:h + 1, :]
            lmat = jnp.where(tri, jnp.exp(seg), 0.0)
            m = (cb * lmat).astype(BF16)
            yd.append(jnp.dot(m, xdt[:, r * SSM_HEAD_DIM:(r + 1) * SSM_HEAD_DIM],
                              preferred_element_type=F32))
        yg = jnp.concatenate(yd, axis=1) + y_off + xg * dskip_ref[:, g * gw:(g + 1) * gw]
        y_ref[:, g * gw:(g + 1) * gw] = yg[:rows]

    @pl.when(ci == pl.num_programs(1) - 1)
    def _():
        for g in range(SSM_GROUPS):
            sout_ref[g * gw:(g + 1) * gw, :] = st_ref[g].T


def ssd_scan(xbc, proj, dt_col, dt_bias, a_log, d_skip, s0, bsz, seq):
    inner = d_skip.shape[0] * SSM_HEAD_DIM
    gn = SSM_GROUPS * SSM_STATE
    rows = min(SSD_BLOCK, seq)
    nc = seq // rows
    heads = d_skip.shape[0]
    pad128 = lambda v: jnp.pad(v, (0, LANES - heads)).reshape(1, LANES)
    row_map = lambda bi, ci: bi * nc + ci
    in_specs = [pl.BlockSpec((rows, inner), lambda bi, ci: (row_map(bi, ci), 0)),
                pl.BlockSpec((rows, gn), lambda bi, ci: (row_map(bi, ci), inner // gn)),
                pl.BlockSpec((rows, gn), lambda bi, ci: (row_map(bi, ci), inner // gn + 1)),
                pl.BlockSpec((rows, LANES), lambda bi, ci: (row_map(bi, ci), dt_col // LANES)),
                pl.BlockSpec((1, LANES), lambda bi, ci: (0, 0)),
                pl.BlockSpec((1, LANES), lambda bi, ci: (0, 0)),
                pl.BlockSpec((1, inner), lambda bi, ci: (0, 0)),
                pl.BlockSpec((2 * LANES, inner), lambda bi, ci: (0, 0)),
                pl.BlockSpec((3 * LANES, heads * SSD_BLOCK), lambda bi, ci: (0, 0))]
    head_of = lambda width: (jnp.arange(heads * width) // width)[None, :] == jnp.arange(LANES)[:, None]
    echan = jnp.tile(head_of(SSM_HEAD_DIM).astype(BF16), (2, 1))
    ecol = jnp.tile(head_of(SSD_BLOCK).astype(BF16), (3, 1))
    args = [xbc, xbc, xbc, proj, pad128(dt_bias), pad128(a_log),
            jnp.repeat(d_skip, SSM_HEAD_DIM).reshape(1, inner), echan, ecol]
    if s0 is not None:
        in_specs.append(pl.BlockSpec((None, inner, SSM_STATE), lambda bi, ci: (bi, 0, 0)))
        args.append(s0)
    return pl.pallas_call(
        functools.partial(_ssd_kernel, rows=rows, has_init=s0 is not None),
        grid=(bsz, nc),
        in_specs=in_specs,
        out_specs=[pl.BlockSpec((rows, inner), lambda bi, ci: (row_map(bi, ci), 0)),
                   pl.BlockSpec((None, inner, SSM_STATE), lambda bi, ci: (bi, 0, 0))],
        out_shape=[jax.ShapeDtypeStruct((bsz * seq, inner), F32),
                   jax.ShapeDtypeStruct((bsz, inner, SSM_STATE), F32)],
        scratch_shapes=[pltpu.VMEM((SSM_GROUPS, SSM_STATE, inner // SSM_GROUPS), F32)],
        compiler_params=_cparams(2, 48),
        name="ssd_scan",
    )(*args)


def _gated_out_kernel(y_ref, z_ref, g_ref, w_ref, r_ref, o_ref):
    z = z_ref[...]
    y = y_ref[...] * (z * jax.nn.sigmoid(z))
    yn = _rms(y, g_ref[...]).astype(BF16)
    o_ref[...] = r_ref[...] + jnp.dot(yn, w_ref[...], preferred_element_type=F32)


def gated_norm_out(y, proj, g, w, res):
    t, inner = y.shape
    n = w.shape[1]
    tm = _tile(t, (256, 128))
    return pl.pallas_call(
        _gated_out_kernel,
        grid=(t // tm,),
        in_specs=[pl.BlockSpec((tm, inner), lambda i: (i, 0)),
                  pl.BlockSpec((tm, inner), lambda i: (i, 0)),
                  pl.BlockSpec((1, inner), lambda i: (0, 0)),
                  pl.BlockSpec((inner, n), lambda i: (0, 0)),
                  pl.BlockSpec((tm, n), lambda i: (i, 0))],
        out_specs=pl.BlockSpec((tm, n), lambda i: (i, 0)),
        out_shape=jax.ShapeDtypeStruct((t, n), F32),
        compiler_params=_cparams(1, 40),
        name="gated_norm_out",
    )(y, proj, g.reshape(1, inner), w, res)


def _attn_banded_kernel(q_ref, sink_ref, kp_ref, kc_ref, vp_ref, vc_ref, o_ref):
    tq = q_ref.shape[0]
    hd = ATTN_HEAD_DIM
    qpk = q_ref.shape[1] // hd // KV_HEADS
    scale = hd ** -0.5
    nk = WINDOW + tq
    back = WINDOW // ATTN_CHUNK
    kc = lax.broadcasted_iota(jnp.int32, (nk, tq), 0) // ATTN_CHUNK
    qc = lax.broadcasted_iota(jnp.int32, (nk, tq), 1) // ATTN_CHUNK
    first_kc = jnp.where(pl.program_id(1) == 0, back, 0)
    valid = (kc >= qc) & (kc <= qc + back) & (kc >= first_kc)
    bias = jnp.concatenate([jnp.where(valid, 0.0, -jnp.inf)] * qpk, axis=1)
    for pair in range(KV_HEADS // 2):
        cols = slice(pair * 2 * hd, (pair + 1) * 2 * hd)
        k2 = jnp.concatenate([kp_ref[:, cols], kc_ref[:, cols]], axis=0)
        v2t = jnp.concatenate([vp_ref[:, cols], vc_ref[:, cols]], axis=0).T
        for sub in range(2):
            kh = pair * 2 + sub
            heads = range(kh * qpk, (kh + 1) * qpk)
            kmat = k2[:, sub * hd:(sub + 1) * hd].astype(BF16)
            vt = v2t[sub * hd:(sub + 1) * hd, :].astype(BF16)
            q4 = jnp.concatenate([q_ref[:, h * hd:(h + 1) * hd] for h in heads], axis=0).astype(BF16)
            s = lax.dot_general(kmat, q4, _NT, preferred_element_type=F32) * scale + bias
            sink = jnp.concatenate([jnp.broadcast_to(sink_ref[:, h:h + 1], (1, tq)) for h in heads], axis=1)
            m = jnp.maximum(jnp.max(s, axis=0, keepdims=True), sink)
            p = jnp.exp(s - m)
            den = jnp.sum(p, axis=0, keepdims=True) + jnp.exp(sink - m)
            ot = jnp.dot(vt, (p * (1.0 / den)).astype(BF16), preferred_element_type=F32)
            for g in range(0, qpk, 2):
                two = jnp.concatenate([ot[:, g * tq:(g + 1) * tq], ot[:, (g + 1) * tq:(g + 2) * tq]], axis=0)
                o_ref[:, (kh * qpk + g) * hd:(kh * qpk + g + 2) * hd] = two.T


def _attn_rows_kernel(q_ref, sink_ref, k_ref, v_ref, o_ref):
    tq = q_ref.shape[0]
    hd = ATTN_HEAD_DIM
    qpk = q_ref.shape[1] // hd // KV_HEADS
    scale = hd ** -0.5
    outs = []
    for kh in range(KV_HEADS):
        kmat = k_ref[:, kh * hd:(kh + 1) * hd].astype(BF16)
        vmat = v_ref[:, kh * hd:(kh + 1) * hd].astype(BF16)
        heads = range(kh * qpk, (kh + 1) * qpk)
        q4 = jnp.concatenate([q_ref[:, h * hd:(h + 1) * hd] for h in heads], axis=0).astype(BF16)
        s = lax.dot_general(q4, kmat, _NT, preferred_element_type=F32) * scale
        sink = jnp.concatenate([jnp.broadcast_to(sink_ref[:, h:h + 1], (tq, 1)) for h in heads], axis=0)
        m = jnp.maximum(jnp.max(s, axis=-1, keepdims=True), sink)
        p = jnp.exp(s - m)
        den = jnp.sum(p, axis=-1, keepdims=True) + jnp.exp(sink - m)
        o4 = jnp.dot((p / den).astype(BF16), vmat, preferred_element_type=F32)
        outs += [o4[g * tq:(g + 1) * tq] for g in range(qpk)]
    o_ref[...] = jnp.concatenate(outs, axis=1)


def swa_attention(q, sinks, k_src, v_src, k_col, v_col, bsz, seq, kv_rows):
    t, hqd = q.shape
    kvd = KV_HEADS * ATTN_HEAD_DIM
    n_heads = sinks.shape[0]
    if kv_rows is None:
        body = _attn_banded_kernel
        tq = _tile(seq, (256, WINDOW))
        nq = seq // tq
        per = tq // WINDOW
        kv_specs = lambda col: [
            pl.BlockSpec((WINDOW, kvd), lambda bi, ci: ((bi * nq + ci) * per - jnp.minimum(ci, 1), col)),
            pl.BlockSpec((tq, kvd), lambda bi, ci: (bi * nq + ci, col))]
    else:
        body = _attn_rows_kernel
        tq, nq = seq, 1
        kv_specs = lambda col: [pl.BlockSpec((kv_rows, kvd), lambda bi, ci: (bi, col))]
    k_specs, v_specs = kv_specs(k_col), kv_specs(v_col)
    in_specs = ([pl.BlockSpec((tq, hqd), lambda bi, ci: (bi * nq + ci, 0)),
                 pl.BlockSpec((1, LANES), lambda bi, ci: (0, 0))] + k_specs + v_specs)
    return pl.pallas_call(
        body,
        grid=(bsz, nq),
        in_specs=in_specs,
        out_specs=pl.BlockSpec((tq, hqd), lambda bi, ci: (bi * nq + ci, 0)),
        out_shape=jax.ShapeDtypeStruct((t, hqd), F32),
        compiler_params=_cparams(2, 40),
        name="swa_attention",
    )(q, jnp.pad(sinks, (0, LANES - n_heads)).reshape(1, LANES),
      *([k_src] * len(k_specs)), *([v_src] * len(v_specs)))


_PAIRS = [(a, b) for a in range(PEER_TOPK) for b in range(PEER_TOPK) if (a + 1) * (b + 1) <= PEER_TOPK]
_CAND_ROWS = -(-len(_PAIRS) // SUBLANES) * SUBLANES
_MARK_BASE = -(2.0 ** 127)
_MARK_STEP = 2.0 ** 115
_MARK_CUT = -1.5 * 2.0 ** 126
_CAND_PAD = -(2.0 ** 126)


def _extract_topk(s, n_rounds, break_ties):
    n = s.shape[0]
    iota = lax.broadcasted_iota(jnp.int32, s.shape, 0)
    vals = []
    for r in range(n_rounds):
        m = jnp.max(s, axis=0, keepdims=True)
        sel = s == m
        if break_ties:
            sel = iota == jnp.min(jnp.where(sel, iota, n), axis=0, keepdims=True)
        s = jnp.where(sel, _MARK_BASE + r * _MARK_STEP, s)
        vals.append(m)
    extracted = s < _MARK_CUT
    rank = jnp.where(extracted, (s - _MARK_BASE) * (1.0 / _MARK_STEP), float(n_rounds))
    count = jnp.sum(jnp.where(extracted, 1.0, 0.0), axis=0, keepdims=True)
    return rank, vals, count


def _peer_topk_kernel(q_ref, k1_ref, k2_ref, e1_ref, c1_ref, e2_ref, r2_ref,
                      s_ref, v_ref, cand_ref, sel_ref, tie_ref, rank_ref):
    half = k1_ref.shape[1]
    nh, tm = q_ref.shape[0], q_ref.shape[1]

    for h in range(nh):
        qh = q_ref[h]
        s_ref[2 * h] = lax.dot_general(k1_ref[...], qh[:, :half], _NT, preferred_element_type=F32)
        s_ref[2 * h + 1] = lax.dot_general(k2_ref[...], qh[:, half:], _NT, preferred_element_type=F32)

    def halves(h, break_ties):
        s1_ref, s2_ref = s_ref.at[2 * h], s_ref.at[2 * h + 1]
        tied = []
        for lc in range(tm // LANES):
            lanes = slice(lc * LANES, (lc + 1) * LANES)
            rank1, v1, n1 = _extract_topk(s1_ref[:, lanes], PEER_TOPK, break_ties)
            rank2, v2, n2 = _extract_topk(s2_ref[:, lanes], PEER_TOPK, break_ties)
            rank_ref[h, :, lanes] = rank1
            r2_ref[h, lanes, :] = rank2.T
            for r in range(PEER_TOPK):
                v_ref[2 * h, r:r + 1, lanes] = v1[r]
                v_ref[2 * h + 1, r:r + 1, lanes] = v2[r]
            tied.append(jnp.where((n1 != PEER_TOPK) | (n2 != PEER_TOPK), 1.0, 0.0))
        return jnp.concatenate(tied, axis=1)

    def joint(h, break_ties):
        s1_ref, s2_ref = s_ref.at[2 * h], s_ref.at[2 * h + 1]
        v1_ref, v2_ref = v_ref.at[2 * h], v_ref.at[2 * h + 1]
        cand_ref[...] = jnp.full((_CAND_ROWS, tm), _CAND_PAD, F32)
        for i, (a, b) in enumerate(_PAIRS):
            cand_ref[i:i + 1, :] = v1_ref[a:a + 1, :] + v2_ref[b:b + 1, :]
        cand = cand_ref[...]
        crank, _, nc = _extract_topk(cand, PEER_TOPK, break_ties)
        chosen = crank < PEER_TOPK
        z = 2.0 * jnp.sum(jnp.where(chosen, jnp.exp(cand - cand[0:1]), 0.0), axis=0, keepdims=True)
        sel_ref[...] = jnp.where(chosen, 1.0, 0.0)
        cnts = []
        row = 0
        for a in range(PEER_TOPK):
            nb = sum(1 for (pa, _) in _PAIRS if pa == a)
            cnts.append(jnp.sum(sel_ref[row:row + nb, :], axis=0, keepdims=True))
            row += nb
        for lc in range(tm // LANES):
            lanes = slice(lc * LANES, (lc + 1) * LANES)
            rank1 = rank_ref[h, :, lanes]
            c1 = jnp.zeros(rank1.shape, F32)
            for a in range(PEER_TOPK):
                c1 = jnp.where(rank1 == a, cnts[a][:, lanes], c1)
            c1_ref[h, lanes, :] = c1.T
            e1_ref[h, lanes, :] = jnp.exp(s1_ref[:, lanes] - v1_ref[0:1, lanes]).T
            e2_ref[h, lanes, :] = (jnp.exp(s2_ref[:, lanes] - v2_ref[0:1, lanes]) / z[:, lanes]).T
        return jnp.where(nc != PEER_TOPK, 1.0, 0.0)

    tied = [halves(0, break_ties=False)]
    for h in range(1, nh):
        tied.append(halves(h, break_ties=False))
        tied[h - 1] = jnp.maximum(tied[h - 1], joint(h - 1, break_ties=False))
    tied[nh - 1] = jnp.maximum(tied[nh - 1], joint(nh - 1, break_ties=False))
    tie_ref[...] = jnp.concatenate(tied, axis=0)

    def exact(h, carry):
        @pl.when(jnp.max(tie_ref[pl.ds(h, 1), :]) > 0.0)
        def _():
            halves(h, break_ties=True)
            joint(h, break_ties=True)

        return carry

    @pl.when(jnp.max(tie_ref[...]) > 0.0)
    def _():
        lax.fori_loop(0, nh, exact, 0)


def peer_topk(qh, k1, k2):
    nh, t, qd = qh.shape
    nk = k1.shape[0]
    tm = _tile(t, (2 * LANES, LANES))
    out32 = jax.ShapeDtypeStruct((nh, t, nk), F32)
    ospec = pl.BlockSpec((nh, tm, nk), lambda i: (0, i, 0))
    return pl.pallas_call(
        _peer_topk_kernel,
        grid=(t // tm,),
        in_specs=[pl.BlockSpec((nh, tm, qd), lambda i: (0, i, 0)),
                  pl.BlockSpec((nk, qd // 2), lambda i: (0, 0)),
                  pl.BlockSpec((nk, qd // 2), lambda i: (0, 0))],
        out_specs=[ospec] * 4,
        out_shape=[out32] * 4,
        scratch_shapes=[pltpu.VMEM((2 * nh, nk, tm), F32), pltpu.VMEM((2 * nh, PEER_TOPK, tm), F32),
                        pltpu.VMEM((_CAND_ROWS, tm), F32), pltpu.VMEM((_CAND_ROWS, tm), F32),
                        pltpu.VMEM((nh, tm), F32), pltpu.VMEM((nh, nk, tm), F32)],
        compiler_params=_cparams(1, 32),
        name="peer_topk",
    )(qh, k1, k2)


def _peer_dense_kernel(h_ref, g_ref, u_ref, v_ref, e1_ref, c1_ref, e2_ref, r2_ref, o_ref,
                       xn_ref, acc_ref, gate_ref):
    ki = pl.program_id(1)
    nh, tm, nkeys = e2_ref.shape
    te = u_ref.shape[0]
    half_tm = tm // 2

    @pl.when(ki == 0)
    def _():
        xn_ref[...] = _rms(h_ref[...], g_ref[...]).astype(BF16)
        acc_ref[...] = jnp.zeros(acc_ref.shape, F32)
        level = (lax.broadcasted_iota(jnp.int32, (PEER_TOPK, nkeys), 0) + 1).astype(F32)

        def token_gate(t):
            lhs, rhs = [], []
            for h in range(nh):
                row = pl.ds(t, 1)
                lhs.append(jnp.where(c1_ref[h, row, :] == level, e1_ref[h, row, :], 0.0))
                rhs.append(jnp.where(r2_ref[h, row, :] < level, e2_ref[h, row, :], 0.0))
            lhs = jnp.concatenate(lhs, axis=0).astype(BF16)
            rhs = jnp.concatenate(rhs, axis=0).astype(BF16)
            return lax.dot_general(lhs, rhs, (((0,), (0,)), ((), ())), preferred_element_type=F32)

        def bf16_bits(g):
            return lax.bitcast_convert_type(g.astype(BF16).astype(F32), jnp.uint32)

        def token_pairs(i, carry):
            for j in range(_GATE_PAIRS_PER_TRIP):
                p = i * _GATE_PAIRS_PER_TRIP + j
                even, odd = bf16_bits(token_gate(2 * p)), bf16_bits(token_gate(2 * p + 1))
                words = (odd & jnp.uint32(0xFFFF0000)) | (even >> 16)
                gate_ref[p] = words
            return carry

        lax.fori_loop(0, half_tm // _GATE_PAIRS_PER_TRIP, token_pairs, 0)

    x = lax.dot_general(xn_ref[...], u_ref[...], _NT, preferred_element_type=F32).astype(BF16)
    act2 = x + x * jnp.tanh(x * (_GELU_C1 + _GELU_C3 * (x * x)))
    pieces = []
    for r in range(te // nkeys):
        key1 = ki * (te // nkeys) + r
        gate = pltpu.bitcast(gate_ref[:, key1, :], BF16)
        pieces.append(gate * act2[:, r * nkeys:(r + 1) * nkeys])
    wact = jnp.concatenate(pieces, axis=1)
    acc_ref[...] += jnp.dot(wact, v_ref[...], preferred_element_type=F32)

    @pl.when(ki == pl.num_programs(1) - 1)
    def _():
        o_ref[...] = h_ref[...] + acc_ref[...]


def peer_dense(h, g, u, v, layer, e1, c1, e2, r2):
    t, d = h.shape
    ne = u.shape[1]
    nh, _, nk = e1.shape
    tm = _tile(t, PEER_ROW_TILES)
    te = PEER_EXPERT_TILE
    key_spec = pl.BlockSpec((nh, tm, nk), lambda i, k: (0, i, 0))
    return pl.pallas_call(
        _peer_dense_kernel,
        grid=(t // tm, ne // te),
        in_specs=[pl.BlockSpec((tm, d), lambda i, k: (i, 0)),
                  pl.BlockSpec((1, d), lambda i, k: (0, 0)),
                  pl.BlockSpec((None, te, d), lambda i, k: (layer, k, 0)),
                  pl.BlockSpec((None, te, d), lambda i, k: (layer, k, 0)),
                  key_spec, key_spec, key_spec, key_spec],
        out_specs=pl.BlockSpec((tm, d), lambda i, k: (i, 0)),
        out_shape=jax.ShapeDtypeStruct((t, d), F32),
        scratch_shapes=[pltpu.VMEM((tm, d), BF16), pltpu.VMEM((tm, d), F32),
                        pltpu.VMEM((tm // 2, nk, nk), jnp.uint32)],
        compiler_params=_cparams(2, 56),
        name="peer_dense",
    )(h, g.reshape(1, d), u, v, e1, c1, e2, r2)


def kernel(x_prompt, x_sample, state_ssm, state_conv, cache_k_win, cache_v_win, norm_mix, norm_ffn, norm_kv, norm_final, m_w_in, m_conv_w, m_conv_b, m_dt_bias, m_a_log, m_d_skip, m_norm, m_w_out, a_w_kv, a_b_kv, a_w_q, a_b_q, a_sinks, a_w_o, a_b_o, p_w_q, p_sub_k1, p_sub_k2, p_u, p_v):
    depth = norm_mix.shape[0]
    n_a = m_w_in.shape[0]
    d = x_prompt.shape[-1]
    inner = m_w_out.shape[1]
    conv_dim = m_conv_w.shape[2]
    heads = m_d_skip.shape[1]
    kvd = a_w_kv.shape[1] // 2
    qdim = p_w_q.shape[2] // PEER_HEADS
    dt_col = inner + conv_dim
    in_pad = dt_col + LANES

    w_in = jnp.pad(m_w_in, ((0, 0), (0, 0), (0, in_pad - m_w_in.shape[2]))).astype(BF16)
    w_out = m_w_out.astype(BF16)
    w_kv = a_w_kv.astype(BF16)
    w_q = a_w_q.astype(BF16)
    w_o = a_w_o.astype(BF16)
    pw_q = p_w_q.astype(BF16)
    k1 = p_sub_k1.astype(BF16)
    k2 = p_sub_k2.astype(BF16)
    u_tab = p_u.astype(BF16)
    v_tab = p_v.astype(BF16)
    zeros_in = jnp.zeros((in_pad,), F32)
    zeros_pq = jnp.zeros((p_w_q.shape[2],), F32)
    in_tn = _tile(in_pad, (896, 128))

    def run(x, ssm_init, conv_init, k_prev, v_prev):
        bsz, seq, _ = x.shape
        h = x.reshape(bsz * seq, d)
        ssm_out, conv_out = [], []
        k_src = v_src = None
        for l in range(depth):
            if l < n_a:
                proj = norm_matmul(h, norm_mix[l], w_in[l], zeros_in, in_tn)
                if conv_init is None:
                    prev8 = jnp.zeros((bsz, SUBLANES, conv_dim), F32)
                else:
                    prev8 = jnp.pad(conv_init[l], ((0, 0), (SUBLANES - (CONV_WIDTH - 1), 0), (0, 0)))
                xbc = causal_conv_silu(proj, prev8, m_conv_w[l], m_conv_b[l], bsz, seq, inner)
                s0 = None if ssm_init is None else ssm_init[l].reshape(bsz, inner, SSM_STATE)
                y, s_new = ssd_scan(xbc, proj, dt_col, m_dt_bias[l], m_a_log[l], m_d_skip[l], s0, bsz, seq)
                ssm_out.append(s_new.reshape(bsz, heads, SSM_HEAD_DIM, SSM_STATE))
                tail = proj.reshape(bsz, seq, in_pad)[:, seq - (CONV_WIDTH - 1):, inner:dt_col]
                conv_out.append(tail)
                h = gated_norm_out(y, proj, m_norm[l], w_out[l], h)
            else:
                j = l - n_a
                if j == 0:
                    kv = norm_matmul(h, norm_kv, w_kv, a_b_kv, 2 * kvd)
                    kv3 = kv.reshape(bsz, seq, 2 * kvd)
                    if k_prev is not None:
                        k_all = jnp.concatenate([k_prev.reshape(bsz, WINDOW, kvd), kv3[:, :, :kvd]], axis=1)
                        v_all = jnp.concatenate([v_prev.reshape(bsz, WINDOW, kvd), kv3[:, :, kvd:]], axis=1)
                        k_src = k_all.reshape(bsz * (WINDOW + seq), kvd)
                        v_src = v_all.reshape(bsz * (WINDOW + seq), kvd)
                        k_win, v_win = k_all[:, -WINDOW:], v_all[:, -WINDOW:]
                    else:
                        k_win, v_win = kv3[:, -WINDOW:, :kvd], kv3[:, -WINDOW:, kvd:]
                qp = norm_matmul(h, norm_mix[l], w_q[j], a_b_q[j], w_q.shape[2])
                if k_prev is None:
                    o = swa_attention(qp, a_sinks[j], kv, kv, 0, 1, bsz, seq, None)
                else:
                    o = swa_attention(qp, a_sinks[j], k_src, v_src, 0, 0, bsz, seq, WINDOW + seq)
                h = matmul_bias_res(o, w_o[j], a_b_o[j], h)
            qh = norm_matmul(h, norm_ffn[l], pw_q[l], zeros_pq, pw_q.shape[2], head_dim=qdim)
            e1, c1, e2, r2 = peer_topk(qh, k1[l], k2[l])
            h = peer_dense(h, norm_ffn[l], u_tab, v_tab, l, e1, c1, e2, r2)
        y = rmsnorm_rows(h, norm_final).reshape(bsz, seq, d)
        kshape = (bsz, WINDOW, KV_HEADS, ATTN_HEAD_DIM)
        return y, jnp.stack(ssm_out), jnp.stack(conv_out), k_win.reshape(kshape), v_win.reshape(kshape)

    y_p, p_ssm, p_conv, p_k, p_v_win = run(x_prompt, None, None, None, None)
    y_s, s_ssm, s_conv, s_k, s_v = run(x_sample, state_ssm, state_conv, cache_k_win, cache_v_win)
    return (y_p, y_s, p_ssm, p_conv, p_k, p_v_win, s_ssm, s_conv, s_k, s_v)
```

```python
import functools

import jax
import jax.numpy as jnp
from jax import lax
from jax.experimental import pallas as pl
from jax.experimental.pallas import tpu as pltpu

F32 = jnp.float32
BF16 = jnp.bfloat16
EPS = 1e-6

WINDOW = 128
SSM_HEAD_DIM = 64
SSM_STATE = 128
SSM_GROUPS = 8
CONV_WIDTH = 4
SSD_BLOCK = 128
ATTN_CHUNK = 64
ATTN_HEAD_DIM = 64
KV_HEADS = 4
PEER_HEADS = 8
PEER_NKEYS = 128
PEER_TOPK = 16
LANES = 128
SUBLANES = 8
MIB = 1024 * 1024

_NT = (((1,), (1,)), ((), ()))
_GELU_C1 = 0.7978845608028654
_GELU_C3 = _GELU_C1 * 0.044715
_GATE_PAIRS_PER_TRIP = 16

ROW_TILES = (1024, 512, 256)
PEER_ROW_TILES = (512, 256)
PEER_EXPERT_TILE = 1024
CONV_ROW_TILES = (256, 128, 64, 32, 16)
CONV_COL_TILE = 1024


def _cparams(n_axes, vmem_mib):
    return pltpu.CompilerParams(
        dimension_semantics=("arbitrary",) * n_axes, vmem_limit_bytes=vmem_mib * MIB)


def _tile(n, prefs):
    for p in prefs:
        if n % p == 0:
            return p
    return n


def _rms(x, g):
    ms = jnp.mean(x * x, axis=-1, keepdims=True)
    return x * lax.rsqrt(ms + EPS) * g


def _norm_matmul_kernel(x_ref, g_ref, w_ref, b_ref, o_ref, xn_ref, *, head_dim):
    @pl.when(pl.program_id(1) == 0)
    def _():
        xn_ref[...] = _rms(x_ref[...], g_ref[...]).astype(BF16)

    res = jnp.dot(xn_ref[...], w_ref[...], preferred_element_type=F32) + b_ref[...]
    if head_dim is None:
        o_ref[...] = res
    else:
        for k in range(o_ref.shape[0]):
            o_ref[k] = res[:, k * head_dim:(k + 1) * head_dim].astype(o_ref.dtype)


def norm_matmul(x, g, w, b, tn, head_dim=None):
    t, d = x.shape
    n = w.shape[1]
    tm = _tile(t, ROW_TILES)
    if head_dim is not None:
        out_shape = jax.ShapeDtypeStruct((n // head_dim, t, head_dim), BF16)
        out_spec = pl.BlockSpec((tn // head_dim, tm, head_dim), lambda i, j: (j, i, 0))
    else:
        out_shape = jax.ShapeDtypeStruct((t, n), F32)
        out_spec = pl.BlockSpec((tm, tn), lambda i, j: (i, j))
    return pl.pallas_call(
        functools.partial(_norm_matmul_kernel, head_dim=head_dim),
        grid=(t // tm, n // tn),
        in_specs=[pl.BlockSpec((tm, d), lambda i, j: (i, 0)),
                  pl.BlockSpec((1, d), lambda i, j: (0, 0)),
                  pl.BlockSpec((d, tn), lambda i, j: (0, j)),
                  pl.BlockSpec((1, tn), lambda i, j: (0, j))],
        out_specs=out_spec,
        out_shape=out_shape,
        scratch_shapes=[pltpu.VMEM((tm, d), BF16)],
        compiler_params=_cparams(2, 40),
        name="norm_matmul",
    )(x, g.reshape(1, d), w, b.reshape(1, n))


def _matmul_res_kernel(x_ref, w_ref, b_ref, r_ref, o_ref):
    mix = jnp.dot(x_ref[...].astype(BF16), w_ref[...], preferred_element_type=F32) + b_ref[...]
    o_ref[...] = r_ref[...] + mix


def matmul_bias_res(x, w, b, res):
    t, k = x.shape
    n = w.shape[1]
    tm = _tile(t, ROW_TILES[1:])
    return pl.pallas_call(
        _matmul_res_kernel,
        grid=(t // tm,),
        in_specs=[pl.BlockSpec((tm, k), lambda i: (i, 0)),
                  pl.BlockSpec((k, n), lambda i: (0, 0)),
                  pl.BlockSpec((1, n), lambda i: (0, 0)),
                  pl.BlockSpec((tm, n), lambda i: (i, 0))],
        out_specs=pl.BlockSpec((tm, n), lambda i: (i, 0)),
        out_shape=jax.ShapeDtypeStruct((t, n), F32),
        compiler_params=_cparams(1, 32),
        name="matmul_bias_res",
    )(x, w, b.reshape(1, n), res)


def _rmsnorm_kernel(x_ref, g_ref, o_ref):
    o_ref[...] = _rms(x_ref[...], g_ref[...])


def rmsnorm_rows(x, g):
    t, d = x.shape
    tm = _tile(t, ROW_TILES[1:])
    return pl.pallas_call(
        _rmsnorm_kernel,
        grid=(t // tm,),
        in_specs=[pl.BlockSpec((tm, d), lambda i: (i, 0)), pl.BlockSpec((1, d), lambda i: (0, 0))],
        out_specs=pl.BlockSpec((tm, d), lambda i: (i, 0)),
        out_shape=jax.ShapeDtypeStruct((t, d), F32),
        compiler_params=_cparams(1, 32),
        name="rmsnorm",
    )(x, g.reshape(1, d))


def _conv_kernel(x_ref, halo_ref, prev_ref, w_ref, b_ref, o_ref):
    first = pl.program_id(1) == 0
    x = x_ref[...]
    ext = jnp.concatenate([jnp.where(first, prev_ref[...], halo_ref[...]), x], axis=0)
    acc = b_ref[...]
    for k in range(CONV_WIDTH - 1):
        tap = pltpu.roll(ext, CONV_WIDTH - 1 - k, 0)[SUBLANES:]
        acc = acc + tap * w_ref[k:k + 1, :]
    acc = acc + x * w_ref[CONV_WIDTH - 1:CONV_WIDTH, :]
    o_ref[...] = acc * jax.nn.sigmoid(acc)


def causal_conv_silu(proj, prev8, w, b, bsz, seq, col0):
    c = w.shape[1]
    tl = _tile(seq, CONV_ROW_TILES)
    tc = CONV_COL_TILE
    nl = seq // tl
    cb0 = col0 // tc
    hpb = tl // SUBLANES
    return pl.pallas_call(
        _conv_kernel,
        grid=(bsz, nl, c // tc),
        in_specs=[pl.BlockSpec((tl, tc), lambda bi, li, ci: (bi * nl + li, cb0 + ci)),
                  pl.BlockSpec((SUBLANES, tc),
                               lambda bi, li, ci: (jnp.maximum((bi * nl + li) * hpb - 1, 0), cb0 + ci)),
                  pl.BlockSpec((None, SUBLANES, tc), lambda bi, li, ci: (bi, 0, ci)),
                  pl.BlockSpec((CONV_WIDTH, tc), lambda bi, li, ci: (0, ci)),
                  pl.BlockSpec((1, tc), lambda bi, li, ci: (0, ci))],
        out_specs=pl.BlockSpec((tl, tc), lambda bi, li, ci: (bi * nl + li, ci)),
        out_shape=jax.ShapeDtypeStruct((bsz * seq, c), F32),
        compiler_params=_cparams(3, 32),
        name="conv_silu",
    )(proj, proj, prev8, w, b.reshape(1, c))


def _split3(x):
    hi = x.astype(BF16)
    r1 = x - hi.astype(F32)
    mid = r1.astype(BF16)
    lo = (r1 - mid.astype(F32)).astype(BF16)
    return hi, mid, lo


def _ssd_kernel(*refs, rows, has_init):
    if has_init:
        (x_ref, b_ref, c_ref, dt_ref, bias_ref, alog_ref, dskip_ref, echan_ref, ecol_ref, s0_ref,
         y_ref, sout_ref, st_ref) = refs
    else:
        (x_ref, b_ref, c_ref, dt_ref, bias_ref, alog_ref, dskip_ref, echan_ref, ecol_ref,
         y_ref, sout_ref, st_ref) = refs
    q = SSD_BLOCK
    gw = x_ref.shape[1] // SSM_GROUPS
    hpg = gw // SSM_HEAD_DIM
    ci = pl.program_id(1)

    @pl.when(ci == 0)
    def _():
        for g in range(SSM_GROUPS):
            if has_init:
                st_ref[g] = s0_ref[g * gw:(g + 1) * gw, :].T
            else:
                st_ref[g] = jnp.zeros((SSM_STATE, gw), F32)

    def pad(a):
        if rows == q:
            return a
        return jnp.concatenate([a, jnp.zeros((q - rows, a.shape[1]), a.dtype)], axis=0)

    raw = dt_ref[...] + bias_ref[...]
    dt = pad(jnp.maximum(raw, 0.0) + jnp.log1p(jnp.exp(-jnp.abs(raw))))
    a_neg = -jnp.exp(alog_ref[...])
    dta = dt * a_neg
    ri = lax.broadcasted_iota(jnp.int32, (q, q), 0)
    cj = lax.broadcasted_iota(jnp.int32, (q, q), 1)
    tri = ri >= cj
    trib = jnp.where(tri, 1.0, 0.0).astype(BF16)
    acum = sum(jnp.dot(trib, part, preferred_element_type=F32) for part in _split3(dta))
    acum_t = acum.T
    a_last = acum[q - 1:q, :]
    dec_end = jnp.exp(a_last - acum)
    ea = jnp.exp(acum)
    wgt = dt * dec_end

    two = lambda a: jnp.concatenate(_split3(a)[:2], axis=1)
    chan = jnp.dot(jnp.concatenate([two(dt), two(wgt), two(ea)], axis=0), echan_ref[...],
                   preferred_element_type=F32)
    dt_e, wgt_e, ea_e = chan[0:q], chan[q:2 * q], chan[2 * q:3 * q]
    acum_cols = jnp.dot(jnp.concatenate(_split3(acum), axis=1), ecol_ref[...],
                        preferred_element_type=F32)

    xs = pad(x_ref[...])
    bm = pad(b_ref[...])
    cm = pad(c_ref[...])

    for g in range(SSM_GROUPS):
        gsl = slice(g * gw, (g + 1) * gw)
        bg = bm[:, g * SSM_STATE:(g + 1) * SSM_STATE]
        cgb = cm[:, g * SSM_STATE:(g + 1) * SSM_STATE].astype(BF16)
        cb = lax.dot_general(cgb, bg.astype(BF16), _NT, preferred_element_type=F32)
        xg = xs[:, gsl]
        xdt = (xg * dt_e[:, gsl]).astype(BF16)
        xw = (xg * wgt_e[:, gsl]).astype(BF16)
        st_prev = st_ref[g]
        y_off = jnp.dot(cgb, st_prev.astype(BF16), preferred_element_type=F32) * ea_e[:, gsl]
        st_new = jnp.dot(bg.T.astype(BF16), xw, preferred_element_type=F32)
        st_ref[g] = ea_e[q - 1:q, gsl] * st_prev + st_new
        yd = []
        for r in range(hpg):
            h = g * hpg + r
            seg = acum_cols[:, h * q:(h + 1) * q] - acum_t[h:h + 1, :]
            lmat = jnp.where(tri, jnp.exp(seg), 0.0)
            m = (cb * lmat).astype(BF16)
            yd.append(jnp.dot(m, xdt[:, r * SSM_HEAD_DIM:(r + 1) * SSM_HEAD_DIM],
                              preferred_element_type=F32))
        yg = jnp.concatenate(yd, axis=1) + y_off + xg * dskip_ref[:, g * gw:(g + 1) * gw]
        y_ref[:, g * gw:(g + 1) * gw] = yg[:rows]

    @pl.when(ci == pl.num_programs(1) - 1)
    def _():
        for g in range(SSM_GROUPS):
            sout_ref[g * gw:(g + 1) * gw, :] = st_ref[g].T


def ssd_scan(xbc, proj, dt_col, dt_bias, a_log, d_skip, s0, bsz, seq):
    inner = d_skip.shape[0] * SSM_HEAD_DIM
    gn = SSM_GROUPS * SSM_STATE
    rows = min(SSD_BLOCK, seq)
    nc = seq // rows
    heads = d_skip.shape[0]
    pad128 = lambda v: jnp.pad(v, (0, LANES - heads)).reshape(1, LANES)
    row_map = lambda bi, ci: bi * nc + ci
    in_specs = [pl.BlockSpec((rows, inner), lambda bi, ci: (row_map(bi, ci), 0)),
                pl.BlockSpec((rows, gn), lambda bi, ci: (row_map(bi, ci), inner // gn)),
                pl.BlockSpec((rows, gn), lambda bi, ci: (row_map(bi, ci), inner // gn + 1)),
                pl.BlockSpec((rows, LANES), lambda bi, ci: (row_map(bi, ci), dt_col // LANES)),
                pl.BlockSpec((1, LANES), lambda bi, ci: (0, 0)),
                pl.BlockSpec((1, LANES), lambda bi, ci: (0, 0)),
                pl.BlockSpec((1, inner), lambda bi, ci: (0, 0)),
                pl.BlockSpec((2 * LANES, inner), lambda bi, ci: (0, 0)),
                pl.BlockSpec((3 * LANES, heads * SSD_BLOCK), lambda bi, ci: (0, 0))]
    head_of = lambda width: (jnp.arange(heads * width) // width)[None, :] == jnp.arange(LANES)[:, None]
    echan = jnp.tile(head_of(SSM_HEAD_DIM).astype(BF16), (2, 1))
    ecol = jnp.tile(head_of(SSD_BLOCK).astype(BF16), (3, 1))
    args = [xbc, xbc, xbc, proj, pad128(dt_bias), pad128(a_log),
            jnp.repeat(d_skip, SSM_HEAD_DIM).reshape(1, inner), echan, ecol]
    if s0 is not None:
        in_specs.append(pl.BlockSpec((None, inner, SSM_STATE), lambda bi, ci: (bi, 0, 0)))
        args.append(s0)
    return pl.pallas_call(
        functools.partial(_ssd_kernel, rows=rows, has_init=s0 is not None),
        grid=(bsz, nc),
        in_specs=in_specs,
        out_specs=[pl.BlockSpec((rows, inner), lambda bi, ci: (row_map(bi, ci), 0)),
                   pl.BlockSpec((None, inner, SSM_STATE), lambda bi, ci: (bi, 0, 0))],
        out_shape=[jax.ShapeDtypeStruct((bsz * seq, inner), F32),
                   jax.ShapeDtypeStruct((bsz, inner, SSM_STATE), F32)],
        scratch_shapes=[pltpu.VMEM((SSM_GROUPS, SSM_STATE, inner // SSM_GROUPS), F32)],
        compiler_params=_cparams(2, 48),
        name="ssd_scan",
    )(*args)


def _gated_out_kernel(y_ref, z_ref, g_ref, w_ref, r_ref, o_ref):
    z = z_ref[...]
    y = y_ref[...] * (z * jax.nn.sigmoid(z))
    yn = _rms(y, g_ref[...]).astype(BF16)
    o_ref[...] = r_ref[...] + jnp.dot(yn, w_ref[...], preferred_element_type=F32)


def gated_norm_out(y, proj, g, w, res):
    t, inner = y.shape
    n = w.shape[1]
    tm = _tile(t, (256, 128))
    return pl.pallas_call(
        _gated_out_kernel,
        grid=(t // tm,),
        in_specs=[pl.BlockSpec((tm, inner), lambda i: (i, 0)),
                  pl.BlockSpec((tm, inner), lambda i: (i, 0)),
                  pl.BlockSpec((1, inner), lambda i: (0, 0)),
                  pl.BlockSpec((inner, n), lambda i: (0, 0)),
                  pl.BlockSpec((tm, n), lambda i: (i, 0))],
        out_specs=pl.BlockSpec((tm, n), lambda i: (i, 0)),
        out_shape=jax.ShapeDtypeStruct((t, n), F32),
        compiler_params=_cparams(1, 40),
        name="gated_norm_out",
    )(y, proj, g.reshape(1, inner), w, res)


def _attn_banded_kernel(q_ref, sink_ref, kp_ref, kc_ref, vp_ref, vc_ref, o_ref):
    tq = q_ref.shape[0]
    hd = ATTN_HEAD_DIM
    qpk = q_ref.shape[1] // hd // KV_HEADS
    scale = hd ** -0.5
    nk = WINDOW + tq
    back = WINDOW // ATTN_CHUNK
    kc = lax.broadcasted_iota(jnp.int32, (nk, tq), 0) // ATTN_CHUNK
    qc = lax.broadcasted_iota(jnp.int32, (nk, tq), 1) // ATTN_CHUNK
    first_kc = jnp.where(pl.program_id(1) == 0, back, 0)
    valid = (kc >= qc) & (kc <= qc + back) & (kc >= first_kc)
    bias = jnp.concatenate([jnp.where(valid, 0.0, -jnp.inf)] * qpk, axis=1)
    for pair in range(KV_HEADS // 2):
        cols = slice(pair * 2 * hd, (pair + 1) * 2 * hd)
        k2 = jnp.concatenate([kp_ref[:, cols], kc_ref[:, cols]], axis=0)
        v2t = jnp.concatenate([vp_ref[:, cols], vc_ref[:, cols]], axis=0).T
        for sub in range(2):
            kh = pair * 2 + sub
            heads = range(kh * qpk, (kh + 1) * qpk)
            kmat = k2[:, sub * hd:(sub + 1) * hd].astype(BF16)
            vt = v2t[sub * hd:(sub + 1) * hd, :].astype(BF16)
            q4 = jnp.concatenate([q_ref[:, h * hd:(h + 1) * hd] for h in heads], axis=0).astype(BF16)
            s = lax.dot_general(kmat, q4, _NT, preferred_element_type=F32) * scale + bias
            sink = jnp.concatenate([jnp.broadcast_to(sink_ref[:, h:h + 1], (1, tq)) for h in heads], axis=1)
            m = jnp.maximum(jnp.max(s, axis=0, keepdims=True), sink)
            p = jnp.exp(s - m)
            den = jnp.sum(p, axis=0, keepdims=True) + jnp.exp(sink - m)
            ot = jnp.dot(vt, (p * (1.0 / den)).astype(BF16), preferred_element_type=F32)
            for g in range(0, qpk, 2):
                two = jnp.concatenate([ot[:, g * tq:(g + 1) * tq], ot[:, (g + 1) * tq:(g + 2) * tq]], axis=0)
                o_ref[:, (kh * qpk + g) * hd:(kh * qpk + g + 2) * hd] = two.T


def _attn_rows_kernel(q_ref, sink_ref, k_ref, v_ref, o_ref):
    tq = q_ref.shape[0]
    hd = ATTN_HEAD_DIM
    qpk = q_ref.shape[1] // hd // KV_HEADS
    scale = hd ** -0.5
    outs = []
    for kh in range(KV_HEADS):
        kmat = k_ref[:, kh * hd:(kh + 1) * hd].astype(BF16)
        vmat = v_ref[:, kh * hd:(kh + 1) * hd].astype(BF16)
        heads = range(kh * qpk, (kh + 1) * qpk)
        q4 = jnp.concatenate([q_ref[:, h * hd:(h + 1) * hd] for h in heads], axis=0).astype(BF16)
        s = lax.dot_general(q4, kmat, _NT, preferred_element_type=F32) * scale
        sink = jnp.concatenate([jnp.broadcast_to(sink_ref[:, h:h + 1], (tq, 1)) for h in heads], axis=0)
        m = jnp.maximum(jnp.max(s, axis=-1, keepdims=True), sink)
        p = jnp.exp(s - m)
        den = jnp.sum(p, axis=-1, keepdims=True) + jnp.exp(sink - m)
        o4 = jnp.dot((p / den).astype(BF16), vmat, preferred_element_type=F32)
        outs += [o4[g * tq:(g + 1) * tq] for g in range(qpk)]
    o_ref[...] = jnp.concatenate(outs, axis=1)


def swa_attention(q, sinks, k_src, v_src, k_col, v_col, bsz, seq, kv_rows):
    t, hqd = q.shape
    kvd = KV_HEADS * ATTN_HEAD_DIM
    n_heads = sinks.shape[0]
    if kv_rows is None:
        body = _attn_banded_kernel
        tq = _tile(seq, (256, WINDOW))
        nq = seq // tq
        per = tq // WINDOW
        kv_specs = lambda col: [
            pl.BlockSpec((WINDOW, kvd), lambda bi, ci: ((bi * nq + ci) * per - jnp.minimum(ci, 1), col)),
            pl.BlockSpec((tq, kvd), lambda bi, ci: (bi * nq + ci, col))]
    else:
        body = _attn_rows_kernel
        tq, nq = seq, 1
        kv_specs = lambda col: [pl.BlockSpec((kv_rows, kvd), lambda bi, ci: (bi, col))]
    k_specs, v_specs = kv_specs(k_col), kv_specs(v_col)
    in_specs = ([pl.BlockSpec((tq, hqd), lambda bi, ci: (bi * nq + ci, 0)),
                 pl.BlockSpec((1, LANES), lambda bi, ci: (0, 0))] + k_specs + v_specs)
    return pl.pallas_call(
        body,
        grid=(bsz, nq),
        in_specs=in_specs,
        out_specs=pl.BlockSpec((tq, hqd), lambda bi, ci: (bi * nq + ci, 0)),
        out_shape=jax.ShapeDtypeStruct((t, hqd), F32),
        compiler_params=_cparams(2, 40),
        name="swa_attention",
    )(q, jnp.pad(sinks, (0, LANES - n_heads)).reshape(1, LANES),
      *([k_src] * len(k_specs)), *([v_src] * len(v_specs)))


_PAIRS = [(a, b) for a in range(PEER_TOPK) for b in range(PEER_TOPK) if (a + 1) * (b + 1) <= PEER_TOPK]
_CAND_ROWS = -(-len(_PAIRS) // SUBLANES) * SUBLANES
_MARK_BASE = -(2.0 ** 127)
_MARK_STEP = 2.0 ** 115
_MARK_CUT = -1.5 * 2.0 ** 126
_CAND_PAD = -(2.0 ** 126)


def _extract_topk(s, n_rounds, break_ties):
    n = s.shape[0]
    iota = lax.broadcasted_iota(jnp.int32, s.shape, 0)
    vals = []
    for r in range(n_rounds):
        m = jnp.max(s, axis=0, keepdims=True)
        sel = s == m
        if break_ties:
            sel = iota == jnp.min(jnp.where(sel, iota, n), axis=0, keepdims=True)
        s = jnp.where(sel, _MARK_BASE + r * _MARK_STEP, s)
        vals.append(m)
    extracted = s < _MARK_CUT
    rank = jnp.where(extracted, (s - _MARK_BASE) * (1.0 / _MARK_STEP), float(n_rounds))
    count = jnp.sum(jnp.where(extracted, 1.0, 0.0), axis=0, keepdims=True)
    return rank, vals, count


def _peer_topk_kernel(q_ref, k1_ref, k2_ref, e1_ref, c1_ref, e2_ref, r2_ref,
                      s_ref, v_ref, cand_ref, sel_ref, tie_ref, rank_ref):
    half = k1_ref.shape[1]
    nh, tm = q_ref.shape[0], q_ref.shape[1]

    for h in range(nh):
        qh = q_ref[h]
        s_ref[2 * h] = lax.dot_general(k1_ref[...], qh[:, :half], _NT, preferred_element_type=F32)
        s_ref[2 * h + 1] = lax.dot_general(k2_ref[...], qh[:, half:], _NT, preferred_element_type=F32)

    def halves(h, break_ties):
        s1_ref, s2_ref = s_ref.at[2 * h], s_ref.at[2 * h + 1]
        tied = []
        for lc in range(tm // LANES):
            lanes = slice(lc * LANES, (lc + 1) * LANES)
            rank1, v1, n1 = _extract_topk(s1_ref[:, lanes], PEER_TOPK, break_ties)
            rank2, v2, n2 = _extract_topk(s2_ref[:, lanes], PEER_TOPK, break_ties)
            rank_ref[h, :, lanes] = rank1
            r2_ref[h, lanes, :] = rank2.T
            for r in range(PEER_TOPK):
                v_ref[2 * h, r:r + 1, lanes] = v1[r]
                v_ref[2 * h + 1, r:r + 1, lanes] = v2[r]
            tied.append(jnp.where((n1 != PEER_TOPK) | (n2 != PEER_TOPK), 1.0, 0.0))
        return jnp.concatenate(tied, axis=1)

    def joint(h, break_ties):
        s1_ref, s2_ref = s_ref.at[2 * h], s_ref.at[2 * h + 1]
        v1_ref, v2_ref = v_ref.at[2 * h], v_ref.at[2 * h + 1]
        cand_ref[...] = jnp.full((_CAND_ROWS, tm), _CAND_PAD, F32)
        for i, (a, b) in enumerate(_PAIRS):
            cand_ref[i:i + 1, :] = v1_ref[a:a + 1, :] + v2_ref[b:b + 1, :]
        cand = cand_ref[...]
        crank, _, nc = _extract_topk(cand, PEER_TOPK, break_ties)
        chosen = crank < PEER_TOPK
        z = 2.0 * jnp.sum(jnp.where(chosen, jnp.exp(cand - cand[0:1]), 0.0), axis=0, keepdims=True)
        sel_ref[...] = jnp.where(chosen, 1.0, 0.0)
        cnts = []
        row = 0
        for a in range(PEER_TOPK):
            nb = sum(1 for (pa, _) in _PAIRS if pa == a)
            cnts.append(jnp.sum(sel_ref[row:row + nb, :], axis=0, keepdims=True))
            row += nb
        for lc in range(tm // LANES):
            lanes = slice(lc * LANES, (lc + 1) * LANES)
            rank1 = rank_ref[h, :, lanes]
            c1 = jnp.zeros(rank1.shape, F32)
            for a in range(PEER_TOPK):
                c1 = jnp.where(rank1 == a, cnts[a][:, lanes], c1)
            c1_ref[h, lanes, :] = c1.T
            e1_ref[h, lanes, :] = jnp.exp(s1_ref[:, lanes] - v1_ref[0:1, lanes]).T
            e2_ref[h, lanes, :] = (jnp.exp(s2_ref[:, lanes] - v2_ref[0:1, lanes]) / z[:, lanes]).T
        return jnp.where(nc != PEER_TOPK, 1.0, 0.0)

    tied = [halves(0, break_ties=False)]
    for h in range(1, nh):
        tied.append(halves(h, break_ties=False))
        tied[h - 1] = jnp.maximum(tied[h - 1], joint(h - 1, break_ties=False))
    tied[nh - 1] = jnp.maximum(tied[nh - 1], joint(nh - 1, break_ties=False))
    tie_ref[...] = jnp.concatenate(tied, axis=0)

    def exact(h, carry):
        @pl.when(jnp.max(tie_ref[pl.ds(h, 1), :]) > 0.0)
        def _():
            halves(h, break_ties=True)
            joint(h, break_ties=True)

        return carry

    @pl.when(jnp.max(tie_ref[...]) > 0.0)
    def _():
        lax.fori_loop(0, nh, exact, 0)


def peer_topk(qh, k1, k2):
    nh, t, qd = qh.shape
    nk = k1.shape[0]
    tm = _tile(t, (2 * LANES, LANES))
    out32 = jax.ShapeDtypeStruct((nh, t, nk), F32)
    ospec = pl.BlockSpec((nh, tm, nk), lambda i: (0, i, 0))
    return pl.pallas_call(
        _peer_topk_kernel,
        grid=(t // tm,),
        in_specs=[pl.BlockSpec((nh, tm, qd), lambda i: (0, i, 0)),
                  pl.BlockSpec((nk, qd // 2), lambda i: (0, 0)),
                  pl.BlockSpec((nk, qd // 2), lambda i: (0, 0))],
        out_specs=[ospec] * 4,
        out_shape=[out32] * 4,
        scratch_shapes=[pltpu.VMEM((2 * nh, nk, tm), F32), pltpu.VMEM((2 * nh, PEER_TOPK, tm), F32),
                        pltpu.VMEM((_CAND_ROWS, tm), F32), pltpu.VMEM((_CAND_ROWS, tm), F32),
                        pltpu.VMEM((nh, tm), F32), pltpu.VMEM((nh, nk, tm), F32)],
        compiler_params=_cparams(1, 32),
        name="peer_topk",
    )(qh, k1, k2)


def _peer_dense_kernel(h_ref, g_ref, u_ref, v_ref, e1_ref, c1_ref, e2_ref, r2_ref, o_ref,
                       xn_ref, acc_ref, gate_ref):
    ki = pl.program_id(1)
    nh, tm, nkeys = e2_ref.shape
    te = u_ref.shape[0]
    half_tm = tm // 2

    @pl.when(ki == 0)
    def _():
        xn_ref[...] = _rms(h_ref[...], g_ref[...]).astype(BF16)
        acc_ref[...] = jnp.zeros(acc_ref.shape, F32)
        level = (lax.broadcasted_iota(jnp.int32, (PEER_TOPK, nkeys), 0) + 1).astype(F32)

        def token_gate(t):
            lhs, rhs = [], []
            for h in range(nh):
                row = pl.ds(t, 1)
                lhs.append(jnp.where(c1_ref[h, row, :] == level, e1_ref[h, row, :], 0.0))
                rhs.append(jnp.where(r2_ref[h, row, :] < level, e2_ref[h, row, :], 0.0))
            lhs = jnp.concatenate(lhs, axis=0).astype(BF16)
            rhs = jnp.concatenate(rhs, axis=0).astype(BF16)
            return lax.dot_general(lhs, rhs, (((0,), (0,)), ((), ())), preferred_element_type=F32)

        def bf16_bits(g):
            return lax.bitcast_convert_type(g.astype(BF16).astype(F32), jnp.uint32)

        def token_pairs(i, carry):
            for j in range(_GATE_PAIRS_PER_TRIP):
                p = i * _GATE_PAIRS_PER_TRIP + j
                even, odd = bf16_bits(token_gate(2 * p)), bf16_bits(token_gate(2 * p + 1))
                words = (odd & jnp.uint32(0xFFFF0000)) | (even >> 16)
                gate_ref[pl.ds(pl.multiple_of(p * nkeys, nkeys), nkeys), :] = words
            return carry

        lax.fori_loop(0, half_tm // _GATE_PAIRS_PER_TRIP, token_pairs, 0)

    x = lax.dot_general(xn_ref[...], u_ref[...], _NT, preferred_element_type=F32).astype(BF16)
    act2 = x + x * jnp.tanh(x * (_GELU_C1 + _GELU_C3 * (x * x)))
    pieces = []
    for r in range(te // nkeys):
        key1 = ki * (te // nkeys) + r
        gate = pltpu.bitcast(gate_ref[pl.ds(key1, half_tm, stride=nkeys), :], BF16)
        pieces.append(gate * act2[:, r * nkeys:(r + 1) * nkeys])
    wact = jnp.concatenate(pieces, axis=1)
    acc_ref[...] += jnp.dot(wact, v_ref[...], preferred_element_type=F32)

    @pl.when(ki == pl.num_programs(1) - 1)
    def _():
        o_ref[...] = h_ref[...] + acc_ref[...]


def peer_dense(h, g, u, v, layer, e1, c1, e2, r2):
    t, d = h.shape
    ne = u.shape[1]
    nh, _, nk = e1.shape
    tm = _tile(t, PEER_ROW_TILES)
    te = PEER_EXPERT_TILE
    key_spec = pl.BlockSpec((nh, tm, nk), lambda i, k: (0, i, 0))
    return pl.pallas_call(
        _peer_dense_kernel,
        grid=(t // tm, ne // te),
        in_specs=[pl.BlockSpec((tm, d), lambda i, k: (i, 0)),
                  pl.BlockSpec((1, d), lambda i, k: (0, 0)),
                  pl.BlockSpec((None, te, d), lambda i, k: (layer, k, 0)),
                  pl.BlockSpec((None, te, d), lambda i, k: (layer, k, 0)),
                  key_spec, key_spec, key_spec, key_spec],
        out_specs=pl.BlockSpec((tm, d), lambda i, k: (i, 0)),
        out_shape=jax.ShapeDtypeStruct((t, d), F32),
        scratch_shapes=[pltpu.VMEM((tm, d), BF16), pltpu.VMEM((tm, d), F32),
                        pltpu.VMEM((nk * tm // 2, nk), jnp.uint32)],
        compiler_params=_cparams(2, 56),
        name="peer_dense",
    )(h, g.reshape(1, d), u, v, e1, c1, e2, r2)


def kernel(x_prompt, x_sample, state_ssm, state_conv, cache_k_win, cache_v_win, norm_mix, norm_ffn, norm_kv, norm_final, m_w_in, m_conv_w, m_conv_b, m_dt_bias, m_a_log, m_d_skip, m_norm, m_w_out, a_w_kv, a_b_kv, a_w_q, a_b_q, a_sinks, a_w_o, a_b_o, p_w_q, p_sub_k1, p_sub_k2, p_u, p_v):
    depth = norm_mix.shape[0]
    n_a = m_w_in.shape[0]
    d = x_prompt.shape[-1]
    inner = m_w_out.shape[1]
    conv_dim = m_conv_w.shape[2]
    heads = m_d_skip.shape[1]
    kvd = a_w_kv.shape[1] // 2
    qdim = p_w_q.shape[2] // PEER_HEADS
    dt_col = inner + conv_dim
    in_pad = dt_col + LANES

    w_in = jnp.pad(m_w_in, ((0, 0), (0, 0), (0, in_pad - m_w_in.shape[2]))).astype(BF16)
    w_out = m_w_out.astype(BF16)
    w_kv = a_w_kv.astype(BF16)
    w_q = a_w_q.astype(BF16)
    w_o = a_w_o.astype(BF16)
    pw_q = p_w_q.astype(BF16)
    k1 = p_sub_k1.astype(BF16)
    k2 = p_sub_k2.astype(BF16)
    u_tab = p_u.astype(BF16)
    v_tab = p_v.astype(BF16)
    zeros_in = jnp.zeros((in_pad,), F32)
    zeros_pq = jnp.zeros((p_w_q.shape[2],), F32)
    in_tn = _tile(in_pad, (896, 128))

    def run(x, ssm_init, conv_init, k_prev, v_prev):
        bsz, seq, _ = x.shape
        h = x.reshape(bsz * seq, d)
        ssm_out, conv_out = [], []
        k_src = v_src = None
        for l in range(depth):
            if l < n_a:
                proj = norm_matmul(h, norm_mix[l], w_in[l], zeros_in, in_tn)
                if conv_init is None:
                    prev8 = jnp.zeros((bsz, SUBLANES, conv_dim), F32)
                else:
                    prev8 = jnp.pad(conv_init[l], ((0, 0), (SUBLANES - (CONV_WIDTH - 1), 0), (0, 0)))
                xbc = causal_conv_silu(proj, prev8, m_conv_w[l], m_conv_b[l], bsz, seq, inner)
                s0 = None if ssm_init is None else ssm_init[l].reshape(bsz, inner, SSM_STATE)
                y, s_new = ssd_scan(xbc, proj, dt_col, m_dt_bias[l], m_a_log[l], m_d_skip[l], s0, bsz, seq)
                ssm_out.append(s_new.reshape(bsz, heads, SSM_HEAD_DIM, SSM_STATE))
                tail = proj.reshape(bsz, seq, in_pad)[:, seq - (CONV_WIDTH - 1):, inner:dt_col]
                conv_out.append(tail)
                h = gated_norm_out(y, proj, m_norm[l], w_out[l], h)
            else:
                j = l - n_a
                if j == 0:
                    kv = norm_matmul(h, norm_kv, w_kv, a_b_kv, 2 * kvd)
                    kv3 = kv.reshape(bsz, seq, 2 * kvd)
                    if k_prev is not None:
                        k_all = jnp.concatenate([k_prev.reshape(bsz, WINDOW, kvd), kv3[:, :, :kvd]], axis=1)
                        v_all = jnp.concatenate([v_prev.reshape(bsz, WINDOW, kvd), kv3[:, :, kvd:]], axis=1)
                        k_src = k_all.reshape(bsz * (WINDOW + seq), kvd)
                        v_src = v_all.reshape(bsz * (WINDOW + seq), kvd)
                        k_win, v_win = k_all[:, -WINDOW:], v_all[:, -WINDOW:]
                    else:
                        k_win, v_win = kv3[:, -WINDOW:, :kvd], kv3[:, -WINDOW:, kvd:]
                qp = norm_matmul(h, norm_mix[l], w_q[j], a_b_q[j], w_q.shape[2])
                if k_prev is None:
                    o = swa_attention(qp, a_sinks[j], kv, kv, 0, 1, bsz, seq, None)
                else:
                    o = swa_attention(qp, a_sinks[j], k_src, v_src, 0, 0, bsz, seq, WINDOW + seq)
                h = matmul_bias_res(o, w_o[j], a_b_o[j], h)
            qh = norm_matmul(h, norm_ffn[l], pw_q[l], zeros_pq, pw_q.shape[2], head_dim=qdim)
            e1, c1, e2, r2 = peer_topk(qh, k1[l], k2[l])
            h = peer_dense(h, norm_ffn[l], u_tab, v_tab, l, e1, c1, e2, r2)
        y = rmsnorm_rows(h, norm_final).reshape(bsz, seq, d)
        kshape = (bsz, WINDOW, KV_HEADS, ATTN_HEAD_DIM)
        return y, jnp.stack(ssm_out), jnp.stack(conv_out), k_win.reshape(kshape), v_win.reshape(kshape)

    y_p, p_ssm, p_conv, p_k, p_v_win = run(x_prompt, None, None, None, None)
    y_s, s_ssm, s_conv, s_k, s_v = run(x_sample, state_ssm, state_conv, cache_k_win, cache_v_win)
    return (y_p, y_s, p_ssm, p_conv, p_k, p_v_win, s_ssm, s_conv, s_k, s_v)
```

```python
import functools

import jax
import jax.numpy as jnp
from jax import lax
from jax.experimental import pallas as pl
from jax.experimental.pallas import tpu as pltpu

F32 = jnp.float32
BF16 = jnp.bfloat16
EPS = 1e-6

WINDOW = 128
SSM_HEAD_DIM = 64
SSM_STATE = 128
SSM_GROUPS = 8
CONV_WIDTH = 4
SSD_BLOCK = 128
ATTN_CHUNK = 64
ATTN_HEAD_DIM = 64
KV_HEADS = 4
PEER_HEADS = 8
PEER_NKEYS = 128
PEER_TOPK = 16
LANES = 128
SUBLANES = 8
MIB = 1024 * 1024

_NT = (((1,), (1,)), ((), ()))
_GELU_C1 = 0.7978845608028654
_GELU_C3 = _GELU_C1 * 0.044715
_GATE_PAIRS_PER_TRIP = 16

ROW_TILES = (1024, 512, 256)
PEER_ROW_TILES = (512, 256)
PEER_EXPERT_TILE = 1024
CONV_ROW_TILES = (256, 128, 64, 32, 16)
CONV_COL_TILE = 1024


def _cparams(n_axes, vmem_mib):
    return pltpu.CompilerParams(
        dimension_semantics=("arbitrary",) * n_axes, vmem_limit_bytes=vmem_mib * MIB)


def _tile(n, prefs):
    for p in prefs:
        if n % p == 0:
            return p
    return n


def _rms(x, g):
    ms = jnp.mean(x * x, axis=-1, keepdims=True)
    return x * lax.rsqrt(ms + EPS) * g


def _norm_matmul_kernel(x_ref, g_ref, w_ref, b_ref, o_ref, xn_ref, *, head_dim):
    @pl.when(pl.program_id(1) == 0)
    def _():
        xn_ref[...] = _rms(x_ref[...], g_ref[...]).astype(BF16)

    res = jnp.dot(xn_ref[...], w_ref[...], preferred_element_type=F32) + b_ref[...]
    if head_dim is None:
        o_ref[...] = res
    else:
        for k in range(o_ref.shape[0]):
            o_ref[k] = res[:, k * head_dim:(k + 1) * head_dim].astype(o_ref.dtype)


def norm_matmul(x, g, w, b, tn, head_dim=None):
    t, d = x.shape
    n = w.shape[1]
    tm = _tile(t, ROW_TILES)
    if head_dim is not None:
        out_shape = jax.ShapeDtypeStruct((n // head_dim, t, head_dim), BF16)
        out_spec = pl.BlockSpec((tn // head_dim, tm, head_dim), lambda i, j: (j, i, 0))
    else:
        out_shape = jax.ShapeDtypeStruct((t, n), F32)
        out_spec = pl.BlockSpec((tm, tn), lambda i, j: (i, j))
    return pl.pallas_call(
        functools.partial(_norm_matmul_kernel, head_dim=head_dim),
        grid=(t // tm, n // tn),
        in_specs=[pl.BlockSpec((tm, d), lambda i, j: (i, 0)),
                  pl.BlockSpec((1, d), lambda i, j: (0, 0)),
                  pl.BlockSpec((d, tn), lambda i, j: (0, j)),
                  pl.BlockSpec((1, tn), lambda i, j: (0, j))],
        out_specs=out_spec,
        out_shape=out_shape,
        scratch_shapes=[pltpu.VMEM((tm, d), BF16)],
        compiler_params=_cparams(2, 40),
        name="norm_matmul",
    )(x, g.reshape(1, d), w, b.reshape(1, n))


def _matmul_res_kernel(x_ref, w_ref, b_ref, r_ref, o_ref):
    mix = jnp.dot(x_ref[...].astype(BF16), w_ref[...], preferred_element_type=F32) + b_ref[...]
    o_ref[...] = r_ref[...] + mix


def matmul_bias_res(x, w, b, res):
    t, k = x.shape
    n = w.shape[1]
    tm = _tile(t, ROW_TILES[1:])
    return pl.pallas_call(
        _matmul_res_kernel,
        grid=(t // tm,),
        in_specs=[pl.BlockSpec((tm, k), lambda i: (i, 0)),
                  pl.BlockSpec((k, n), lambda i: (0, 0)),
                  pl.BlockSpec((1, n), lambda i: (0, 0)),
                  pl.BlockSpec((tm, n), lambda i: (i, 0))],
        out_specs=pl.BlockSpec((tm, n), lambda i: (i, 0)),
        out_shape=jax.ShapeDtypeStruct((t, n), F32),
        compiler_params=_cparams(1, 32),
        name="matmul_bias_res",
    )(x, w, b.reshape(1, n), res)


def _rmsnorm_kernel(x_ref, g_ref, o_ref):
    o_ref[...] = _rms(x_ref[...], g_ref[...])


def rmsnorm_rows(x, g):
    t, d = x.shape
    tm = _tile(t, ROW_TILES[1:])
    return pl.pallas_call(
        _rmsnorm_kernel,
        grid=(t // tm,),
        in_specs=[pl.BlockSpec((tm, d), lambda i: (i, 0)), pl.BlockSpec((1, d), lambda i: (0, 0))],
        out_specs=pl.BlockSpec((tm, d), lambda i: (i, 0)),
        out_shape=jax.ShapeDtypeStruct((t, d), F32),
        compiler_params=_cparams(1, 32),
        name="rmsnorm",
    )(x, g.reshape(1, d))


def _conv_kernel(x_ref, halo_ref, prev_ref, w_ref, b_ref, o_ref):
    first = pl.program_id(1) == 0
    x = x_ref[...]
    ext = jnp.concatenate([jnp.where(first, prev_ref[...], halo_ref[...]), x], axis=0)
    acc = b_ref[...]
    for k in range(CONV_WIDTH - 1):
        tap = pltpu.roll(ext, CONV_WIDTH - 1 - k, 0)[SUBLANES:]
        acc = acc + tap * w_ref[k:k + 1, :]
    acc = acc + x * w_ref[CONV_WIDTH - 1:CONV_WIDTH, :]
    o_ref[...] = acc * jax.nn.sigmoid(acc)


def causal_conv_silu(proj, prev8, w, b, bsz, seq, col0):
    c = w.shape[1]
    tl = _tile(seq, CONV_ROW_TILES)
    tc = CONV_COL_TILE
    nl = seq // tl
    cb0 = col0 // tc
    hpb = tl // SUBLANES
    return pl.pallas_call(
        _conv_kernel,
        grid=(bsz, nl, c // tc),
        in_specs=[pl.BlockSpec((tl, tc), lambda bi, li, ci: (bi * nl + li, cb0 + ci)),
                  pl.BlockSpec((SUBLANES, tc),
                               lambda bi, li, ci: (jnp.maximum((bi * nl + li) * hpb - 1, 0), cb0 + ci)),
                  pl.BlockSpec((None, SUBLANES, tc), lambda bi, li, ci: (bi, 0, ci)),
                  pl.BlockSpec((CONV_WIDTH, tc), lambda bi, li, ci: (0, ci)),
                  pl.BlockSpec((1, tc), lambda bi, li, ci: (0, ci))],
        out_specs=pl.BlockSpec((tl, tc), lambda bi, li, ci: (bi * nl + li, ci)),
        out_shape=jax.ShapeDtypeStruct((bsz * seq, c), F32),
        compiler_params=_cparams(3, 32),
        name="conv_silu",
    )(proj, proj, prev8, w, b.reshape(1, c))


def _split3(x):
    hi = x.astype(BF16)
    r1 = x - hi.astype(F32)
    mid = r1.astype(BF16)
    lo = (r1 - mid.astype(F32)).astype(BF16)
    return hi, mid, lo


def _ssd_kernel(*refs, rows, has_init):
    if has_init:
        (x_ref, b_ref, c_ref, dt_ref, bias_ref, alog_ref, dskip_ref, echan_ref, ecol_ref, s0_ref,
         y_ref, sout_ref, st_ref) = refs
    else:
        (x_ref, b_ref, c_ref, dt_ref, bias_ref, alog_ref, dskip_ref, echan_ref, ecol_ref,
         y_ref, sout_ref, st_ref) = refs
    q = SSD_BLOCK
    gw = x_ref.shape[1] // SSM_GROUPS
    hpg = gw // SSM_HEAD_DIM
    ci = pl.program_id(1)

    @pl.when(ci == 0)
    def _():
        for g in range(SSM_GROUPS):
            if has_init:
                st_ref[g] = s0_ref[g * gw:(g + 1) * gw, :].T
            else:
                st_ref[g] = jnp.zeros((SSM_STATE, gw), F32)

    def pad(a):
        if rows == q:
            return a
        return jnp.concatenate([a, jnp.zeros((q - rows, a.shape[1]), a.dtype)], axis=0)

    raw = dt_ref[...] + bias_ref[...]
    dt = pad(jnp.maximum(raw, 0.0) + jnp.log1p(jnp.exp(-jnp.abs(raw))))
    a_neg = -jnp.exp(alog_ref[...])
    dta = dt * a_neg
    ri = lax.broadcasted_iota(jnp.int32, (q, q), 0)
    cj = lax.broadcasted_iota(jnp.int32, (q, q), 1)
    tri = ri >= cj
    trib = jnp.where(tri, 1.0, 0.0).astype(BF16)
    acum = sum(jnp.dot(trib, part, preferred_element_type=F32) for part in _split3(dta))
    acum_t = acum.T
    a_last = acum[q - 1:q, :]
    dec_end = jnp.exp(a_last - acum)
    ea = jnp.exp(acum)
    wgt = dt * dec_end

    two = lambda a: jnp.concatenate(_split3(a)[:2], axis=1)
    chan = jnp.dot(jnp.concatenate([two(dt), two(wgt), two(ea)], axis=0), echan_ref[...],
                   preferred_element_type=F32)
    dt_e, wgt_e, ea_e = chan[0:q], chan[q:2 * q], chan[2 * q:3 * q]
    acum_cols = jnp.dot(jnp.concatenate(_split3(acum), axis=1), ecol_ref[...],
                        preferred_element_type=F32)

    xs = pad(x_ref[...])
    bm = pad(b_ref[...])
    cm = pad(c_ref[...])

    for g in range(SSM_GROUPS):
        gsl = slice(g * gw, (g + 1) * gw)
        bg = bm[:, g * SSM_STATE:(g + 1) * SSM_STATE]
        cgb = cm[:, g * SSM_STATE:(g + 1) * SSM_STATE].astype(BF16)
        cb = lax.dot_general(cgb, bg.astype(BF16), _NT, preferred_element_type=F32)
        xg = xs[:, gsl]
        xdt = (xg * dt_e[:, gsl]).astype(BF16)
        xw = (xg * wgt_e[:, gsl]).astype(BF16)
        st_prev = st_ref[g]
        y_off = jnp.dot(cgb, st_prev.astype(BF16), preferred_element_type=F32) * ea_e[:, gsl]
        st_new = jnp.dot(bg.T.astype(BF16), xw, preferred_element_type=F32)
        st_ref[g] = ea_e[q - 1:q, gsl] * st_prev + st_new
        yd = []
        for r in range(hpg):
            h = g * hpg + r
            seg = acum_cols[:, h * q:(h + 1) * q] - acum_t[h:h + 1, :]
            lmat = jnp.where(tri, jnp.exp(seg), 0.0)
            m = (cb * lmat).astype(BF16)
            yd.append(jnp.dot(m, xdt[:, r * SSM_HEAD_DIM:(r + 1) * SSM_HEAD_DIM],
                              preferred_element_type=F32))
        yg = jnp.concatenate(yd, axis=1) + y_off + xg * dskip_ref[:, g * gw:(g + 1) * gw]
        y_ref[:, g * gw:(g + 1) * gw] = yg[:rows]

    @pl.when(ci == pl.num_programs(1) - 1)
    def _():
        for g in range(SSM_GROUPS):
            sout_ref[g * gw:(g + 1) * gw, :] = st_ref[g].T


def ssd_scan(xbc, proj, dt_col, dt_bias, a_log, d_skip, s0, bsz, seq):
    inner = d_skip.shape[0] * SSM_HEAD_DIM
    gn = SSM_GROUPS * SSM_STATE
    rows = min(SSD_BLOCK, seq)
    nc = seq // rows
    heads = d_skip.shape[0]
    pad128 = lambda v: jnp.pad(v, (0, LANES - heads)).reshape(1, LANES)
    row_map = lambda bi, ci: bi * nc + ci
    in_specs = [pl.BlockSpec((rows, inner), lambda bi, ci: (row_map(bi, ci), 0)),
                pl.BlockSpec((rows, gn), lambda bi, ci: (row_map(bi, ci), inner // gn)),
                pl.BlockSpec((rows, gn), lambda bi, ci: (row_map(bi, ci), inner // gn + 1)),
                pl.BlockSpec((rows, LANES), lambda bi, ci: (row_map(bi, ci), dt_col // LANES)),
                pl.BlockSpec((1, LANES), lambda bi, ci: (0, 0)),
                pl.BlockSpec((1, LANES), lambda bi, ci: (0, 0)),
                pl.BlockSpec((1, inner), lambda bi, ci: (0, 0)),
                pl.BlockSpec((2 * LANES, inner), lambda bi, ci: (0, 0)),
                pl.BlockSpec((3 * LANES, heads * SSD_BLOCK), lambda bi, ci: (0, 0))]
    head_of = lambda width: (jnp.arange(heads * width) // width)[None, :] == jnp.arange(LANES)[:, None]
    echan = jnp.tile(head_of(SSM_HEAD_DIM).astype(BF16), (2, 1))
    ecol = jnp.tile(head_of(SSD_BLOCK).astype(BF16), (3, 1))
    args = [xbc, xbc, xbc, proj, pad128(dt_bias), pad128(a_log),
            jnp.repeat(d_skip, SSM_HEAD_DIM).reshape(1, inner), echan, ecol]
    if s0 is not None:
        in_specs.append(pl.BlockSpec((None, inner, SSM_STATE), lambda bi, ci: (bi, 0, 0)))
        args.append(s0)
    return pl.pallas_call(
        functools.partial(_ssd_kernel, rows=rows, has_init=s0 is not None),
        grid=(bsz, nc),
        in_specs=in_specs,
        out_specs=[pl.BlockSpec((rows, inner), lambda bi, ci: (row_map(bi, ci), 0)),
                   pl.BlockSpec((None, inner, SSM_STATE), lambda bi, ci: (bi, 0, 0))],
        out_shape=[jax.ShapeDtypeStruct((bsz * seq, inner), F32),
                   jax.ShapeDtypeStruct((bsz, inner, SSM_STATE), F32)],
        scratch_shapes=[pltpu.VMEM((SSM_GROUPS, SSM_STATE, inner // SSM_GROUPS), F32)],
        compiler_params=_cparams(2, 48),
        name="ssd_scan",
    )(*args)


def _gated_out_kernel(y_ref, z_ref, g_ref, w_ref, r_ref, o_ref):
    z = z_ref[...]
    y = y_ref[...] * (z * jax.nn.sigmoid(z))
    yn = _rms(y, g_ref[...]).astype(BF16)
    o_ref[...] = r_ref[...] + jnp.dot(yn, w_ref[...], preferred_element_type=F32)


def gated_norm_out(y, proj, g, w, res):
    t, inner = y.shape
    n = w.shape[1]
    tm = _tile(t, (256, 128))
    return pl.pallas_call(
        _gated_out_kernel,
        grid=(t // tm,),
        in_specs=[pl.BlockSpec((tm, inner), lambda i: (i, 0)),
                  pl.BlockSpec((tm, inner), lambda i: (i, 0)),
                  pl.BlockSpec((1, inner), lambda i: (0, 0)),
                  pl.BlockSpec((inner, n), lambda i: (0, 0)),
                  pl.BlockSpec((tm, n), lambda i: (i, 0))],
        out_specs=pl.BlockSpec((tm, n), lambda i: (i, 0)),
        out_shape=jax.ShapeDtypeStruct((t, n), F32),
        compiler_params=_cparams(1, 40),
        name="gated_norm_out",
    )(y, proj, g.reshape(1, inner), w, res)


def _attn_banded_kernel(q_ref, sink_ref, kp_ref, kc_ref, vp_ref, vc_ref, o_ref):
    tq = q_ref.shape[0]
    hd = ATTN_HEAD_DIM
    qpk = q_ref.shape[1] // hd // KV_HEADS
    scale = hd ** -0.5
    nk = WINDOW + tq
    back = WINDOW // ATTN_CHUNK
    kc = lax.broadcasted_iota(jnp.int32, (nk, tq), 0) // ATTN_CHUNK
    qc = lax.broadcasted_iota(jnp.int32, (nk, tq), 1) // ATTN_CHUNK
    first_kc = jnp.where(pl.program_id(1) == 0, back, 0)
    valid = (kc >= qc) & (kc <= qc + back) & (kc >= first_kc)
    bias = jnp.concatenate([jnp.where(valid, 0.0, -jnp.inf)] * qpk, axis=1)
    for pair in range(KV_HEADS // 2):
        cols = slice(pair * 2 * hd, (pair + 1) * 2 * hd)
        k2 = jnp.concatenate([kp_ref[:, cols], kc_ref[:, cols]], axis=0)
        v2t = jnp.concatenate([vp_ref[:, cols], vc_ref[:, cols]], axis=0).T
        for sub in range(2):
            kh = pair * 2 + sub
            heads = range(kh * qpk, (kh + 1) * qpk)
            kmat = k2[:, sub * hd:(sub + 1) * hd].astype(BF16)
            vt = v2t[sub * hd:(sub + 1) * hd, :].astype(BF16)
            q4 = jnp.concatenate([q_ref[:, h * hd:(h + 1) * hd] for h in heads], axis=0).astype(BF16)
            s = lax.dot_general(kmat, q4, _NT, preferred_element_type=F32) * scale + bias
            sink = jnp.concatenate([jnp.broadcast_to(sink_ref[:, h:h + 1], (1, tq)) for h in heads], axis=1)
            m = jnp.maximum(jnp.max(s, axis=0, keepdims=True), sink)
            p = jnp.exp(s - m)
            den = jnp.sum(p, axis=0, keepdims=True) + jnp.exp(sink - m)
            ot = jnp.dot(vt, (p * (1.0 / den)).astype(BF16), preferred_element_type=F32)
            for g in range(0, qpk, 2):
                two = jnp.concatenate([ot[:, g * tq:(g + 1) * tq], ot[:, (g + 1) * tq:(g + 2) * tq]], axis=0)
                o_ref[:, (kh * qpk + g) * hd:(kh * qpk + g + 2) * hd] = two.T


def _attn_rows_kernel(q_ref, sink_ref, k_ref, v_ref, o_ref):
    tq = q_ref.shape[0]
    hd = ATTN_HEAD_DIM
    qpk = q_ref.shape[1] // hd // KV_HEADS
    scale = hd ** -0.5
    outs = []
    for kh in range(KV_HEADS):
        kmat = k_ref[:, kh * hd:(kh + 1) * hd].astype(BF16)
        vmat = v_ref[:, kh * hd:(kh + 1) * hd].astype(BF16)
        heads = range(kh * qpk, (kh + 1) * qpk)
        q4 = jnp.concatenate([q_ref[:, h * hd:(h + 1) * hd] for h in heads], axis=0).astype(BF16)
        s = lax.dot_general(q4, kmat, _NT, preferred_element_type=F32) * scale
        sink = jnp.concatenate([jnp.broadcast_to(sink_ref[:, h:h + 1], (tq, 1)) for h in heads], axis=0)
        m = jnp.maximum(jnp.max(s, axis=-1, keepdims=True), sink)
        p = jnp.exp(s - m)
        den = jnp.sum(p, axis=-1, keepdims=True) + jnp.exp(sink - m)
        o4 = jnp.dot((p / den).astype(BF16), vmat, preferred_element_type=F32)
        outs += [o4[g * tq:(g + 1) * tq] for g in range(qpk)]
    o_ref[...] = jnp.concatenate(outs, axis=1)


def swa_attention(q, sinks, k_src, v_src, k_col, v_col, bsz, seq, kv_rows):
    t, hqd = q.shape
    kvd = KV_HEADS * ATTN_HEAD_DIM
    n_heads = sinks.shape[0]
    if kv_rows is None:
        body = _attn_banded_kernel
        tq = _tile(seq, (256, WINDOW))
        nq = seq // tq
        per = tq // WINDOW
        kv_specs = lambda col: [
            pl.BlockSpec((WINDOW, kvd), lambda bi, ci: ((bi * nq + ci) * per - jnp.minimum(ci, 1), col)),
            pl.BlockSpec((tq, kvd), lambda bi, ci: (bi * nq + ci, col))]
    else:
        body = _attn_rows_kernel
        tq, nq = seq, 1
        kv_specs = lambda col: [pl.BlockSpec((kv_rows, kvd), lambda bi, ci: (bi, col))]
    k_specs, v_specs = kv_specs(k_col), kv_specs(v_col)
    in_specs = ([pl.BlockSpec((tq, hqd), lambda bi, ci: (bi * nq + ci, 0)),
                 pl.BlockSpec((1, LANES), lambda bi, ci: (0, 0))] + k_specs + v_specs)
    return pl.pallas_call(
        body,
        grid=(bsz, nq),
        in_specs=in_specs,
        out_specs=pl.BlockSpec((tq, hqd), lambda bi, ci: (bi * nq + ci, 0)),
        out_shape=jax.ShapeDtypeStruct((t, hqd), F32),
        compiler_params=_cparams(2, 40),
        name="swa_attention",
    )(q, jnp.pad(sinks, (0, LANES - n_heads)).reshape(1, LANES),
      *([k_src] * len(k_specs)), *([v_src] * len(v_specs)))


_PAIRS = [(a, b) for a in range(PEER_TOPK) for b in range(PEER_TOPK) if (a + 1) * (b + 1) <= PEER_TOPK]
_CAND_ROWS = -(-len(_PAIRS) // SUBLANES) * SUBLANES
_MARK_BASE = -(2.0 ** 127)
_MARK_STEP = 2.0 ** 115
_MARK_CUT = -1.5 * 2.0 ** 126
_CAND_PAD = -(2.0 ** 126)


def _extract_topk(s, n_rounds, break_ties):
    n = s.shape[0]
    iota = lax.broadcasted_iota(jnp.int32, s.shape, 0)
    vals = []
    for r in range(n_rounds):
        m = jnp.max(s, axis=0, keepdims=True)
        sel = s == m
        if break_ties:
            sel = iota == jnp.min(jnp.where(sel, iota, n), axis=0, keepdims=True)
        s = jnp.where(sel, _MARK_BASE + r * _MARK_STEP, s)
        vals.append(m)
    extracted = s < _MARK_CUT
    rank = jnp.where(extracted, (s - _MARK_BASE) * (1.0 / _MARK_STEP), float(n_rounds))
    count = jnp.sum(jnp.where(extracted, 1.0, 0.0), axis=0, keepdims=True)
    return rank, vals, count


def _peer_topk_kernel(q_ref, k1_ref, k2_ref, e1_ref, c1_ref, e2_ref, r2_ref,
                      s_ref, v_ref, cand_ref, sel_ref, tie_ref, rank_ref):
    half = k1_ref.shape[1]
    nh, tm = q_ref.shape[0], q_ref.shape[1]

    for h in range(nh):
        qh = q_ref[h]
        s_ref[2 * h] = lax.dot_general(k1_ref[...], qh[:, :half], _NT, preferred_element_type=F32)
        s_ref[2 * h + 1] = lax.dot_general(k2_ref[...], qh[:, half:], _NT, preferred_element_type=F32)

    def halves(h, break_ties):
        s1_ref, s2_ref = s_ref.at[2 * h], s_ref.at[2 * h + 1]
        tied = []
        for lc in range(tm // LANES):
            lanes = slice(lc * LANES, (lc + 1) * LANES)
            rank1, v1, n1 = _extract_topk(s1_ref[:, lanes], PEER_TOPK, break_ties)
            rank2, v2, n2 = _extract_topk(s2_ref[:, lanes], PEER_TOPK, break_ties)
            rank_ref[h, :, lanes] = rank1
            r2_ref[h, lanes, :] = rank2.T
            for r in range(PEER_TOPK):
                v_ref[2 * h, r:r + 1, lanes] = v1[r]
                v_ref[2 * h + 1, r:r + 1, lanes] = v2[r]
            tied.append(jnp.where((n1 != PEER_TOPK) | (n2 != PEER_TOPK), 1.0, 0.0))
        return jnp.concatenate(tied, axis=1)

    def joint(h, break_ties):
        s1_ref, s2_ref = s_ref.at[2 * h], s_ref.at[2 * h + 1]
        v1_ref, v2_ref = v_ref.at[2 * h], v_ref.at[2 * h + 1]
        cand_ref[...] = jnp.full((_CAND_ROWS, tm), _CAND_PAD, F32)
        for i, (a, b) in enumerate(_PAIRS):
            cand_ref[i:i + 1, :] = v1_ref[a:a + 1, :] + v2_ref[b:b + 1, :]
        cand = cand_ref[...]
        crank, _, nc = _extract_topk(cand, PEER_TOPK, break_ties)
        chosen = crank < PEER_TOPK
        z = 2.0 * jnp.sum(jnp.where(chosen, jnp.exp(cand - cand[0:1]), 0.0), axis=0, keepdims=True)
        sel_ref[...] = jnp.where(chosen, 1.0, 0.0)
        cnts = []
        row = 0
        for a in range(PEER_TOPK):
            nb = sum(1 for (pa, _) in _PAIRS if pa == a)
            cnts.append(jnp.sum(sel_ref[row:row + nb, :], axis=0, keepdims=True))
            row += nb
        for lc in range(tm // LANES):
            lanes = slice(lc * LANES, (lc + 1) * LANES)
            rank1 = rank_ref[h, :, lanes]
            c1 = jnp.zeros(rank1.shape, F32)
            for a in range(PEER_TOPK):
                c1 = jnp.where(rank1 == a, cnts[a][:, lanes], c1)
            c1_ref[h, lanes, :] = c1.T
            e1_ref[h, lanes, :] = jnp.exp(s1_ref[:, lanes] - v1_ref[0:1, lanes]).T
            e2_ref[h, lanes, :] = (jnp.exp(s2_ref[:, lanes] - v2_ref[0:1, lanes]) / z[:, lanes]).T
        return jnp.where(nc != PEER_TOPK, 1.0, 0.0)

    tied = [halves(0, break_ties=False)]
    for h in range(1, nh):
        tied.append(halves(h, break_ties=False))
        tied[h - 1] = jnp.maximum(tied[h - 1], joint(h - 1, break_ties=False))
    tied[nh - 1] = jnp.maximum(tied[nh - 1], joint(nh - 1, break_ties=False))
    tie_ref[...] = jnp.concatenate(tied, axis=0)

    def exact(h, carry):
        @pl.when(jnp.max(tie_ref[pl.ds(h, 1), :]) > 0.0)
        def _():
            halves(h, break_ties=True)
            joint(h, break_ties=True)

        return carry

    @pl.when(jnp.max(tie_ref[...]) > 0.0)
    def _():
        lax.fori_loop(0, nh, exact, 0)


def peer_topk(qh, k1, k2):
    nh, t, qd = qh.shape
    nk = k1.shape[0]
    tm = _tile(t, (2 * LANES, LANES))
    out32 = jax.ShapeDtypeStruct((nh, t, nk), F32)
    ospec = pl.BlockSpec((nh, tm, nk), lambda i: (0, i, 0))
    return pl.pallas_call(
        _peer_topk_kernel,
        grid=(t // tm,),
        in_specs=[pl.BlockSpec((nh, tm, qd), lambda i: (0, i, 0)),
                  pl.BlockSpec((nk, qd // 2), lambda i: (0, 0)),
                  pl.BlockSpec((nk, qd // 2), lambda i: (0, 0))],
        out_specs=[ospec] * 4,
        out_shape=[out32] * 4,
        scratch_shapes=[pltpu.VMEM((2 * nh, nk, tm), F32), pltpu.VMEM((2 * nh, PEER_TOPK, tm), F32),
                        pltpu.VMEM((_CAND_ROWS, tm), F32), pltpu.VMEM((_CAND_ROWS, tm), F32),
                        pltpu.VMEM((nh, tm), F32), pltpu.VMEM((nh, nk, tm), F32)],
        compiler_params=_cparams(1, 32),
        name="peer_topk",
    )(qh, k1, k2)


def _peer_dense_kernel(h_ref, g_ref, u_ref, v_ref, e1_ref, c1_ref, e2_ref, r2_ref, o_ref,
                       xn_ref, acc_ref, gate_ref):
    ki = pl.program_id(1)
    nh, tm, nkeys = e2_ref.shape
    te = u_ref.shape[0]
    half_tm = tm // 2

    @pl.when(ki == 0)
    def _():
        xn_ref[...] = _rms(h_ref[...], g_ref[...]).astype(BF16)
        acc_ref[...] = jnp.zeros(acc_ref.shape, F32)
        level = (lax.broadcasted_iota(jnp.int32, (PEER_TOPK, nkeys), 0) + 1).astype(F32)

        def token_factors(t):
            lhs, rhs = [], []
            for h in range(nh):
                row = pl.ds(t, 1)
                lhs.append(jnp.where(c1_ref[h, row, :] == level, e1_ref[h, row, :], 0.0))
                rhs.append(jnp.where(r2_ref[h, row, :] < level, e2_ref[h, row, :], 0.0))
            lhs = jnp.concatenate(lhs, axis=0).astype(BF16)
            rhs = jnp.concatenate(rhs, axis=0).astype(BF16)
            return lhs, rhs

        def pair_gates(t):
            lhs0, rhs0 = token_factors(t)
            lhs1, rhs1 = token_factors(t + 1)
            blank = jnp.zeros(rhs0.shape, BF16)
            lhs = jnp.concatenate([lhs0, lhs1], axis=0)
            rhs = jnp.concatenate([jnp.concatenate([rhs0, blank], axis=1),
                                   jnp.concatenate([blank, rhs1], axis=1)], axis=0)
            both = lax.dot_general(lhs, rhs, (((0,), (0,)), ((), ())), preferred_element_type=F32)
            return both[:, :nkeys], both[:, nkeys:]

        def bf16_bits(g):
            return lax.bitcast_convert_type(g.astype(BF16).astype(F32), jnp.uint32)

        def token_pairs(i, carry):
            for j in range(_GATE_PAIRS_PER_TRIP):
                p = i * _GATE_PAIRS_PER_TRIP + j
                even, odd = (bf16_bits(g) for g in pair_gates(2 * p))
                words = (odd & jnp.uint32(0xFFFF0000)) | (even >> 16)
                gate_ref[pl.ds(pl.multiple_of(p * nkeys, nkeys), nkeys), :] = words
            return carry

        lax.fori_loop(0, half_tm // _GATE_PAIRS_PER_TRIP, token_pairs, 0)

    x = lax.dot_general(xn_ref[...], u_ref[...], _NT, preferred_element_type=F32).astype(BF16)
    act2 = x + x * jnp.tanh(x * (_GELU_C1 + _GELU_C3 * (x * x)))
    pieces = []
    for r in range(te // nkeys):
        key1 = ki * (te // nkeys) + r
        gate = pltpu.bitcast(gate_ref[pl.ds(key1, half_tm, stride=nkeys), :], BF16)
        pieces.append(gate * act2[:, r * nkeys:(r + 1) * nkeys])
    wact = jnp.concatenate(pieces, axis=1)
    acc_ref[...] += jnp.dot(wact, v_ref[...], preferred_element_type=F32)

    @pl.when(ki == pl.num_programs(1) - 1)
    def _():
        o_ref[...] = h_ref[...] + acc_ref[...]


def peer_dense(h, g, u, v, layer, e1, c1, e2, r2):
    t, d = h.shape
    ne = u.shape[1]
    nh, _, nk = e1.shape
    tm = _tile(t, PEER_ROW_TILES)
    te = PEER_EXPERT_TILE
    key_spec = pl.BlockSpec((nh, tm, nk), lambda i, k: (0, i, 0))
    return pl.pallas_call(
        _peer_dense_kernel,
        grid=(t // tm, ne // te),
        in_specs=[pl.BlockSpec((tm, d), lambda i, k: (i, 0)),
                  pl.BlockSpec((1, d), lambda i, k: (0, 0)),
                  pl.BlockSpec((None, te, d), lambda i, k: (layer, k, 0)),
                  pl.BlockSpec((None, te, d), lambda i, k: (layer, k, 0)),
                  key_spec, key_spec, key_spec, key_spec],
        out_specs=pl.BlockSpec((tm, d), lambda i, k: (i, 0)),
        out_shape=jax.ShapeDtypeStruct((t, d), F32),
        scratch_shapes=[pltpu.VMEM((tm, d), BF16), pltpu.VMEM((tm, d), F32),
                        pltpu.VMEM((nk * tm // 2, nk), jnp.uint32)],
        compiler_params=_cparams(2, 56),
        name="peer_dense",
    )(h, g.reshape(1, d), u, v, e1, c1, e2, r2)


def kernel(x_prompt, x_sample, state_ssm, state_conv, cache_k_win, cache_v_win, norm_mix, norm_ffn, norm_kv, norm_final, m_w_in, m_conv_w, m_conv_b, m_dt_bias, m_a_log, m_d_skip, m_norm, m_w_out, a_w_kv, a_b_kv, a_w_q, a_b_q, a_sinks, a_w_o, a_b_o, p_w_q, p_sub_k1, p_sub_k2, p_u, p_v):
    depth = norm_mix.shape[0]
    n_a = m_w_in.shape[0]
    d = x_prompt.shape[-1]
    inner = m_w_out.shape[1]
    conv_dim = m_conv_w.shape[2]
    heads = m_d_skip.shape[1]
    kvd = a_w_kv.shape[1] // 2
    qdim = p_w_q.shape[2] // PEER_HEADS
    dt_col = inner + conv_dim
    in_pad = dt_col + LANES

    w_in = jnp.pad(m_w_in, ((0, 0), (0, 0), (0, in_pad - m_w_in.shape[2]))).astype(BF16)
    w_out = m_w_out.astype(BF16)
    w_kv = a_w_kv.astype(BF16)
    w_q = a_w_q.astype(BF16)
    w_o = a_w_o.astype(BF16)
    pw_q = p_w_q.astype(BF16)
    k1 = p_sub_k1.astype(BF16)
    k2 = p_sub_k2.astype(BF16)
    u_tab = p_u.astype(BF16)
    v_tab = p_v.astype(BF16)
    zeros_in = jnp.zeros((in_pad,), F32)
    zeros_pq = jnp.zeros((p_w_q.shape[2],), F32)
    in_tn = _tile(in_pad, (896, 128))

    def run(x, ssm_init, conv_init, k_prev, v_prev):
        bsz, seq, _ = x.shape
        h = x.reshape(bsz * seq, d)
        ssm_out, conv_out = [], []
        k_src = v_src = None
        for l in range(depth):
            if l < n_a:
                proj = norm_matmul(h, norm_mix[l], w_in[l], zeros_in, in_tn)
                if conv_init is None:
                    prev8 = jnp.zeros((bsz, SUBLANES, conv_dim), F32)
                else:
                    prev8 = jnp.pad(conv_init[l], ((0, 0), (SUBLANES - (CONV_WIDTH - 1), 0), (0, 0)))
                xbc = causal_conv_silu(proj, prev8, m_conv_w[l], m_conv_b[l], bsz, seq, inner)
                s0 = None if ssm_init is None else ssm_init[l].reshape(bsz, inner, SSM_STATE)
                y, s_new = ssd_scan(xbc, proj, dt_col, m_dt_bias[l], m_a_log[l], m_d_skip[l], s0, bsz, seq)
                ssm_out.append(s_new.reshape(bsz, heads, SSM_HEAD_DIM, SSM_STATE))
                tail = proj.reshape(bsz, seq, in_pad)[:, seq - (CONV_WIDTH - 1):, inner:dt_col]
                conv_out.append(tail)
                h = gated_norm_out(y, proj, m_norm[l], w_out[l], h)
            else:
                j = l - n_a
                if j == 0:
                    kv = norm_matmul(h, norm_kv, w_kv, a_b_kv, 2 * kvd)
                    kv3 = kv.reshape(bsz, seq, 2 * kvd)
                    if k_prev is not None:
                        k_all = jnp.concatenate([k_prev.reshape(bsz, WINDOW, kvd), kv3[:, :, :kvd]], axis=1)
                        v_all = jnp.concatenate([v_prev.reshape(bsz, WINDOW, kvd), kv3[:, :, kvd:]], axis=1)
                        k_src = k_all.reshape(bsz * (WINDOW + seq), kvd)
                        v_src = v_all.reshape(bsz * (WINDOW + seq), kvd)
                        k_win, v_win = k_all[:, -WINDOW:], v_all[:, -WINDOW:]
                    else:
                        k_win, v_win = kv3[:, -WINDOW:, :kvd], kv3[:, -WINDOW:, kvd:]
                qp = norm_matmul(h, norm_mix[l], w_q[j], a_b_q[j], w_q.shape[2])
                if k_prev is None:
                    o = swa_attention(qp, a_sinks[j], kv, kv, 0, 1, bsz, seq, None)
                else:
                    o = swa_attention(qp, a_sinks[j], k_src, v_src, 0, 0, bsz, seq, WINDOW + seq)
                h = matmul_bias_res(o, w_o[j], a_b_o[j], h)
            qh = norm_matmul(h, norm_ffn[l], pw_q[l], zeros_pq, pw_q.shape[2], head_dim=qdim)
            e1, c1, e2, r2 = peer_topk(qh, k1[l], k2[l])
            h = peer_dense(h, norm_ffn[l], u_tab, v_tab, l, e1, c1, e2, r2)
        y = rmsnorm_rows(h, norm_final).reshape(bsz, seq, d)
        kshape = (bsz, WINDOW, KV_HEADS, ATTN_HEAD_DIM)
        return y, jnp.stack(ssm_out), jnp.stack(conv_out), k_win.reshape(kshape), v_win.reshape(kshape)

    y_p, p_ssm, p_conv, p_k, p_v_win = run(x_prompt, None, None, None, None)
    y_s, s_ssm, s_conv, s_k, s_v = run(x_sample, state_ssm, state_conv, cache_k_win, cache_v_win)
    return (y_p, y_s, p_ssm, p_conv, p_k, p_v_win, s_ssm, s_conv, s_k, s_v)
```
